```python
import jax, jax.numpy as jnp
from jax import lax
import numpy as np

D_MODEL = 1024
BATCH = 8
SEQ = 2048
DEPTH = 4
DEC_BATCH = 128
DEC_SEQ = 8
PAST_LEN = 16384
PAGE_SIZE = 128

MIX_DIM = D_MODEL
CONV_DIM = MIX_DIM // 2
CONV_GROUPS = 8
CONV_WIDTH = 3
RWKV_DIM = MIX_DIM - CONV_DIM
HEAD_DIM = 64
RWKV_HEADS = RWKV_DIM // HEAD_DIM
LORA_DECAY = 64
LORA_A = 64
LORA_GATE = 128
RWKV_PROJ = 3 * RWKV_DIM + LORA_DECAY + LORA_A + LORA_GATE
IN_PROJ = 3 * CONV_DIM + RWKV_PROJ
D_FF = ((8 * D_MODEL // 3 + 255) // 256) * 256
RMS_EPS = 1e-6
GN_EPS = 64e-5

kernel_name = "hymba_style_shortconv_rwkv7_decoder_step"


def rms_norm(x, g):
    xf = x.astype(jnp.float32)
    y = xf * lax.rsqrt(jnp.mean(xf * xf, axis=-1, keepdims=True) + RMS_EPS)
    return (y * g.astype(jnp.float32)).astype(x.dtype)


def wkv7_scan(r, decay, k, v, kk, a, s0):
    def step(S, inp):
        r_t, d_t, k_t, v_t, kk_t, a_t = inp
        sa = jnp.einsum('bhvk,bhk->bhv', S, -kk_t)
        S = (S * d_t[:, :, None, :] + sa[..., None] * (kk_t * a_t)[:, :, None, :]
             + v_t[..., None] * k_t[:, :, None, :])
        y_t = jnp.einsum('bhvk,bhk->bhv', S, r_t)
        return S, y_t
    xs = tuple(jnp.swapaxes(t, 0, 1) for t in (r, decay, k, v, kk, a))
    s_fin, ys = lax.scan(step, s0, xs)
    return jnp.swapaxes(ys, 0, 1), s_fin


def hybrid_mixer(xn, shift_prev, conv_buf, wkv_state, w_in, mu_shift, conv_w, conv_norm,
                 w0, w_decay_up, a0, a_up, g_up, k_k, k_a, r_k, ln_x_w, ln_x_b, w_out):
    bsz, T, _ = xn.shape
    f32 = jnp.float32
    p = xn @ w_in
    h_c = p[..., :CONV_DIM]
    b_c = p[..., CONV_DIM:2 * CONV_DIM]
    c_c = p[..., 2 * CONV_DIM:3 * CONV_DIM]
    u = c_c * h_c
    u_ext = jnp.concatenate([conv_buf.astype(u.dtype), u], axis=1)
    conv = (u_ext[:, 0:T] * conv_w[0] + u_ext[:, 1:T + 1] * conv_w[1]
            + u_ext[:, 2:T + 2] * conv_w[2])
    new_conv = u_ext[:, T:]
    y_conv = rms_norm(b_c * conv, conv_norm)
    p_r = p[..., 3 * CONV_DIM:]
    p_prev = shift_prev.astype(xn.dtype) @ w_in[:, 3 * CONV_DIM:]
    p_shift = jnp.concatenate([p_prev[:, None], p_r[:, :-1]], axis=1)
    m = p_r + (p_shift - p_r) * mu_shift
    o1, o2, o3 = RWKV_DIM, 2 * RWKV_DIM, 3 * RWKV_DIM
    o4, o5 = o3 + LORA_DECAY, o3 + LORA_DECAY + LORA_A
    r, k, v = m[..., :o1], m[..., o1:o2], m[..., o2:o3]
    wd, ad, gd = m[..., o3:o4], m[..., o4:o5], m[..., o5:]
    w = -jax.nn.softplus(-(w0 + jnp.tanh(wd) @ w_decay_up)) - 0.5
    decay = jnp.exp(-jnp.exp(w.astype(f32)))
    a = jax.nn.sigmoid(a0 + ad @ a_up)
    g = jax.nn.sigmoid(gd) @ g_up
    hs = (bsz, T, RWKV_HEADS, HEAD_DIM)
    kk = (k * k_k).reshape(hs).astype(f32)
    kk = kk / jnp.maximum(jnp.linalg.norm(kk, axis=-1, keepdims=True), 1e-12)
    k = k * (1.0 + (a - 1.0) * k_a)
    rh = r.reshape(hs).astype(f32)
    kh = k.reshape(hs).astype(f32)
    vh = v.reshape(hs).astype(f32)
    ah = a.reshape(hs).astype(f32)
    y, s_new = wkv7_scan(rh, decay.reshape(hs), kh, vh, kk, ah, wkv_state.astype(f32))
    mu = jnp.mean(y, axis=-1, keepdims=True)
    var = jnp.mean(jnp.square(y - mu), axis=-1, keepdims=True)
    y = ((y - mu) * lax.rsqrt(var + GN_EPS)).reshape(bsz, T, RWKV_DIM)
    y = y * ln_x_w.astype(f32) + ln_x_b.astype(f32)
    bonus = jnp.sum(rh * kh * r_k.reshape(RWKV_HEADS, HEAD_DIM).astype(f32), axis=-1, keepdims=True) * vh
    y_rwkv = ((y + bonus.reshape(bsz, T, RWKV_DIM)) * g.astype(f32)).astype(xn.dtype)
    out = jnp.concatenate([y_conv, y_rwkv], axis=-1) @ w_out
    return out, xn[:, -1], new_conv, s_new.astype(wkv_state.dtype)


def swiglu(x, w_gate, w_up, w_down):
    return (jax.nn.silu(x @ w_gate) * (x @ w_up)) @ w_down


def trunk(x, shift0, conv0, wkv0, params):
    (norm1, w_in, mu_shift, conv_w, conv_norm, w0, w_decay_up, a0, a_up, g_up, k_k, k_a,
     r_k, ln_x_w, ln_x_b, w_out, norm2, w_gate, w_up, w_down, final_norm) = params
    shifts, convs, wkvs = [], [], []
    for l in range(DEPTH):
        xn = rms_norm(x, norm1[l])
        mix, s_shift, s_conv, s_wkv = hybrid_mixer(
            xn, shift0[l], conv0[l], wkv0[l], w_in[l], mu_shift[l], conv_w[l], conv_norm[l],
            w0[l], w_decay_up[l], a0[l], a_up[l], g_up[l], k_k[l], k_a[l], r_k[l],
            ln_x_w[l], ln_x_b[l], w_out[l])
        x = x + mix
        x = x + swiglu(rms_norm(x, norm2[l]), w_gate[l], w_up[l], w_down[l])
        shifts.append(s_shift)
        convs.append(s_conv)
        wkvs.append(s_wkv)
    return rms_norm(x, final_norm), jnp.stack(shifts), jnp.stack(convs), jnp.stack(wkvs)


def setup_inputs(seed: int = 0) -> dict:
    key = jax.random.key(seed)
    ks = jax.random.split(key, 32)
    f32 = jnp.float32
    nrm = lambda k, shape, s: (jax.random.normal(k, shape, f32) * s)
    L = DEPTH
    return {
        "x_prompt": nrm(ks[0], (BATCH, SEQ, D_MODEL), 1.0),
        "x_sample": nrm(ks[1], (DEC_BATCH, DEC_SEQ, D_MODEL), 1.0),
        "state_shift": nrm(ks[2], (L, DEC_BATCH, D_MODEL), 1.0),
        "state_conv": nrm(ks[3], (L, DEC_BATCH, CONV_WIDTH - 1, CONV_DIM), 1.0),
        "state_wkv": nrm(ks[4], (L, DEC_BATCH, RWKV_HEADS, HEAD_DIM, HEAD_DIM), 1.0),
        "norm1": 1.0 + nrm(ks[5], (L, D_MODEL), 0.02),
        "w_in": nrm(ks[6], (L, D_MODEL, IN_PROJ), D_MODEL ** -0.5),
        "mu_shift": jax.random.uniform(ks[7], (L, RWKV_PROJ), f32),
        "conv_w": nrm(ks[8], (L, CONV_WIDTH, CONV_DIM), CONV_WIDTH ** -0.5),
        "conv_norm": 1.0 + nrm(ks[9], (L, CONV_DIM), 0.02),
        "w0": jax.random.uniform(ks[10], (L, RWKV_DIM), f32, -5.0, -1.0),
        "w_decay_up": nrm(ks[11], (L, LORA_DECAY, RWKV_DIM), 0.5 * LORA_DECAY ** -0.5),
        "a0": nrm(ks[12], (L, RWKV_DIM), 0.1),
        "a_up": nrm(ks[13], (L, LORA_A, RWKV_DIM), LORA_A ** -0.5),
        "g_up": nrm(ks[14], (L, LORA_GATE, RWKV_DIM), LORA_GATE ** -0.5),
        "k_k": 0.85 + nrm(ks[15], (L, RWKV_DIM), 0.05),
        "k_a": 1.0 + nrm(ks[16], (L, RWKV_DIM), 0.05),
        "r_k": nrm(ks[17], (L, RWKV_DIM), 0.1),
        "ln_x_w": 1.0 + nrm(ks[18], (L, RWKV_DIM), 0.02),
        "ln_x_b": nrm(ks[19], (L, RWKV_DIM), 0.02),
        "w_out": nrm(ks[20], (L, MIX_DIM, D_MODEL), MIX_DIM ** -0.5),
        "norm2": 1.0 + nrm(ks[21], (L, D_MODEL), 0.02),
        "w_gate": nrm(ks[22], (L, D_MODEL, D_FF), D_MODEL ** -0.5),
        "w_up": nrm(ks[23], (L, D_MODEL, D_FF), D_MODEL ** -0.5),
        "w_down": nrm(ks[24], (L, D_FF, D_MODEL), D_FF ** -0.5),
        "final_norm": 1.0 + nrm(ks[25], (D_MODEL,), 0.02),
    }


def reference(x_prompt, x_sample, state_shift, state_conv, state_wkv, norm1, w_in, mu_shift,
              conv_w, conv_norm, w0, w_decay_up, a0, a_up, g_up, k_k, k_a, r_k, ln_x_w,
              ln_x_b, w_out, norm2, w_gate, w_up, w_down, final_norm):
    params = (norm1, w_in, mu_shift, conv_w, conv_norm, w0, w_decay_up, a0, a_up, g_up, k_k,
              k_a, r_k, ln_x_w, ln_x_b, w_out, norm2, w_gate, w_up, w_down, final_norm)
    dt = x_prompt.dtype
    p_shift0 = jnp.zeros((DEPTH, BATCH, D_MODEL), dt)
    p_conv0 = jnp.zeros((DEPTH, BATCH, CONV_WIDTH - 1, CONV_DIM), dt)
    p_wkv0 = jnp.zeros((DEPTH, BATCH, RWKV_HEADS, HEAD_DIM, HEAD_DIM), state_wkv.dtype)
    y_prompt, prompt_shift, prompt_conv, prompt_wkv = trunk(x_prompt, p_shift0, p_conv0, p_wkv0, params)
    y_sample, sample_shift, sample_conv, sample_wkv = trunk(x_sample, state_shift, state_conv, state_wkv, params)
    return (y_prompt, y_sample, prompt_shift, prompt_conv, prompt_wkv, sample_shift, sample_conv, sample_wkv)
```

```python
import functools

import jax
import jax.numpy as jnp
from jax import lax
from jax.experimental import pallas as pl
from jax.experimental.pallas import tpu as pltpu

D_MODEL = 1024
DEPTH = 4
CONV_DIM = 512
CONV_WIDTH = 3
RWKV_DIM = 512
HEAD_DIM = 64
RWKV_HEADS = RWKV_DIM // HEAD_DIM
LORA_DECAY = 64
LORA_A = 64
LORA_GATE = 128
RWKV_PROJ = 3 * RWKV_DIM + LORA_DECAY + LORA_A + LORA_GATE
IN_PROJ = 3 * CONV_DIM + RWKV_PROJ
D_FF = 2816
RMS_EPS = 1e-6
GN_EPS = 64e-5

LANES = 128
SUBLANES = 8
HEADS_PER_GROUP = LANES // HEAD_DIM
HEAD_GROUPS = RWKV_HEADS // HEADS_PER_GROUP
VMEM_LIMIT_BYTES = 56 * 1024 * 1024
ROW_TILE = 512
SEQ_BLOCK = SUBLANES
TIME_CHUNK = 64
PREP_TOKENS = 8
SPLIT_TERMS = 2

f32 = jnp.float32
bf16 = jnp.bfloat16


def _split_bf16(x, terms):
    parts = []
    rem = x
    for i in range(terms):
        part = rem.astype(bf16)
        parts.append(part)
        if i + 1 < terms:
            rem = rem - part.astype(f32)
    return parts


def _group_sum(x, ones_blk, terms):
    acc = None
    for part in _split_bf16(x, terms):
        d = jnp.dot(part, ones_blk, preferred_element_type=f32)
        acc = d if acc is None else acc + d
    return acc


def _group_sum_wide(x, ones_blk, terms):
    cols = [
        _group_sum(x[:, c * LANES:(c + 1) * LANES], ones_blk, terms)
        for c in range(x.shape[1] // LANES)
    ]
    return jnp.concatenate(cols, axis=1)


def _sigmoid(x):
    return 1.0 / (1.0 + jnp.exp(-x))


def _inproj_kernel(x_ref, g_ref, w_ref, o_ref, *, normalize):
    x = x_ref[...]
    if normalize:
        x = x * lax.rsqrt(jnp.mean(x * x, axis=-1, keepdims=True) + RMS_EPS) * g_ref[...]
    o_ref[...] = jnp.dot(x.astype(bf16), w_ref[...], preferred_element_type=f32)


def _inproj(x, g, w, *, normalize):
    n, d = x.shape
    m = w.shape[1]
    tile = min(ROW_TILE, n)
    return pl.pallas_call(
        functools.partial(_inproj_kernel, normalize=normalize),
        grid=(n // tile,),
        in_specs=[
            pl.BlockSpec((tile, d), lambda i: (i, 0)),
            pl.BlockSpec((1, d), lambda i: (0, 0)),
            pl.BlockSpec((d, m), lambda i: (0, 0)),
        ],
        out_specs=pl.BlockSpec((tile, m), lambda i: (i, 0)),
        out_shape=jax.ShapeDtypeStruct((n, m), f32),
        compiler_params=pltpu.CompilerParams(
            dimension_semantics=("arbitrary",), vmem_limit_bytes=VMEM_LIMIT_BYTES),
        name="inproj" if normalize else "shift_proj",
    )(x, g, w)


def _rmsnorm_kernel(x_ref, g_ref, o_ref):
    x = x_ref[...]
    o_ref[...] = x * lax.rsqrt(jnp.mean(x * x, axis=-1, keepdims=True) + RMS_EPS) * g_ref[...]


def _rmsnorm(x, g):
    n, d = x.shape
    tile = min(ROW_TILE, n)
    return pl.pallas_call(
        _rmsnorm_kernel,
        grid=(n // tile,),
        in_specs=[pl.BlockSpec((tile, d), lambda i: (i, 0)),
                  pl.BlockSpec((1, d), lambda i: (0, 0))],
        out_specs=pl.BlockSpec((tile, d), lambda i: (i, 0)),
        out_shape=jax.ShapeDtypeStruct((n, d), f32),
        compiler_params=pltpu.CompilerParams(dimension_semantics=("arbitrary",)),
        name="rmsnorm",
    )(x, g)


def _mixer_kernel(
        p_ref, pprev_ref, conv0_ref, wkv0_ref, ones_ref,
        mu_ref, convw_ref, convn_ref, w0_ref, wdu_ref, a0_ref, aup_ref, gup_ref,
        kk_ref, ka_ref, rk_ref, lnw_ref, lnb_ref,
        mix_ref, convout_ref, wkvout_ref,
        s_ref, cpr_ref, cu_ref,
        nkk_s, dr_s, d_s, b_s, k_s, v_s, br_s, kr_s, g_s, bonus_s, y_s,
        *, chunk, sub, terms):
    t_idx = pl.program_id(1)
    n_t = pl.num_programs(1)
    nb = SEQ_BLOCK
    rows = sub * nb
    n_sub = chunk // sub

    @pl.when(t_idx == 0)
    def _():
        s_ref[...] = wkv0_ref[...]
        cpr_ref[...] = pprev_ref[...]
        cu_ref[...] = conv0_ref[...]

    ones_blk = ones_ref[...]

    def flat(ref, t0, lo, hi):
        return ref[pl.ds(t0, sub), :, lo:hi].reshape(rows, hi - lo)

    def unflat(x):
        return x.reshape(sub, nb, x.shape[-1])

    def conv_input(t0):
        return flat(p_ref, t0, 2 * CONV_DIM, 3 * CONV_DIM) * flat(p_ref, t0, 0, CONV_DIM)

    def prep_block(t0, first):
        blk = pl.ds(t0, sub)
        bc = flat(p_ref, t0, CONV_DIM, 2 * CONV_DIM)
        u = conv_input(t0)
        if first:
            um2, um1 = cu_ref[0], cu_ref[1]
            u1 = jnp.concatenate([um1, u[:rows - nb]], axis=0)
            u2 = jnp.concatenate([um2, um1, u[:rows - 2 * nb]], axis=0)
        else:
            u1 = conv_input(t0 - 1)
            u2 = conv_input(t0 - 2)
        conv = u2 * convw_ref[0:1, :] + u1 * convw_ref[1:2, :] + u * convw_ref[2:3, :]
        yc = bc * conv
        yc = yc * lax.rsqrt(jnp.mean(yc * yc, axis=-1, keepdims=True) + RMS_EPS) * convn_ref[...]
        mix_ref[blk, :, 0:CONV_DIM] = unflat(yc)

        pr = flat(p_ref, t0, 3 * CONV_DIM, IN_PROJ)
        if first:
            shifted = jnp.concatenate([cpr_ref[...], pr[:rows - nb]], axis=0)
        else:
            shifted = flat(p_ref, t0 - 1, 3 * CONV_DIM, IN_PROJ)
        m = pr + (shifted - pr) * mu_ref[...]
        o1, o2, o3 = RWKV_DIM, 2 * RWKV_DIM, 3 * RWKV_DIM
        r = m[:, :o1]
        k = m[:, o1:o2]
        v = m[:, o2:o3]
        wa_d = m[:, o3:o3 + LORA_DECAY + LORA_A]
        g_d = m[:, o3 + LORA_DECAY + LORA_A:]
        z = w0_ref[...] + jnp.dot(jnp.tanh(wa_d).astype(bf16), wdu_ref[...],
                                  preferred_element_type=f32)
        nz = -z
        softplus = jnp.maximum(nz, 0.0) + jnp.log1p(jnp.exp(-jnp.abs(nz)))
        w = -softplus - 0.5
        d = jnp.exp(-jnp.exp(w))
        a = _sigmoid(a0_ref[...] + jnp.dot(wa_d.astype(bf16), aup_ref[...],
                                           preferred_element_type=f32))
        g = jnp.dot(_sigmoid(g_d).astype(bf16), gup_ref[...], preferred_element_type=f32)
        kk = k * kk_ref[...]
        kk_norm = jnp.sqrt(_group_sum_wide(kk * kk, ones_blk, 3))
        kk = kk / jnp.maximum(kk_norm, 1e-12)
        k = k * (1.0 + (a - 1.0) * ka_ref[...])
        b = kk * a
        nkk_s[blk] = unflat(-kk)
        dr_s[blk] = unflat(d * r)
        d_s[blk] = unflat(d)
        b_s[blk] = unflat(b)
        k_s[blk] = unflat(k)
        v_s[blk] = unflat(v)
        br_s[blk] = unflat(_group_sum_wide(b * r, ones_blk, 3))
        kr_s[blk] = unflat(_group_sum_wide(k * r, ones_blk, 3))
        g_s[blk] = unflat(g)
        bonus_s[blk] = unflat(_group_sum_wide(r * k * rk_ref[...], ones_blk, 3) * v)

    prep_block(0, True)
    if n_sub > 1:
        def prep_body(i, carry):
            prep_block(i * sub, False)
            return carry
        lax.fori_loop(1, n_sub, prep_body, 0)

    cu_ref[...] = (p_ref[chunk - 2:chunk, :, 2 * CONV_DIM:3 * CONV_DIM]
                   * p_ref[chunk - 2:chunk, :, 0:CONV_DIM])
    cpr_ref[...] = p_ref[chunk - 1, :, 3 * CONV_DIM:IN_PROJ]

    sub_i = lax.broadcasted_iota(jnp.int32, (HEAD_DIM, LANES), 0)
    lane_i = lax.broadcasted_iota(jnp.int32, (HEAD_DIM, LANES), 1)
    diag = (lane_i % HEAD_DIM) == sub_i

    def step(t, carry):
        for bi in range(nb):
            for j in range(HEAD_GROUPS):
                cols = slice(j * LANES, (j + 1) * LANES)

                def rowv(ref):
                    return ref[t, bi:bi + 1, cols]

                s = s_ref[bi, j]
                v_row = rowv(v_s)
                sa = _group_sum(s * rowv(nkk_s), ones_blk, terms)
                zz = _group_sum(s * rowv(dr_s), ones_blk, terms)
                vb = _group_sum(jnp.where(diag, v_row, 0.0), ones_blk, terms)
                s_ref[bi, j] = s * rowv(d_s) + sa * rowv(b_s) + vb * rowv(k_s)
                yp = jnp.where(diag, zz + sa * rowv(br_s), 0.0)
                y_row = jnp.sum(yp, axis=0, keepdims=True) + v_row * rowv(kr_s)
                y_s[t, bi:bi + 1, cols] = y_row
        return carry

    lax.fori_loop(0, chunk, step, 0)

    def post_body(i, carry):
        blk = pl.ds(i * sub, sub)
        y = y_s[blk].reshape(rows, RWKV_DIM)
        mean = _group_sum_wide(y, ones_blk, 3) * (1.0 / HEAD_DIM)
        yc = y - mean
        var = _group_sum_wide(yc * yc, ones_blk, 3) * (1.0 / HEAD_DIM)
        yn = yc * lax.rsqrt(var + GN_EPS)
        yn = yn * lnw_ref[...] + lnb_ref[...]
        out = (yn + bonus_s[blk].reshape(rows, RWKV_DIM)) * g_s[blk].reshape(rows, RWKV_DIM)
        mix_ref[blk, :, CONV_DIM:CONV_DIM + RWKV_DIM] = unflat(out)
        return carry

    lax.fori_loop(0, n_sub, post_body, 0)

    @pl.when(t_idx == n_t - 1)
    def _():
        convout_ref[...] = cu_ref[...]
        wkvout_ref[...] = s_ref[...]


def _mixer(p, p_prev, conv0, wkv0, ones_blk, lp):
    t_len, bsz, _ = p.shape
    nb = SEQ_BLOCK
    chunk = min(TIME_CHUNK, t_len)
    sub = min(PREP_TOKENS, chunk)
    assert sub >= CONV_WIDTH - 1 and chunk % sub == 0 and t_len % chunk == 0 and bsz % nb == 0
    grid = (bsz // nb, t_len // chunk)
    const = lambda shape: pl.BlockSpec(shape, lambda i, t: (0,) * len(shape))
    in_specs = [
        pl.BlockSpec((chunk, nb, IN_PROJ), lambda i, t: (t, i, 0)),
        pl.BlockSpec((nb, RWKV_PROJ), lambda i, t: (i, 0)),
        pl.BlockSpec((CONV_WIDTH - 1, nb, CONV_DIM), lambda i, t: (0, i, 0)),
        pl.BlockSpec((nb, HEAD_GROUPS, HEAD_DIM, LANES), lambda i, t: (i, 0, 0, 0)),
        const((LANES, LANES)),
        const((1, RWKV_PROJ)),
        const((CONV_WIDTH, CONV_DIM)),
        const((1, CONV_DIM)),
        const((1, RWKV_DIM)),
        const((LORA_DECAY + LORA_A, RWKV_DIM)),
        const((1, RWKV_DIM)),
        const((LORA_DECAY + LORA_A, RWKV_DIM)),
        const((LORA_GATE, RWKV_DIM)),
        const((1, RWKV_DIM)),
        const((1, RWKV_DIM)),
        const((1, RWKV_DIM)),
        const((1, RWKV_DIM)),
        const((1, RWKV_DIM)),
    ]
    out_specs = [
        pl.BlockSpec((chunk, nb, D_MODEL), lambda i, t: (t, i, 0)),
        pl.BlockSpec((CONV_WIDTH - 1, nb, CONV_DIM), lambda i, t: (0, i, 0)),
        pl.BlockSpec((nb, HEAD_GROUPS, HEAD_DIM, LANES), lambda i, t: (i, 0, 0, 0)),
    ]
    out_shape = [
        jax.ShapeDtypeStruct((t_len, bsz, D_MODEL), f32),
        jax.ShapeDtypeStruct((CONV_WIDTH - 1, bsz, CONV_DIM), f32),
        jax.ShapeDtypeStruct((bsz, HEAD_GROUPS, HEAD_DIM, LANES), f32),
    ]
    scratch = [
        pltpu.VMEM((nb, HEAD_GROUPS, HEAD_DIM, LANES), f32),
        pltpu.VMEM((nb, RWKV_PROJ), f32),
        pltpu.VMEM((CONV_WIDTH - 1, nb, CONV_DIM), f32),
    ] + [pltpu.VMEM((chunk, nb, RWKV_DIM), f32) for _ in range(11)]
    return pl.pallas_call(
        functools.partial(_mixer_kernel, chunk=chunk, sub=sub, terms=SPLIT_TERMS),
        grid=grid,
        in_specs=in_specs,
        out_specs=out_specs,
        out_shape=out_shape,
        scratch_shapes=scratch,
        compiler_params=pltpu.CompilerParams(
            dimension_semantics=("arbitrary", "arbitrary"),
            vmem_limit_bytes=VMEM_LIMIT_BYTES),
        name="mixer",
    )(p, p_prev, conv0, wkv0, ones_blk,
      lp["mu"], lp["conv_w"], lp["conv_norm"], lp["w0"], lp["wdu"], lp["a0"], lp["aup"],
      lp["gup"], lp["k_k"], lp["k_a"], lp["r_k"], lp["ln_w"], lp["ln_b"])


def _ffn_kernel(x_ref, mix_ref, wout_ref, g2_ref, wg_ref, wu_ref, wd_ref, o_ref):
    x = x_ref[...] + jnp.dot(mix_ref[...].astype(bf16), wout_ref[...],
                             preferred_element_type=f32)
    h = x * lax.rsqrt(jnp.mean(x * x, axis=-1, keepdims=True) + RMS_EPS) * g2_ref[...]
    hb = h.astype(bf16)
    gate = jnp.dot(hb, wg_ref[...], preferred_element_type=f32)
    up = jnp.dot(hb, wu_ref[...], preferred_element_type=f32)
    act = gate * _sigmoid(gate) * up
    o_ref[...] = x + jnp.dot(act.astype(bf16), wd_ref[...], preferred_element_type=f32)


def _ffn(x, mix, lp):
    n, d = x.shape
    tile = min(ROW_TILE, n)
    resident = lambda shape: pl.BlockSpec(shape, lambda i: (0, 0),
                                          pipeline_mode=pl.Buffered(1))
    return pl.pallas_call(
        _ffn_kernel,
        grid=(n // tile,),
        in_specs=[
            pl.BlockSpec((tile, d), lambda i: (i, 0)),
            pl.BlockSpec((tile, d), lambda i: (i, 0)),
            resident((d, d)),
            resident((1, d)),
            resident((d, D_FF)),
            resident((d, D_FF)),
            resident((D_FF, d)),
        ],
        out_specs=pl.BlockSpec((tile, d), lambda i: (i, 0)),
        out_shape=jax.ShapeDtypeStruct((n, d), f32),
        compiler_params=pltpu.CompilerParams(
            dimension_semantics=("arbitrary",), vmem_limit_bytes=VMEM_LIMIT_BYTES),
        name="outproj_ffn",
    )(x, mix, lp["w_out"], lp["norm2"], lp["w_gate"], lp["w_up"], lp["w_down"])


def _pack_wkv(s):
    b = s.shape[0]
    s = s.reshape(b, HEAD_GROUPS, HEADS_PER_GROUP, HEAD_DIM, HEAD_DIM)
    return s.transpose(0, 1, 3, 2, 4).reshape(b, HEAD_GROUPS, HEAD_DIM, LANES)


def _unpack_wkv(s):
    b = s.shape[0]
    s = s.reshape(b, HEAD_GROUPS, HEAD_DIM, HEADS_PER_GROUP, HEAD_DIM)
    return s.transpose(0, 1, 3, 2, 4).reshape(b, RWKV_HEADS, HEAD_DIM, HEAD_DIM)


def _prepare_params(norm1, w_in, mu_shift, conv_w, conv_norm, w0, w_decay_up, a0, a_up, g_up,
                    k_k, k_a, r_k, ln_x_w, ln_x_b, w_out, norm2, w_gate, w_up, w_down,
                    final_norm):
    row = lambda a: a.reshape(1, -1)
    zeros_lora = jnp.zeros((LORA_DECAY, RWKV_DIM), f32)
    layers = []
    for l in range(DEPTH):
        w_in_b = w_in[l].astype(bf16)
        layers.append(dict(
            norm1=row(norm1[l]), w_in=w_in_b, w_in_rwkv=w_in_b[:, 3 * CONV_DIM:],
            mu=row(mu_shift[l]), conv_w=conv_w[l], conv_norm=row(conv_norm[l]),
            w0=row(w0[l]),
            wdu=jnp.concatenate([w_decay_up[l], zeros_lora], axis=0).astype(bf16),
            a0=row(a0[l]),
            aup=jnp.concatenate([zeros_lora, a_up[l]], axis=0).astype(bf16),
            gup=g_up[l].astype(bf16),
            k_k=row(k_k[l]), k_a=row(k_a[l]), r_k=row(r_k[l]),
            ln_w=row(ln_x_w[l]), ln_b=row(ln_x_b[l]),
            w_out=w_out[l].astype(bf16), norm2=row(norm2[l]),
            w_gate=w_gate[l].astype(bf16), w_up=w_up[l].astype(bf16),
            w_down=w_down[l].astype(bf16)))
    lane = jnp.arange(LANES) // HEAD_DIM
    ones_blk = (lane[:, None] == lane[None, :]).astype(bf16)
    return layers, row(final_norm), ones_blk


def _trunk(x, shift0, conv0, wkv0, prepared):
    layers, final_norm, ones_blk = prepared
    bsz, t_len, d = x.shape
    xf = x.transpose(1, 0, 2).reshape(t_len * bsz, d)
    shifts, convs, wkvs = [], [], []
    for l, lp in enumerate(layers):
        p = _inproj(xf, lp["norm1"], lp["w_in"], normalize=True)
        shifts.append(_rmsnorm(xf[(t_len - 1) * bsz:], lp["norm1"]))
        p_prev = _inproj(shift0[l], lp["norm1"], lp["w_in_rwkv"], normalize=False)
        mix, conv_new, wkv_new = _mixer(
            p.reshape(t_len, bsz, IN_PROJ), p_prev, conv0[l].transpose(1, 0, 2),
            _pack_wkv(wkv0[l]), ones_blk, lp)
        xf = _ffn(xf, mix.reshape(t_len * bsz, d), lp)
        convs.append(conv_new.transpose(1, 0, 2))
        wkvs.append(_unpack_wkv(wkv_new))
    y = _rmsnorm(xf, final_norm).reshape(t_len, bsz, d).transpose(1, 0, 2)
    return y, jnp.stack(shifts), jnp.stack(convs), jnp.stack(wkvs)


def kernel(x_prompt, x_sample, state_shift, state_conv, state_wkv, norm1, w_in, mu_shift, conv_w, conv_norm, w0, w_decay_up, a0, a_up, g_up, k_k, k_a, r_k, ln_x_w, ln_x_b, w_out, norm2, w_gate, w_up, w_down, final_norm):
    prepared = _prepare_params(norm1, w_in, mu_shift, conv_w, conv_norm, w0, w_decay_up, a0,
                               a_up, g_up, k_k, k_a, r_k, ln_x_w, ln_x_b, w_out, norm2,
                               w_gate, w_up, w_down, final_norm)
    bp = x_prompt.shape[0]
    dt = x_prompt.dtype
    p_shift0 = jnp.zeros((DEPTH, bp, D_MODEL), dt)
    p_conv0 = jnp.zeros((DEPTH, bp, CONV_WIDTH - 1, CONV_DIM), dt)
    p_wkv0 = jnp.zeros((DEPTH, bp, RWKV_HEADS, HEAD_DIM, HEAD_DIM), state_wkv.dtype)
    y_p, sh_p, cv_p, wk_p = _trunk(x_prompt, p_shift0, p_conv0, p_wkv0, prepared)
    y_s, sh_s, cv_s, wk_s = _trunk(x_sample, state_shift, state_conv, state_wkv, prepared)
    return (y_p, y_s, sh_p, cv_p, wk_p, sh_s, cv_s, wk_s)
```

```python
import functools

import jax
import jax.numpy as jnp
from jax import lax
from jax.experimental import pallas as pl
from jax.experimental.pallas import tpu as pltpu

D_MODEL = 1024
DEPTH = 4
CONV_DIM = 512
CONV_WIDTH = 3
RWKV_DIM = 512
HEAD_DIM = 64
RWKV_HEADS = RWKV_DIM // HEAD_DIM
LORA_DECAY = 64
LORA_A = 64
LORA_GATE = 128
RWKV_PROJ = 3 * RWKV_DIM + LORA_DECAY + LORA_A + LORA_GATE
IN_PROJ = 3 * CONV_DIM + RWKV_PROJ
D_FF = 2816
RMS_EPS = 1e-6
GN_EPS = 64e-5

LANES = 128
SUBLANES = 8
HEADS_PER_GROUP = LANES // HEAD_DIM
HEAD_GROUPS = RWKV_HEADS // HEADS_PER_GROUP
VMEM_LIMIT_BYTES = 56 * 1024 * 1024
ROW_TILE = 512
SEQ_BLOCK = SUBLANES
TIME_CHUNK = 64
PREP_TOKENS = 8
STEP_SEQS = 2

f32 = jnp.float32
bf16 = jnp.bfloat16


def _split_bf16(x, terms):
    parts = []
    rem = x
    for i in range(terms):
        part = rem.astype(bf16)
        parts.append(part)
        if i + 1 < terms:
            rem = rem - part.astype(f32)
    return parts


def _group_sum(x, ones_blk, terms):
    acc = None
    for part in _split_bf16(x, terms):
        d = jnp.dot(part, ones_blk, preferred_element_type=f32)
        acc = d if acc is None else acc + d
    return acc


def _group_sum_wide(x, ones_blk, terms):
    cols = [
        _group_sum(x[:, c * LANES:(c + 1) * LANES], ones_blk, terms)
        for c in range(x.shape[1] // LANES)
    ]
    return jnp.concatenate(cols, axis=1)


def _bf16_part(x):
    bits = lax.bitcast_convert_type(x, jnp.uint32) & jnp.uint32(0xFFFF0000)
    return lax.bitcast_convert_type(bits, f32)


def _sigmoid(x):
    return 1.0 / (1.0 + jnp.exp(-x))


def _inproj_kernel(x_ref, g_ref, w_ref, o_ref, *, normalize):
    x = x_ref[...]
    if normalize:
        x = x * lax.rsqrt(jnp.mean(x * x, axis=-1, keepdims=True) + RMS_EPS) * g_ref[...]
    o_ref[...] = jnp.dot(x.astype(bf16), w_ref[...], preferred_element_type=f32)


def _inproj(x, g, w, *, normalize):
    n, d = x.shape
    m = w.shape[1]
    tile = min(ROW_TILE, n)
    return pl.pallas_call(
        functools.partial(_inproj_kernel, normalize=normalize),
        grid=(n // tile,),
        in_specs=[
            pl.BlockSpec((tile, d), lambda i: (i, 0)),
            pl.BlockSpec((1, d), lambda i: (0, 0)),
            pl.BlockSpec((d, m), lambda i: (0, 0)),
        ],
        out_specs=pl.BlockSpec((tile, m), lambda i: (i, 0)),
        out_shape=jax.ShapeDtypeStruct((n, m), f32),
        compiler_params=pltpu.CompilerParams(
            dimension_semantics=("arbitrary",), vmem_limit_bytes=VMEM_LIMIT_BYTES),
        name="inproj" if normalize else "shift_proj",
    )(x, g, w)


def _rmsnorm_kernel(x_ref, g_ref, o_ref):
    x = x_ref[...]
    o_ref[...] = x * lax.rsqrt(jnp.mean(x * x, axis=-1, keepdims=True) + RMS_EPS) * g_ref[...]


def _rmsnorm(x, g):
    n, d = x.shape
    tile = min(ROW_TILE, n)
    return pl.pallas_call(
        _rmsnorm_kernel,
        grid=(n // tile,),
        in_specs=[pl.BlockSpec((tile, d), lambda i: (i, 0)),
                  pl.BlockSpec((1, d), lambda i: (0, 0))],
        out_specs=pl.BlockSpec((tile, d), lambda i: (i, 0)),
        out_shape=jax.ShapeDtypeStruct((n, d), f32),
        compiler_params=pltpu.CompilerParams(dimension_semantics=("arbitrary",)),
        name="rmsnorm",
    )(x, g)


def _mixer_kernel(
        p_ref, pprev_ref, conv0_ref, wkv0_ref, ones_ref, pair_ref,
        mu_ref, convw_ref, convn_ref, w0_ref, wdu_ref, a0_ref, aup_ref, gup_ref,
        kk_ref, ka_ref, rk_ref, lnw_ref, lnb_ref,
        mix_ref, convout_ref, wkvout_ref,
        s_ref, cpr_ref, cu_ref,
        nkk_s, dr_s, d_s, b_s, k_s, vhi_s, vlo_s, vkr_s, g_s, bonus_s, y_s,
        *, chunk, sub):
    t_idx = pl.program_id(1)
    n_t = pl.num_programs(1)
    nb = SEQ_BLOCK
    rows = sub * nb
    n_sub = chunk // sub

    @pl.when(t_idx == 0)
    def _():
        s_ref[...] = wkv0_ref[...]
        cpr_ref[...] = pprev_ref[...]
        cu_ref[...] = conv0_ref[...]

    ones_blk = ones_ref[...]

    def flat(ref, t0, lo, hi):
        return ref[pl.ds(t0, sub), :, lo:hi].reshape(rows, hi - lo)

    def unflat(x):
        return x.reshape(sub, nb, x.shape[-1])

    def conv_input(t0):
        return flat(p_ref, t0, 2 * CONV_DIM, 3 * CONV_DIM) * flat(p_ref, t0, 0, CONV_DIM)

    def prep_block(t0, first):
        blk = pl.ds(t0, sub)
        bc = flat(p_ref, t0, CONV_DIM, 2 * CONV_DIM)
        u = conv_input(t0)
        if first:
            um2, um1 = cu_ref[0], cu_ref[1]
            u1 = jnp.concatenate([um1, u[:rows - nb]], axis=0)
            u2 = jnp.concatenate([um2, um1, u[:rows - 2 * nb]], axis=0)
        else:
            u1 = conv_input(t0 - 1)
            u2 = conv_input(t0 - 2)
        conv = u2 * convw_ref[0:1, :] + u1 * convw_ref[1:2, :] + u * convw_ref[2:3, :]
        yc = bc * conv
        yc = yc * lax.rsqrt(jnp.mean(yc * yc, axis=-1, keepdims=True) + RMS_EPS) * convn_ref[...]
        mix_ref[blk, :, 0:CONV_DIM] = unflat(yc)

        pr = flat(p_ref, t0, 3 * CONV_DIM, IN_PROJ)
        if first:
            shifted = jnp.concatenate([cpr_ref[...], pr[:rows - nb]], axis=0)
        else:
            shifted = flat(p_ref, t0 - 1, 3 * CONV_DIM, IN_PROJ)
        m = pr + (shifted - pr) * mu_ref[...]
        o1, o2, o3 = RWKV_DIM, 2 * RWKV_DIM, 3 * RWKV_DIM
        r = m[:, :o1]
        k = m[:, o1:o2]
        v = m[:, o2:o3]
        wa_d = m[:, o3:o3 + LORA_DECAY + LORA_A]
        g_d = m[:, o3 + LORA_DECAY + LORA_A:]
        z = w0_ref[...] + jnp.dot(jnp.tanh(wa_d).astype(bf16), wdu_ref[...],
                                  preferred_element_type=f32)
        nz = -z
        softplus = jnp.maximum(nz, 0.0) + jnp.log1p(jnp.exp(-jnp.abs(nz)))
        w = -softplus - 0.5
        d = jnp.exp(-jnp.exp(w))
        a = _sigmoid(a0_ref[...] + jnp.dot(wa_d.astype(bf16), aup_ref[...],
                                           preferred_element_type=f32))
        g = jnp.dot(_sigmoid(g_d).astype(bf16), gup_ref[...], preferred_element_type=f32)
        kk = k * kk_ref[...]
        kk_norm = jnp.sqrt(_group_sum_wide(kk * kk, ones_blk, 3))
        kk = kk / jnp.maximum(kk_norm, 1e-12)
        k = k * (1.0 + (a - 1.0) * ka_ref[...])
        b = kk * a
        nkk_s[blk] = unflat(-kk)
        dr_s[blk] = unflat(d * r - kk * _group_sum_wide(b * r, ones_blk, 3))
        d_s[blk] = unflat(d)
        b_s[blk] = unflat(b)
        k_s[blk] = unflat(k)
        v_hi = _bf16_part(v)
        vhi_s[blk] = unflat(v_hi)
        vlo_s[blk] = unflat(v - v_hi)
        vkr_s[blk] = unflat(_group_sum_wide(k * r, ones_blk, 3) * v)
        g_s[blk] = unflat(g)
        bonus_s[blk] = unflat(_group_sum_wide(r * k * rk_ref[...], ones_blk, 3) * v)

    prep_block(0, True)
    if n_sub > 1:
        def prep_body(i, carry):
            prep_block(i * sub, False)
            return carry
        lax.fori_loop(1, n_sub, prep_body, 0)

    cu_ref[...] = (p_ref[chunk - 2:chunk, :, 2 * CONV_DIM:3 * CONV_DIM]
                   * p_ref[chunk - 2:chunk, :, 0:CONV_DIM])
    cpr_ref[...] = p_ref[chunk - 1, :, 3 * CONV_DIM:IN_PROJ]

    sub_i = lax.broadcasted_iota(jnp.int32, (HEAD_DIM, LANES), 0)
    lane_i = lax.broadcasted_iota(jnp.int32, (HEAD_DIM, LANES), 1)
    diag = (lane_i % HEAD_DIM) == sub_i

    pair_w = pair_ref[...]

    def step(t, carry):
        for b0 in range(0, nb, STEP_SEQS):
            streams = [(bi, j) for bi in range(b0, b0 + STEP_SEQS) for j in range(HEAD_GROUPS)]

            def rowv(ref, bi, j):
                return ref[t, bi:bi + 1, j * LANES:(j + 1) * LANES]

            hi_rows, lo_rows = [], []
            for bi, j in streams:
                s = s_ref[bi, j]
                p = jnp.concatenate([s * rowv(nkk_s, bi, j), s * rowv(dr_s, bi, j)], axis=1)
                p_hi = _bf16_part(p)
                hi_rows.append(p_hi)
                lo_rows.append(p - p_hi)
            for n in range(0, len(streams), 2):
                hi_rows.append(jnp.concatenate(
                    [jnp.where(diag, rowv(vhi_s, *streams[n + c]), 0.0) for c in range(2)], axis=1))
                lo_rows.append(jnp.concatenate(
                    [jnp.where(diag, rowv(vlo_s, *streams[n + c]), 0.0) for c in range(2)], axis=1))
            lhs = jnp.concatenate(
                [jnp.concatenate(hi_rows, axis=0), jnp.concatenate(lo_rows, axis=0)], axis=1)
            res = jnp.dot(lhs, pair_w, preferred_element_type=f32)
            for n, (bi, j) in enumerate(streams):
                sa = res[n * HEAD_DIM:(n + 1) * HEAD_DIM, 0:LANES]
                zz = res[n * HEAD_DIM:(n + 1) * HEAD_DIM, LANES:2 * LANES]
                vrow0 = (len(streams) + n // 2) * HEAD_DIM
                vb = res[vrow0:vrow0 + HEAD_DIM, (n % 2) * LANES:(n % 2 + 1) * LANES]
                s = s_ref[bi, j]
                s_ref[bi, j] = s * rowv(d_s, bi, j) + sa * rowv(b_s, bi, j) + vb * rowv(k_s, bi, j)
                yp = jnp.where(diag, zz, 0.0)
                y_row = jnp.sum(yp, axis=0, keepdims=True) + rowv(vkr_s, bi, j)
                y_s[t, bi:bi + 1, j * LANES:(j + 1) * LANES] = y_row
        return carry

    lax.fori_loop(0, chunk, step, 0)

    def post_body(i, carry):
        blk = pl.ds(i * sub, sub)
        y = y_s[blk].reshape(rows, RWKV_DIM)
        mean = _group_sum_wide(y, ones_blk, 3) * (1.0 / HEAD_DIM)
        yc = y - mean
        var = _group_sum_wide(yc * yc, ones_blk, 3) * (1.0 / HEAD_DIM)
        yn = yc * lax.rsqrt(var + GN_EPS)
        yn = yn * lnw_ref[...] + lnb_ref[...]
        out = (yn + bonus_s[blk].reshape(rows, RWKV_DIM)) * g_s[blk].reshape(rows, RWKV_DIM)
        mix_ref[blk, :, CONV_DIM:CONV_DIM + RWKV_DIM] = unflat(out)
        return carry

    lax.fori_loop(0, n_sub, post_body, 0)

    @pl.when(t_idx == n_t - 1)
    def _():
        convout_ref[...] = cu_ref[...]
        wkvout_ref[...] = s_ref[...]


def _mixer(p, p_prev, conv0, wkv0, ones_blk, pair_w, lp):
    t_len, bsz, _ = p.shape
    nb = SEQ_BLOCK
    chunk = min(TIME_CHUNK, t_len)
    sub = min(PREP_TOKENS, chunk)
    assert sub >= CONV_WIDTH - 1 and chunk % sub == 0 and t_len % chunk == 0 and bsz % nb == 0
    grid = (bsz // nb, t_len // chunk)
    const = lambda shape: pl.BlockSpec(shape, lambda i, t: (0,) * len(shape))
    in_specs = [
        pl.BlockSpec((chunk, nb, IN_PROJ), lambda i, t: (t, i, 0)),
        pl.BlockSpec((nb, RWKV_PROJ), lambda i, t: (i, 0)),
        pl.BlockSpec((CONV_WIDTH - 1, nb, CONV_DIM), lambda i, t: (0, i, 0)),
        pl.BlockSpec((nb, HEAD_GROUPS, HEAD_DIM, LANES), lambda i, t: (i, 0, 0, 0)),
        const((LANES, LANES)),
        const((4 * LANES, 2 * LANES)),
        const((1, RWKV_PROJ)),
        const((CONV_WIDTH, CONV_DIM)),
        const((1, CONV_DIM)),
        const((1, RWKV_DIM)),
        const((LORA_DECAY + LORA_A, RWKV_DIM)),
        const((1, RWKV_DIM)),
        const((LORA_DECAY + LORA_A, RWKV_DIM)),
        const((LORA_GATE, RWKV_DIM)),
        const((1, RWKV_DIM)),
        const((1, RWKV_DIM)),
        const((1, RWKV_DIM)),
        const((1, RWKV_DIM)),
        const((1, RWKV_DIM)),
    ]
    out_specs = [
        pl.BlockSpec((chunk, nb, D_MODEL), lambda i, t: (t, i, 0)),
        pl.BlockSpec((CONV_WIDTH - 1, nb, CONV_DIM), lambda i, t: (0, i, 0)),
        pl.BlockSpec((nb, HEAD_GROUPS, HEAD_DIM, LANES), lambda i, t: (i, 0, 0, 0)),
    ]
    out_shape = [
        jax.ShapeDtypeStruct((t_len, bsz, D_MODEL), f32),
        jax.ShapeDtypeStruct((CONV_WIDTH - 1, bsz, CONV_DIM), f32),
        jax.ShapeDtypeStruct((bsz, HEAD_GROUPS, HEAD_DIM, LANES), f32),
    ]
    scratch = [
        pltpu.VMEM((nb, HEAD_GROUPS, HEAD_DIM, LANES), f32),
        pltpu.VMEM((nb, RWKV_PROJ), f32),
        pltpu.VMEM((CONV_WIDTH - 1, nb, CONV_DIM), f32),
    ] + [pltpu.VMEM((chunk, nb, RWKV_DIM), f32) for _ in range(11)]
    return pl.pallas_call(
        functools.partial(_mixer_kernel, chunk=chunk, sub=sub),
        grid=grid,
        in_specs=in_specs,
        out_specs=out_specs,
        out_shape=out_shape,
        scratch_shapes=scratch,
        compiler_params=pltpu.CompilerParams(
            dimension_semantics=("arbitrary", "arbitrary"),
            vmem_limit_bytes=VMEM_LIMIT_BYTES),
        name="mixer",
    )(p, p_prev, conv0, wkv0, ones_blk, pair_w,
      lp["mu"], lp["conv_w"], lp["conv_norm"], lp["w0"], lp["wdu"], lp["a0"], lp["aup"],
      lp["gup"], lp["k_k"], lp["k_a"], lp["r_k"], lp["ln_w"], lp["ln_b"])


def _ffn_kernel(x_ref, mix_ref, wout_ref, g2_ref, wg_ref, wu_ref, wd_ref, o_ref):
    x = x_ref[...] + jnp.dot(mix_ref[...].astype(bf16), wout_ref[...],
                             preferred_element_type=f32)
    h = x * lax.rsqrt(jnp.mean(x * x, axis=-1, keepdims=True) + RMS_EPS) * g2_ref[...]
    hb = h.astype(bf16)
    gate = jnp.dot(hb, wg_ref[...], preferred_element_type=f32)
    up = jnp.dot(hb, wu_ref[...], preferred_element_type=f32)
    act = gate * _sigmoid(gate) * up
    o_ref[...] = x + jnp.dot(act.astype(bf16), wd_ref[...], preferred_element_type=f32)


def _ffn(x, mix, lp):
    n, d = x.shape
    tile = min(ROW_TILE, n)
    resident = lambda shape: pl.BlockSpec(shape, lambda i: (0, 0),
                                          pipeline_mode=pl.Buffered(1))
    return pl.pallas_call(
        _ffn_kernel,
        grid=(n // tile,),
        in_specs=[
            pl.BlockSpec((tile, d), lambda i: (i, 0)),
            pl.BlockSpec((tile, d), lambda i: (i, 0)),
            resident((d, d)),
            resident((1, d)),
            resident((d, D_FF)),
            resident((d, D_FF)),
            resident((D_FF, d)),
        ],
        out_specs=pl.BlockSpec((tile, d), lambda i: (i, 0)),
        out_shape=jax.ShapeDtypeStruct((n, d), f32),
        compiler_params=pltpu.CompilerParams(
            dimension_semantics=("arbitrary",), vmem_limit_bytes=VMEM_LIMIT_BYTES),
        name="outproj_ffn",
    )(x, mix, lp["w_out"], lp["norm2"], lp["w_gate"], lp["w_up"], lp["w_down"])


def _pack_wkv(s):
    b = s.shape[0]
    s = s.reshape(b, HEAD_GROUPS, HEADS_PER_GROUP, HEAD_DIM, HEAD_DIM)
    return s.transpose(0, 1, 3, 2, 4).reshape(b, HEAD_GROUPS, HEAD_DIM, LANES)


def _unpack_wkv(s):
    b = s.shape[0]
    s = s.reshape(b, HEAD_GROUPS, HEAD_DIM, HEADS_PER_GROUP, HEAD_DIM)
    return s.transpose(0, 1, 3, 2, 4).reshape(b, RWKV_HEADS, HEAD_DIM, HEAD_DIM)


def _prepare_params(norm1, w_in, mu_shift, conv_w, conv_norm, w0, w_decay_up, a0, a_up, g_up,
                    k_k, k_a, r_k, ln_x_w, ln_x_b, w_out, norm2, w_gate, w_up, w_down,
                    final_norm):
    row = lambda a: a.reshape(1, -1)
    zeros_lora = jnp.zeros((LORA_DECAY, RWKV_DIM), f32)
    layers = []
    for l in range(DEPTH):
        w_in_b = w_in[l].astype(bf16)
        layers.append(dict(
            norm1=row(norm1[l]), w_in=w_in_b, w_in_rwkv=w_in_b[:, 3 * CONV_DIM:],
            mu=row(mu_shift[l]), conv_w=conv_w[l], conv_norm=row(conv_norm[l]),
            w0=row(w0[l]),
            wdu=jnp.concatenate([w_decay_up[l], zeros_lora], axis=0).astype(bf16),
            a0=row(a0[l]),
            aup=jnp.concatenate([zeros_lora, a_up[l]], axis=0).astype(bf16),
            gup=g_up[l].astype(bf16),
            k_k=row(k_k[l]), k_a=row(k_a[l]), r_k=row(r_k[l]),
            ln_w=row(ln_x_w[l]), ln_b=row(ln_x_b[l]),
            w_out=w_out[l].astype(bf16), norm2=row(norm2[l]),
            w_gate=w_gate[l].astype(bf16), w_up=w_up[l].astype(bf16),
            w_down=w_down[l].astype(bf16)))
    lane = jnp.arange(LANES) // HEAD_DIM
    ones_blk = (lane[:, None] == lane[None, :]).astype(bf16)
    zero = jnp.zeros((LANES, LANES), bf16)
    pair = jnp.block([[ones_blk, zero], [zero, ones_blk]])
    pair_w = jnp.concatenate([pair, pair], axis=0).astype(f32)
    return layers, row(final_norm), ones_blk, pair_w


def _trunk(x, shift0, conv0, wkv0, prepared):
    layers, final_norm, ones_blk, pair_w = prepared
    bsz, t_len, d = x.shape
    xf = x.transpose(1, 0, 2).reshape(t_len * bsz, d)
    shifts, convs, wkvs = [], [], []
    for l, lp in enumerate(layers):
        p = _inproj(xf, lp["norm1"], lp["w_in"], normalize=True)
        shifts.append(_rmsnorm(xf[(t_len - 1) * bsz:], lp["norm1"]))
        p_prev = _inproj(shift0[l], lp["norm1"], lp["w_in_rwkv"], normalize=False)
        mix, conv_new, wkv_new = _mixer(
            p.reshape(t_len, bsz, IN_PROJ), p_prev, conv0[l].transpose(1, 0, 2),
            _pack_wkv(wkv0[l]), ones_blk, pair_w, lp)
        xf = _ffn(xf, mix.reshape(t_len * bsz, d), lp)
        convs.append(conv_new.transpose(1, 0, 2))
        wkvs.append(_unpack_wkv(wkv_new))
    y = _rmsnorm(xf, final_norm).reshape(t_len, bsz, d).transpose(1, 0, 2)
    return y, jnp.stack(shifts), jnp.stack(convs), jnp.stack(wkvs)


def kernel(x_prompt, x_sample, state_shift, state_conv, state_wkv, norm1, w_in, mu_shift, conv_w, conv_norm, w0, w_decay_up, a0, a_up, g_up, k_k, k_a, r_k, ln_x_w, ln_x_b, w_out, norm2, w_gate, w_up, w_down, final_norm):
    prepared = _prepare_params(norm1, w_in, mu_shift, conv_w, conv_norm, w0, w_decay_up, a0,
                               a_up, g_up, k_k, k_a, r_k, ln_x_w, ln_x_b, w_out, norm2,
                               w_gate, w_up, w_down, final_norm)
    bp = x_prompt.shape[0]
    dt = x_prompt.dtype
    p_shift0 = jnp.zeros((DEPTH, bp, D_MODEL), dt)
    p_conv0 = jnp.zeros((DEPTH, bp, CONV_WIDTH - 1, CONV_DIM), dt)
    p_wkv0 = jnp.zeros((DEPTH, bp, RWKV_HEADS, HEAD_DIM, HEAD_DIM), state_wkv.dtype)
    y_p, sh_p, cv_p, wk_p = _trunk(x_prompt, p_shift0, p_conv0, p_wkv0, prepared)
    y_s, sh_s, cv_s, wk_s = _trunk(x_sample, state_shift, state_conv, state_wkv, prepared)
    return (y_p, y_s, sh_p, cv_p, wk_p, sh_s, cv_s, wk_s)
```

```python
import functools

import jax
import jax.numpy as jnp
from jax import lax
from jax.experimental import pallas as pl
from jax.experimental.pallas import tpu as pltpu

D_MODEL = 1024
DEPTH = 4
CONV_DIM = 512
CONV_WIDTH = 3
RWKV_DIM = 512
HEAD_DIM = 64
RWKV_HEADS = RWKV_DIM // HEAD_DIM
LORA_DECAY = 64
LORA_A = 64
LORA_GATE = 128
RWKV_PROJ = 3 * RWKV_DIM + LORA_DECAY + LORA_A + LORA_GATE
IN_PROJ = 3 * CONV_DIM + RWKV_PROJ
D_FF = 2816
RMS_EPS = 1e-6
GN_EPS = 64e-5

LANES = 128
SUBLANES = 8
HEADS_PER_GROUP = LANES // HEAD_DIM
HEAD_GROUPS = RWKV_HEADS // HEADS_PER_GROUP
VMEM_LIMIT_BYTES = 56 * 1024 * 1024
ROW_TILE = 512
SEQ_BLOCK = SUBLANES
TIME_CHUNK = 64
PREP_TOKENS = 8
STEP_SEQS = 2

f32 = jnp.float32
bf16 = jnp.bfloat16


def _split_bf16(x, terms):
    parts = []
    rem = x
    for i in range(terms):
        part = rem.astype(bf16)
        parts.append(part)
        if i + 1 < terms:
            rem = rem - part.astype(f32)
    return parts


def _group_sum(x, ones_blk, terms):
    acc = None
    for part in _split_bf16(x, terms):
        d = jnp.dot(part, ones_blk, preferred_element_type=f32)
        acc = d if acc is None else acc + d
    return acc


def _group_sum_wide(x, ones_blk, terms):
    cols = [
        _group_sum(x[:, c * LANES:(c + 1) * LANES], ones_blk, terms)
        for c in range(x.shape[1] // LANES)
    ]
    return jnp.concatenate(cols, axis=1)


def _sigmoid(x):
    return 1.0 / (1.0 + jnp.exp(-x))


def _inproj_kernel(x_ref, g_ref, w_ref, o_ref, *, normalize):
    x = x_ref[...]
    if normalize:
        x = x * lax.rsqrt(jnp.mean(x * x, axis=-1, keepdims=True) + RMS_EPS) * g_ref[...]
    o_ref[...] = jnp.dot(x.astype(bf16), w_ref[...], preferred_element_type=f32)


def _inproj(x, g, w, *, normalize):
    n, d = x.shape
    m = w.shape[1]
    tile = min(ROW_TILE, n)
    return pl.pallas_call(
        functools.partial(_inproj_kernel, normalize=normalize),
        grid=(n // tile,),
        in_specs=[
            pl.BlockSpec((tile, d), lambda i: (i, 0)),
            pl.BlockSpec((1, d), lambda i: (0, 0)),
            pl.BlockSpec((d, m), lambda i: (0, 0)),
        ],
        out_specs=pl.BlockSpec((tile, m), lambda i: (i, 0)),
        out_shape=jax.ShapeDtypeStruct((n, m), f32),
        compiler_params=pltpu.CompilerParams(
            dimension_semantics=("arbitrary",), vmem_limit_bytes=VMEM_LIMIT_BYTES),
        name="inproj" if normalize else "shift_proj",
    )(x, g, w)


def _rmsnorm_kernel(x_ref, g_ref, o_ref):
    x = x_ref[...]
    o_ref[...] = x * lax.rsqrt(jnp.mean(x * x, axis=-1, keepdims=True) + RMS_EPS) * g_ref[...]


def _rmsnorm(x, g):
    n, d = x.shape
    tile = min(ROW_TILE, n)
    return pl.pallas_call(
        _rmsnorm_kernel,
        grid=(n // tile,),
        in_specs=[pl.BlockSpec((tile, d), lambda i: (i, 0)),
                  pl.BlockSpec((1, d), lambda i: (0, 0))],
        out_specs=pl.BlockSpec((tile, d), lambda i: (i, 0)),
        out_shape=jax.ShapeDtypeStruct((n, d), f32),
        compiler_params=pltpu.CompilerParams(dimension_semantics=("arbitrary",)),
        name="rmsnorm",
    )(x, g)


def _mixer_kernel(
        p_ref, pprev_ref, conv0_ref, wkv0_ref, ones_ref, pair_ref,
        mu_ref, convw_ref, convn_ref, w0_ref, wdu_ref, a0_ref, aup_ref, gup_ref,
        kk_ref, ka_ref, rk_ref, lnw_ref, lnb_ref,
        mix_ref, convout_ref, wkvout_ref,
        s_ref, cpr_ref, cu_ref,
        nkk_s, dr_s, d_s, b_s, k_s, vr_s, vkr_s, g_s, bonus_s, y_s,
        *, chunk, sub):
    t_idx = pl.program_id(1)
    n_t = pl.num_programs(1)
    nb = SEQ_BLOCK
    rows = sub * nb
    n_sub = chunk // sub

    @pl.when(t_idx == 0)
    def _():
        s_ref[...] = wkv0_ref[...]
        cpr_ref[...] = pprev_ref[...]
        cu_ref[...] = conv0_ref[...]

    ones_blk = ones_ref[...]

    def flat(ref, t0, lo, hi):
        return ref[pl.ds(t0, sub), :, lo:hi].reshape(rows, hi - lo)

    def unflat(x):
        return x.reshape(sub, nb, x.shape[-1])

    def conv_input(t0):
        return flat(p_ref, t0, 2 * CONV_DIM, 3 * CONV_DIM) * flat(p_ref, t0, 0, CONV_DIM)

    def prep_block(t0, first):
        blk = pl.ds(t0, sub)
        bc = flat(p_ref, t0, CONV_DIM, 2 * CONV_DIM)
        u = conv_input(t0)
        if first:
            um2, um1 = cu_ref[0], cu_ref[1]
            u1 = jnp.concatenate([um1, u[:rows - nb]], axis=0)
            u2 = jnp.concatenate([um2, um1, u[:rows - 2 * nb]], axis=0)
        else:
            u1 = conv_input(t0 - 1)
            u2 = conv_input(t0 - 2)
        conv = u2 * convw_ref[0:1, :] + u1 * convw_ref[1:2, :] + u * convw_ref[2:3, :]
        yc = bc * conv
        yc = yc * lax.rsqrt(jnp.mean(yc * yc, axis=-1, keepdims=True) + RMS_EPS) * convn_ref[...]
        mix_ref[blk, :, 0:CONV_DIM] = unflat(yc)

        pr = flat(p_ref, t0, 3 * CONV_DIM, IN_PROJ)
        if first:
            shifted = jnp.concatenate([cpr_ref[...], pr[:rows - nb]], axis=0)
        else:
            shifted = flat(p_ref, t0 - 1, 3 * CONV_DIM, IN_PROJ)
        m = pr + (shifted - pr) * mu_ref[...]
        o1, o2, o3 = RWKV_DIM, 2 * RWKV_DIM, 3 * RWKV_DIM
        r = m[:, :o1]
        k = m[:, o1:o2]
        v = m[:, o2:o3]
        wa_d = m[:, o3:o3 + LORA_DECAY + LORA_A]
        g_d = m[:, o3 + LORA_DECAY + LORA_A:]
        z = w0_ref[...] + jnp.dot(jnp.tanh(wa_d).astype(bf16), wdu_ref[...],
                                  preferred_element_type=f32)
        nz = -z
        softplus = jnp.maximum(nz, 0.0) + jnp.log1p(jnp.exp(-jnp.abs(nz)))
        w = -softplus - 0.5
        d = jnp.exp(-jnp.exp(w))
        a = _sigmoid(a0_ref[...] + jnp.dot(wa_d.astype(bf16), aup_ref[...],
                                           preferred_element_type=f32))
        g = jnp.dot(_sigmoid(g_d).astype(bf16), gup_ref[...], preferred_element_type=f32)
        kk = k * kk_ref[...]
        kk_norm = jnp.sqrt(_group_sum_wide(kk * kk, ones_blk, 3))
        kk = kk / jnp.maximum(kk_norm, 1e-12)
        k = k * (1.0 + (a - 1.0) * ka_ref[...])
        b = kk * a
        nkk_s[blk] = unflat(-kk)
        dr_s[blk] = unflat(d * r - kk * _group_sum_wide(b * r, ones_blk, 3))
        d_s[blk] = unflat(d)
        b_s[blk] = unflat(b)
        k_s[blk] = unflat(k)
        vr_s[blk] = unflat(v.astype(bf16).astype(f32))
        vkr_s[blk] = unflat(_group_sum_wide(k * r, ones_blk, 3) * v)
        g_s[blk] = unflat(g)
        bonus_s[blk] = unflat(_group_sum_wide(r * k * rk_ref[...], ones_blk, 3) * v)

    prep_block(0, True)
    if n_sub > 1:
        def prep_body(i, carry):
            prep_block(i * sub, False)
            return carry
        lax.fori_loop(1, n_sub, prep_body, 0)

    cu_ref[...] = (p_ref[chunk - 2:chunk, :, 2 * CONV_DIM:3 * CONV_DIM]
                   * p_ref[chunk - 2:chunk, :, 0:CONV_DIM])
    cpr_ref[...] = p_ref[chunk - 1, :, 3 * CONV_DIM:IN_PROJ]

    sub_i = lax.broadcasted_iota(jnp.int32, (HEAD_DIM, LANES), 0)
    lane_i = lax.broadcasted_iota(jnp.int32, (HEAD_DIM, LANES), 1)
    diag = (lane_i % HEAD_DIM) == sub_i

    pair_w = pair_ref[...]
    pair_b = pair_w.astype(bf16)

    def step(t, carry):
        for b0 in range(0, nb, STEP_SEQS):
            streams = [(bi, j) for bi in range(b0, b0 + STEP_SEQS) for j in range(HEAD_GROUPS)]

            def rowv(ref, bi, j):
                return ref[t, bi:bi + 1, j * LANES:(j + 1) * LANES]

            prod_rows = []
            for bi, j in streams:
                s = s_ref[bi, j]
                prod_rows.append(jnp.concatenate(
                    [(s * rowv(nkk_s, bi, j)).astype(bf16), (s * rowv(dr_s, bi, j)).astype(bf16)],
                    axis=1))
            sums = jnp.dot(jnp.concatenate(prod_rows, axis=0), pair_b,
                           preferred_element_type=f32)
            v_rows = []
            for n in range(0, len(streams), 2):
                v_rows.append(jnp.concatenate(
                    [jnp.where(diag, rowv(vr_s, *streams[n + c]), 0.0) for c in range(2)], axis=1))
            vbs = jnp.dot(jnp.concatenate(v_rows, axis=0), pair_w, preferred_element_type=f32)
            for n, (bi, j) in enumerate(streams):
                sa = sums[n * HEAD_DIM:(n + 1) * HEAD_DIM, 0:LANES]
                zz = sums[n * HEAD_DIM:(n + 1) * HEAD_DIM, LANES:2 * LANES]
                vrow0 = (n // 2) * HEAD_DIM
                vb = vbs[vrow0:vrow0 + HEAD_DIM, (n % 2) * LANES:(n % 2 + 1) * LANES]
                s = s_ref[bi, j]
                s_ref[bi, j] = s * rowv(d_s, bi, j) + sa * rowv(b_s, bi, j) + vb * rowv(k_s, bi, j)
                yp = jnp.where(diag, zz, 0.0)
                y_row = jnp.sum(yp, axis=0, keepdims=True) + rowv(vkr_s, bi, j)
                y_s[t, bi:bi + 1, j * LANES:(j + 1) * LANES] = y_row
        return carry

    lax.fori_loop(0, chunk, step, 0)

    def post_body(i, carry):
        blk = pl.ds(i * sub, sub)
        y = y_s[blk].reshape(rows, RWKV_DIM)
        mean = _group_sum_wide(y, ones_blk, 3) * (1.0 / HEAD_DIM)
        yc = y - mean
        var = _group_sum_wide(yc * yc, ones_blk, 3) * (1.0 / HEAD_DIM)
        yn = yc * lax.rsqrt(var + GN_EPS)
        yn = yn * lnw_ref[...] + lnb_ref[...]
        out = (yn + bonus_s[blk].reshape(rows, RWKV_DIM)) * g_s[blk].reshape(rows, RWKV_DIM)
        mix_ref[blk, :, CONV_DIM:CONV_DIM + RWKV_DIM] = unflat(out)
        return carry

    lax.fori_loop(0, n_sub, post_body, 0)

    @pl.when(t_idx == n_t - 1)
    def _():
        convout_ref[...] = cu_ref[...]
        wkvout_ref[...] = s_ref[...]


def _mixer(p, p_prev, conv0, wkv0, ones_blk, pair_w, lp):
    t_len, bsz, _ = p.shape
    nb = SEQ_BLOCK
    chunk = min(TIME_CHUNK, t_len)
    sub = min(PREP_TOKENS, chunk)
    assert sub >= CONV_WIDTH - 1 and chunk % sub == 0 and t_len % chunk == 0 and bsz % nb == 0
    grid = (bsz // nb, t_len // chunk)
    const = lambda shape: pl.BlockSpec(shape, lambda i, t: (0,) * len(shape))
    in_specs = [
        pl.BlockSpec((chunk, nb, IN_PROJ), lambda i, t: (t, i, 0)),
        pl.BlockSpec((nb, RWKV_PROJ), lambda i, t: (i, 0)),
        pl.BlockSpec((CONV_WIDTH - 1, nb, CONV_DIM), lambda i, t: (0, i, 0)),
        pl.BlockSpec((nb, HEAD_GROUPS, HEAD_DIM, LANES), lambda i, t: (i, 0, 0, 0)),
        const((LANES, LANES)),
        const((2 * LANES, 2 * LANES)),
        const((1, RWKV_PROJ)),
        const((CONV_WIDTH, CONV_DIM)),
        const((1, CONV_DIM)),
        const((1, RWKV_DIM)),
        const((LORA_DECAY + LORA_A, RWKV_DIM)),
        const((1, RWKV_DIM)),
        const((LORA_DECAY + LORA_A, RWKV_DIM)),
        const((LORA_GATE, RWKV_DIM)),
        const((1, RWKV_DIM)),
        const((1, RWKV_DIM)),
        const((1, RWKV_DIM)),
        const((1, RWKV_DIM)),
        const((1, RWKV_DIM)),
    ]
    out_specs = [
        pl.BlockSpec((chunk, nb, D_MODEL), lambda i, t: (t, i, 0)),
        pl.BlockSpec((CONV_WIDTH - 1, nb, CONV_DIM), lambda i, t: (0, i, 0)),
        pl.BlockSpec((nb, HEAD_GROUPS, HEAD_DIM, LANES), lambda i, t: (i, 0, 0, 0)),
    ]
    out_shape = [
        jax.ShapeDtypeStruct((t_len, bsz, D_MODEL), f32),
        jax.ShapeDtypeStruct((CONV_WIDTH - 1, bsz, CONV_DIM), f32),
        jax.ShapeDtypeStruct((bsz, HEAD_GROUPS, HEAD_DIM, LANES), f32),
    ]
    scratch = [
        pltpu.VMEM((nb, HEAD_GROUPS, HEAD_DIM, LANES), f32),
        pltpu.VMEM((nb, RWKV_PROJ), f32),
        pltpu.VMEM((CONV_WIDTH - 1, nb, CONV_DIM), f32),
    ] + [pltpu.VMEM((chunk, nb, RWKV_DIM), f32) for _ in range(10)]
    return pl.pallas_call(
        functools.partial(_mixer_kernel, chunk=chunk, sub=sub),
        grid=grid,
        in_specs=in_specs,
        out_specs=out_specs,
        out_shape=out_shape,
        scratch_shapes=scratch,
        compiler_params=pltpu.CompilerParams(
            dimension_semantics=("arbitrary", "arbitrary"),
            vmem_limit_bytes=VMEM_LIMIT_BYTES),
        name="mixer",
    )(p, p_prev, conv0, wkv0, ones_blk, pair_w,
      lp["mu"], lp["conv_w"], lp["conv_norm"], lp["w0"], lp["wdu"], lp["a0"], lp["aup"],
      lp["gup"], lp["k_k"], lp["k_a"], lp["r_k"], lp["ln_w"], lp["ln_b"])


def _ffn_kernel(x_ref, mix_ref, wout_ref, g2_ref, wg_ref, wu_ref, wd_ref, o_ref):
    x = x_ref[...] + jnp.dot(mix_ref[...].astype(bf16), wout_ref[...],
                             preferred_element_type=f32)
    h = x * lax.rsqrt(jnp.mean(x * x, axis=-1, keepdims=True) + RMS_EPS) * g2_ref[...]
    hb = h.astype(bf16)
    gate = jnp.dot(hb, wg_ref[...], preferred_element_type=f32)
    up = jnp.dot(hb, wu_ref[...], preferred_element_type=f32)
    act = gate * _sigmoid(gate) * up
    o_ref[...] = x + jnp.dot(act.astype(bf16), wd_ref[...], preferred_element_type=f32)


def _ffn(x, mix, lp):
    n, d = x.shape
    tile = min(ROW_TILE, n)
    resident = lambda shape: pl.BlockSpec(shape, lambda i: (0, 0),
                                          pipeline_mode=pl.Buffered(1))
    return pl.pallas_call(
        _ffn_kernel,
        grid=(n // tile,),
        in_specs=[
            pl.BlockSpec((tile, d), lambda i: (i, 0)),
            pl.BlockSpec((tile, d), lambda i: (i, 0)),
            resident((d, d)),
            resident((1, d)),
            resident((d, D_FF)),
            resident((d, D_FF)),
            resident((D_FF, d)),
        ],
        out_specs=pl.BlockSpec((tile, d), lambda i: (i, 0)),
        out_shape=jax.ShapeDtypeStruct((n, d), f32),
        compiler_params=pltpu.CompilerParams(
            dimension_semantics=("arbitrary",), vmem_limit_bytes=VMEM_LIMIT_BYTES),
        name="outproj_ffn",
    )(x, mix, lp["w_out"], lp["norm2"], lp["w_gate"], lp["w_up"], lp["w_down"])


def _pack_wkv(s):
    b = s.shape[0]
    s = s.reshape(b, HEAD_GROUPS, HEADS_PER_GROUP, HEAD_DIM, HEAD_DIM)
    return s.transpose(0, 1, 3, 2, 4).reshape(b, HEAD_GROUPS, HEAD_DIM, LANES)


def _unpack_wkv(s):
    b = s.shape[0]
    s = s.reshape(b, HEAD_GROUPS, HEAD_DIM, HEADS_PER_GROUP, HEAD_DIM)
    return s.transpose(0, 1, 3, 2, 4).reshape(b, RWKV_HEADS, HEAD_DIM, HEAD_DIM)


def _prepare_params(norm1, w_in, mu_shift, conv_w, conv_norm, w0, w_decay_up, a0, a_up, g_up,
                    k_k, k_a, r_k, ln_x_w, ln_x_b, w_out, norm2, w_gate, w_up, w_down,
                    final_norm):
    row = lambda a: a.reshape(1, -1)
    zeros_lora = jnp.zeros((LORA_DECAY, RWKV_DIM), f32)
    layers = []
    for l in range(DEPTH):
        w_in_b = w_in[l].astype(bf16)
        layers.append(dict(
            norm1=row(norm1[l]), w_in=w_in_b, w_in_rwkv=w_in_b[:, 3 * CONV_DIM:],
            mu=row(mu_shift[l]), conv_w=conv_w[l], conv_norm=row(conv_norm[l]),
            w0=row(w0[l]),
            wdu=jnp.concatenate([w_decay_up[l], zeros_lora], axis=0).astype(bf16),
            a0=row(a0[l]),
            aup=jnp.concatenate([zeros_lora, a_up[l]], axis=0).astype(bf16),
            gup=g_up[l].astype(bf16),
            k_k=row(k_k[l]), k_a=row(k_a[l]), r_k=row(r_k[l]),
            ln_w=row(ln_x_w[l]), ln_b=row(ln_x_b[l]),
            w_out=w_out[l].astype(bf16), norm2=row(norm2[l]),
            w_gate=w_gate[l].astype(bf16), w_up=w_up[l].astype(bf16),
            w_down=w_down[l].astype(bf16)))
    lane = jnp.arange(LANES) // HEAD_DIM
    ones_blk = (lane[:, None] == lane[None, :]).astype(bf16)
    zero = jnp.zeros((LANES, LANES), bf16)
    pair_w = jnp.block([[ones_blk, zero], [zero, ones_blk]]).astype(f32)
    return layers, row(final_norm), ones_blk, pair_w


def _trunk(x, shift0, conv0, wkv0, prepared):
    layers, final_norm, ones_blk, pair_w = prepared
    bsz, t_len, d = x.shape
    xf = x.transpose(1, 0, 2).reshape(t_len * bsz, d)
    shifts, convs, wkvs = [], [], []
    for l, lp in enumerate(layers):
        p = _inproj(xf, lp["norm1"], lp["w_in"], normalize=True)
        shifts.append(_rmsnorm(xf[(t_len - 1) * bsz:], lp["norm1"]))
        p_prev = _inproj(shift0[l], lp["norm1"], lp["w_in_rwkv"], normalize=False)
        mix, conv_new, wkv_new = _mixer(
            p.reshape(t_len, bsz, IN_PROJ), p_prev, conv0[l].transpose(1, 0, 2),
            _pack_wkv(wkv0[l]), ones_blk, pair_w, lp)
        xf = _ffn(xf, mix.reshape(t_len * bsz, d), lp)
        convs.append(conv_new.transpose(1, 0, 2))
        wkvs.append(_unpack_wkv(wkv_new))
    y = _rmsnorm(xf, final_norm).reshape(t_len, bsz, d).transpose(1, 0, 2)
    return y, jnp.stack(shifts), jnp.stack(convs), jnp.stack(wkvs)


def kernel(x_prompt, x_sample, state_shift, state_conv, state_wkv, norm1, w_in, mu_shift, conv_w, conv_norm, w0, w_decay_up, a0, a_up, g_up, k_k, k_a, r_k, ln_x_w, ln_x_b, w_out, norm2, w_gate, w_up, w_down, final_norm):
    prepared = _prepare_params(norm1, w_in, mu_shift, conv_w, conv_norm, w0, w_decay_up, a0,
                               a_up, g_up, k_k, k_a, r_k, ln_x_w, ln_x_b, w_out, norm2,
                               w_gate, w_up, w_down, final_norm)
    bp = x_prompt.shape[0]
    dt = x_prompt.dtype
    p_shift0 = jnp.zeros((DEPTH, bp, D_MODEL), dt)
    p_conv0 = jnp.zeros((DEPTH, bp, CONV_WIDTH - 1, CONV_DIM), dt)
    p_wkv0 = jnp.zeros((DEPTH, bp, RWKV_HEADS, HEAD_DIM, HEAD_DIM), state_wkv.dtype)
    y_p, sh_p, cv_p, wk_p = _trunk(x_prompt, p_shift0, p_conv0, p_wkv0, prepared)
    y_s, sh_s, cv_s, wk_s = _trunk(x_sample, state_shift, state_conv, state_wkv, prepared)
    return (y_p, y_s, sh_p, cv_p, wk_p, sh_s, cv_s, wk_s)
```

```python
import functools

import jax
import jax.numpy as jnp
from jax import lax
from jax.experimental import pallas as pl
from jax.experimental.pallas import tpu as pltpu

D_MODEL = 1024
DEPTH = 4
CONV_DIM = 512
CONV_WIDTH = 3
RWKV_DIM = 512
HEAD_DIM = 64
RWKV_HEADS = RWKV_DIM // HEAD_DIM
LORA_DECAY = 64
LORA_A = 64
LORA_GATE = 128
RWKV_PROJ = 3 * RWKV_DIM + LORA_DECAY + LORA_A + LORA_GATE
IN_PROJ = 3 * CONV_DIM + RWKV_PROJ
D_FF = 2816
RMS_EPS = 1e-6
GN_EPS = 64e-5

LANES = 128
SUBLANES = 8
HEADS_PER_GROUP = LANES // HEAD_DIM
HEAD_GROUPS = RWKV_HEADS // HEADS_PER_GROUP
VMEM_LIMIT_BYTES = 56 * 1024 * 1024
ROW_TILE = 512
SEQ_BLOCK = SUBLANES
TIME_CHUNK = 64
PREP_TOKENS = 8
STEP_SEQS = 2

f32 = jnp.float32
bf16 = jnp.bfloat16


def _split_bf16(x, terms):
    parts = []
    rem = x
    for i in range(terms):
        part = rem.astype(bf16)
        parts.append(part)
        if i + 1 < terms:
            rem = rem - part.astype(f32)
    return parts


def _group_sum(x, ones_blk, terms):
    acc = None
    for part in _split_bf16(x, terms):
        d = jnp.dot(part, ones_blk, preferred_element_type=f32)
        acc = d if acc is None else acc + d
    return acc


def _group_sum_wide(x, ones_blk, terms):
    cols = [
        _group_sum(x[:, c * LANES:(c + 1) * LANES], ones_blk, terms)
        for c in range(x.shape[1] // LANES)
    ]
    return jnp.concatenate(cols, axis=1)


def _sigmoid(x):
    return 1.0 / (1.0 + jnp.exp(-x))


def _inproj_kernel(x_ref, g_ref, w_ref, o_ref, *, normalize):
    x = x_ref[...]
    if normalize:
        x = x * lax.rsqrt(jnp.mean(x * x, axis=-1, keepdims=True) + RMS_EPS) * g_ref[...]
    o_ref[...] = jnp.dot(x.astype(bf16), w_ref[...], preferred_element_type=f32)


def _inproj(x, g, w, *, normalize):
    n, d = x.shape
    m = w.shape[1]
    tile = min(ROW_TILE, n)
    return pl.pallas_call(
        functools.partial(_inproj_kernel, normalize=normalize),
        grid=(n // tile,),
        in_specs=[
            pl.BlockSpec((tile, d), lambda i: (i, 0)),
            pl.BlockSpec((1, d), lambda i: (0, 0)),
            pl.BlockSpec((d, m), lambda i: (0, 0)),
        ],
        out_specs=pl.BlockSpec((tile, m), lambda i: (i, 0)),
        out_shape=jax.ShapeDtypeStruct((n, m), f32),
        compiler_params=pltpu.CompilerParams(
            dimension_semantics=("arbitrary",), vmem_limit_bytes=VMEM_LIMIT_BYTES),
        name="inproj" if normalize else "shift_proj",
    )(x, g, w)


def _rmsnorm_kernel(x_ref, g_ref, o_ref):
    x = x_ref[...]
    o_ref[...] = x * lax.rsqrt(jnp.mean(x * x, axis=-1, keepdims=True) + RMS_EPS) * g_ref[...]


def _rmsnorm(x, g):
    n, d = x.shape
    tile = min(ROW_TILE, n)
    return pl.pallas_call(
        _rmsnorm_kernel,
        grid=(n // tile,),
        in_specs=[pl.BlockSpec((tile, d), lambda i: (i, 0)),
                  pl.BlockSpec((1, d), lambda i: (0, 0))],
        out_specs=pl.BlockSpec((tile, d), lambda i: (i, 0)),
        out_shape=jax.ShapeDtypeStruct((n, d), f32),
        compiler_params=pltpu.CompilerParams(dimension_semantics=("arbitrary",)),
        name="rmsnorm",
    )(x, g)


def _mixer_kernel(
        p_ref, pprev_ref, conv0_ref, wkv0_ref, ones_ref, pair_ref,
        mu_ref, convw_ref, convn_ref, w0_ref, wdu_ref, a0_ref, aup_ref, gup_ref,
        kk_ref, ka_ref, rk_ref, lnw_ref, lnb_ref,
        mix_ref, convout_ref, wkvout_ref,
        s_ref, cpr_ref, cu_ref,
        nkk_s, dr_s, d_s, b_s, k_s, vr_s, vkr_s, g_s, bonus_s, y_s,
        *, chunk, sub):
    t_idx = pl.program_id(1)
    n_t = pl.num_programs(1)
    nb = SEQ_BLOCK
    rows = sub * nb
    n_sub = chunk // sub

    @pl.when(t_idx == 0)
    def _():
        s_ref[...] = wkv0_ref[...]
        cpr_ref[...] = pprev_ref[...]
        cu_ref[...] = conv0_ref[...]

    ones_blk = ones_ref[...]

    def flat(ref, t0, lo, hi):
        return ref[pl.ds(t0, sub), :, lo:hi].reshape(rows, hi - lo)

    def unflat(x):
        return x.reshape(sub, nb, x.shape[-1])

    def conv_input(t0):
        return flat(p_ref, t0, 2 * CONV_DIM, 3 * CONV_DIM) * flat(p_ref, t0, 0, CONV_DIM)

    def prep_block(t0, first):
        blk = pl.ds(t0, sub)
        bc = flat(p_ref, t0, CONV_DIM, 2 * CONV_DIM)
        u = conv_input(t0)
        if first:
            um2, um1 = cu_ref[0], cu_ref[1]
            u1 = jnp.concatenate([um1, u[:rows - nb]], axis=0)
            u2 = jnp.concatenate([um2, um1, u[:rows - 2 * nb]], axis=0)
        else:
            u1 = conv_input(t0 - 1)
            u2 = conv_input(t0 - 2)
        conv = u2 * convw_ref[0:1, :] + u1 * convw_ref[1:2, :] + u * convw_ref[2:3, :]
        yc = bc * conv
        yc = yc * lax.rsqrt(jnp.mean(yc * yc, axis=-1, keepdims=True) + RMS_EPS) * convn_ref[...]
        mix_ref[blk, :, 0:CONV_DIM] = unflat(yc)

        pr = flat(p_ref, t0, 3 * CONV_DIM, IN_PROJ)
        if first:
            shifted = jnp.concatenate([cpr_ref[...], pr[:rows - nb]], axis=0)
        else:
            shifted = flat(p_ref, t0 - 1, 3 * CONV_DIM, IN_PROJ)
        m = pr + (shifted - pr) * mu_ref[...]
        o1, o2, o3 = RWKV_DIM, 2 * RWKV_DIM, 3 * RWKV_DIM
        r = m[:, :o1]
        k = m[:, o1:o2]
        v = m[:, o2:o3]
        wa_d = m[:, o3:o3 + LORA_DECAY + LORA_A]
        g_d = m[:, o3 + LORA_DECAY + LORA_A:]
        z = w0_ref[...] + jnp.dot(jnp.tanh(wa_d).astype(bf16), wdu_ref[...],
                                  preferred_element_type=f32)
        nz = -z
        softplus = jnp.maximum(nz, 0.0) + jnp.log1p(jnp.exp(-jnp.abs(nz)))
        w = -softplus - 0.5
        d = jnp.exp(-jnp.exp(w))
        a = _sigmoid(a0_ref[...] + jnp.dot(wa_d.astype(bf16), aup_ref[...],
                                           preferred_element_type=f32))
        g = jnp.dot(_sigmoid(g_d).astype(bf16), gup_ref[...], preferred_element_type=f32)
        kk = k * kk_ref[...]
        kk_norm = jnp.sqrt(_group_sum_wide(kk * kk, ones_blk, 3))
        kk = kk / jnp.maximum(kk_norm, 1e-12)
        k = k * (1.0 + (a - 1.0) * ka_ref[...])
        b = kk * a
        nkk_s[blk] = unflat(-kk)
        dr_s[blk] = unflat(d * r - kk * _group_sum_wide(b * r, ones_blk, 3))
        d_s[blk] = unflat(d)
        b_s[blk] = unflat(b)
        k_s[blk] = unflat(k)
        vr_s[blk] = unflat(v.astype(bf16).astype(f32))
        vkr_s[blk] = unflat(_group_sum_wide(k * r, ones_blk, 3) * v)
        g_s[blk] = unflat(g)
        bonus_s[blk] = unflat(_group_sum_wide(r * k * rk_ref[...], ones_blk, 3) * v)

    prep_block(0, True)
    if n_sub > 1:
        def prep_body(i, carry):
            prep_block(i * sub, False)
            return carry
        lax.fori_loop(1, n_sub, prep_body, 0)

    cu_ref[...] = (p_ref[chunk - 2:chunk, :, 2 * CONV_DIM:3 * CONV_DIM]
                   * p_ref[chunk - 2:chunk, :, 0:CONV_DIM])
    cpr_ref[...] = p_ref[chunk - 1, :, 3 * CONV_DIM:IN_PROJ]

    sub_i = lax.broadcasted_iota(jnp.int32, (HEAD_DIM, LANES), 0)
    lane_i = lax.broadcasted_iota(jnp.int32, (HEAD_DIM, LANES), 1)
    diag = (lane_i % HEAD_DIM) == sub_i

    pair_w = pair_ref[...]
    pair_b = pair_w.astype(bf16)

    def step(t, carry):
        for b0 in range(0, nb, STEP_SEQS):
            streams = [(bi, j) for bi in range(b0, b0 + STEP_SEQS) for j in range(HEAD_GROUPS)]

            def rowv(ref, bi, j):
                return ref[t, bi:bi + 1, j * LANES:(j + 1) * LANES]

            prod_rows = []
            for bi, j in streams:
                s = s_ref[bi, j]
                prod_rows.append(jnp.concatenate(
                    [(s * rowv(nkk_s, bi, j)).astype(bf16), (s * rowv(dr_s, bi, j)).astype(bf16)],
                    axis=1))
            sums = jnp.dot(jnp.concatenate(prod_rows, axis=0), pair_b,
                           preferred_element_type=f32)
            v_rows = []
            for n in range(0, len(streams), 2):
                v_rows.append(jnp.concatenate(
                    [jnp.where(diag, rowv(vr_s, *streams[n + c]), 0.0) for c in range(2)], axis=1))
            vbs = jnp.dot(jnp.concatenate(v_rows, axis=0), pair_w, preferred_element_type=f32)
            for n, (bi, j) in enumerate(streams):
                sa = sums[n * HEAD_DIM:(n + 1) * HEAD_DIM, 0:LANES]
                zz = sums[n * HEAD_DIM:(n + 1) * HEAD_DIM, LANES:2 * LANES]
                vrow0 = (n // 2) * HEAD_DIM
                vb = vbs[vrow0:vrow0 + HEAD_DIM, (n % 2) * LANES:(n % 2 + 1) * LANES]
                s = s_ref[bi, j]
                s_ref[bi, j] = s * rowv(d_s, bi, j) + sa * rowv(b_s, bi, j) + vb * rowv(k_s, bi, j)
                yp = jnp.where(diag, zz, 0.0)
                y_row = jnp.sum(yp, axis=0, keepdims=True) + rowv(vkr_s, bi, j)
                y_s[t, bi:bi + 1, j * LANES:(j + 1) * LANES] = y_row
        return carry

    lax.fori_loop(0, chunk, step, 0)

    def post_body(i, carry):
        blk = pl.ds(i * sub, sub)
        y = y_s[blk].reshape(rows, RWKV_DIM)
        mean = _group_sum_wide(y, ones_blk, 3) * (1.0 / HEAD_DIM)
        yc = y - mean
        var = _group_sum_wide(yc * yc, ones_blk, 3) * (1.0 / HEAD_DIM)
        yn = yc * lax.rsqrt(var + GN_EPS)
        yn = yn * lnw_ref[...] + lnb_ref[...]
        out = (yn + bonus_s[blk].reshape(rows, RWKV_DIM)) * g_s[blk].reshape(rows, RWKV_DIM)
        mix_ref[blk, :, CONV_DIM:CONV_DIM + RWKV_DIM] = unflat(out)
        return carry

    lax.fori_loop(0, n_sub, post_body, 0)

    @pl.when(t_idx == n_t - 1)
    def _():
        convout_ref[...] = cu_ref[...]
        wkvout_ref[...] = s_ref[...]


def _mixer(p, p_prev, conv0, wkv0, ones_blk, pair_w, lp):
    t_len, bsz, _ = p.shape
    nb = SEQ_BLOCK
    chunk = min(TIME_CHUNK, t_len)
    sub = min(PREP_TOKENS, chunk)
    assert sub >= CONV_WIDTH - 1 and chunk % sub == 0 and t_len % chunk == 0 and bsz % nb == 0
    grid = (bsz // nb, t_len // chunk)
    const = lambda shape: pl.BlockSpec(shape, lambda i, t: (0,) * len(shape))
    in_specs = [
        pl.BlockSpec((chunk, nb, IN_PROJ), lambda i, t: (t, i, 0)),
        pl.BlockSpec((nb, RWKV_PROJ), lambda i, t: (i, 0)),
        pl.BlockSpec((CONV_WIDTH - 1, nb, CONV_DIM), lambda i, t: (0, i, 0)),
        pl.BlockSpec((nb, HEAD_GROUPS, HEAD_DIM, LANES), lambda i, t: (i, 0, 0, 0)),
        const((LANES, LANES)),
        const((2 * LANES, 2 * LANES)),
        const((1, RWKV_PROJ)),
        const((CONV_WIDTH, CONV_DIM)),
        const((1, CONV_DIM)),
        const((1, RWKV_DIM)),
        const((LORA_DECAY + LORA_A, RWKV_DIM)),
        const((1, RWKV_DIM)),
        const((LORA_DECAY + LORA_A, RWKV_DIM)),
        const((LORA_GATE, RWKV_DIM)),
        const((1, RWKV_DIM)),
        const((1, RWKV_DIM)),
        const((1, RWKV_DIM)),
        const((1, RWKV_DIM)),
        const((1, RWKV_DIM)),
    ]
    out_specs = [
        pl.BlockSpec((chunk, nb, D_MODEL), lambda i, t: (t, i, 0)),
        pl.BlockSpec((CONV_WIDTH - 1, nb, CONV_DIM), lambda i, t: (0, i, 0)),
        pl.BlockSpec((nb, HEAD_GROUPS, HEAD_DIM, LANES), lambda i, t: (i, 0, 0, 0)),
    ]
    out_shape = [
        jax.ShapeDtypeStruct((t_len, bsz, D_MODEL), f32),
        jax.ShapeDtypeStruct((CONV_WIDTH - 1, bsz, CONV_DIM), f32),
        jax.ShapeDtypeStruct((bsz, HEAD_GROUPS, HEAD_DIM, LANES), f32),
    ]
    scratch = [
        pltpu.VMEM((nb, HEAD_GROUPS, HEAD_DIM, LANES), f32),
        pltpu.VMEM((nb, RWKV_PROJ), f32),
        pltpu.VMEM((CONV_WIDTH - 1, nb, CONV_DIM), f32),
    ] + [pltpu.VMEM((chunk, nb, RWKV_DIM), f32) for _ in range(10)]
    return pl.pallas_call(
        functools.partial(_mixer_kernel, chunk=chunk, sub=sub),
        grid=grid,
        in_specs=in_specs,
        out_specs=out_specs,
        out_shape=out_shape,
        scratch_shapes=scratch,
        compiler_params=pltpu.CompilerParams(
            dimension_semantics=("arbitrary", "arbitrary"),
            vmem_limit_bytes=VMEM_LIMIT_BYTES),
        name="mixer",
    )(p, p_prev, conv0, wkv0, ones_blk, pair_w,
      lp["mu"], lp["conv_w"], lp["conv_norm"], lp["w0"], lp["wdu"], lp["a0"], lp["aup"],
      lp["gup"], lp["k_k"], lp["k_a"], lp["r_k"], lp["ln_w"], lp["ln_b"])


def _nt_dot(a, b):
    return lax.dot_general(a, b, (((1,), (1,)), ((), ())), preferred_element_type=f32)


def _tn_dot(a, b):
    return lax.dot_general(a, b, (((0,), (0,)), ((), ())), preferred_element_type=f32)


def _stack_heads(x):
    lane = lax.broadcasted_iota(jnp.int32, x.shape, 1)
    return jnp.concatenate(
        [jnp.where(lane < HEAD_DIM, x, 0.0), jnp.where(lane >= HEAD_DIM, x, 0.0)], axis=0)


def _chunk_mixer_kernel(
        p_ref, pprev_ref, conv0_ref, wkv0_ref, ones_ref, tri_ref,
        mu_ref, convw_ref, convn_ref, w0_ref, wdu_ref, a0_ref, aup_ref, gup_ref,
        kk_ref, ka_ref, rk_ref, lnw_ref, lnb_ref,
        mix_ref, convout_ref, wkvout_ref,
        s_ref, cpr_ref, cu_ref,
        *, chunk):
    t_idx = pl.program_id(1)
    n_t = pl.num_programs(1)
    c2 = 2 * chunk

    @pl.when(t_idx == 0)
    def _():
        s_ref[...] = wkv0_ref[0]
        cpr_ref[...] = pprev_ref[0]
        cu_ref[...] = conv0_ref[0]

    ones_blk = ones_ref[...]
    row = lax.broadcasted_iota(jnp.int32, (chunk, 1), 0)

    def shift_rows(x, prev_rows):
        out = pltpu.roll(x, len(prev_rows), axis=0)
        for i, pr in enumerate(prev_rows):
            out = jnp.where(row == i, pr, out)
        return out

    u = p_ref[0, :, 2 * CONV_DIM:3 * CONV_DIM] * p_ref[0, :, 0:CONV_DIM]
    um2, um1 = cu_ref[0:1, :], cu_ref[1:2, :]
    conv = (shift_rows(u, [um2, um1]) * convw_ref[0:1, :] + shift_rows(u, [um1]) * convw_ref[1:2, :]
            + u * convw_ref[2:3, :])
    cu_ref[...] = u[chunk - 2:chunk, :]
    yc = p_ref[0, :, CONV_DIM:2 * CONV_DIM] * conv
    yc = yc * lax.rsqrt(jnp.mean(yc * yc, axis=-1, keepdims=True) + RMS_EPS) * convn_ref[...]
    mix_ref[0, :, 0:CONV_DIM] = yc

    pr = p_ref[0, :, 3 * CONV_DIM:IN_PROJ]
    shifted = shift_rows(pr, [cpr_ref[...]])
    cpr_ref[...] = pr[chunk - 1:chunk, :]
    m = pr + (shifted - pr) * mu_ref[...]
    o1, o2, o3 = RWKV_DIM, 2 * RWKV_DIM, 3 * RWKV_DIM
    r = m[:, :o1]
    k = m[:, o1:o2]
    v = m[:, o2:o3]
    wa_d = m[:, o3:o3 + LORA_DECAY + LORA_A]
    g_d = m[:, o3 + LORA_DECAY + LORA_A:]
    z = w0_ref[...] + jnp.dot(jnp.tanh(wa_d).astype(bf16), wdu_ref[...],
                              preferred_element_type=f32)
    nz = -z
    softplus = jnp.maximum(nz, 0.0) + jnp.log1p(jnp.exp(-jnp.abs(nz)))
    log_d = -jnp.exp(-softplus - 0.5)
    a = _sigmoid(a0_ref[...] + jnp.dot(wa_d.astype(bf16), aup_ref[...],
                                       preferred_element_type=f32))
    g = jnp.dot(_sigmoid(g_d).astype(bf16), gup_ref[...], preferred_element_type=f32)
    kk = k * kk_ref[...]
    kk = kk / jnp.maximum(jnp.sqrt(_group_sum_wide(kk * kk, ones_blk, 3)), 1e-12)
    k = k * (1.0 + (a - 1.0) * ka_ref[...])
    bonus = _group_sum_wide(r * k * rk_ref[...], ones_blk, 3) * v

    tri = tri_ref[...]
    cum = None
    for part in _split_bf16(log_d, 3):
        term = jnp.dot(tri, part, preferred_element_type=f32)
        cum = term if cum is None else cum + term
    total = cum[chunk - 1:chunk, :]
    p_inc = jnp.exp(cum)
    p_inv = jnp.exp(-cum)
    p_rest = jnp.exp(total - cum)
    at = -kk * jnp.exp(cum - log_d)
    rt = r * p_inc
    beta = kk * a
    bt = beta * p_inv
    kt = k * p_inv
    bc = beta * p_rest
    kc = k * p_rest
    p_total = jnp.exp(total)

    ri = lax.broadcasted_iota(jnp.int32, (c2, c2), 0) % chunk
    ci = lax.broadcasted_iota(jnp.int32, (c2, c2), 1) % chunk
    strict = ci < ri
    incl = ci <= ri
    eye = (lax.broadcasted_iota(jnp.int32, (c2, c2), 0)
           == lax.broadcasted_iota(jnp.int32, (c2, c2), 1)).astype(f32)

    groups = range(HEAD_GROUPS)
    mm = functools.partial(jnp.dot, preferred_element_type=f32)

    def stack(x, j):
        return _stack_heads(x[:, j * LANES:(j + 1) * LANES]).astype(bf16)

    ar = [jnp.concatenate([stack(at, j), stack(rt, j)], axis=0) for j in groups]
    bk = [jnp.concatenate([stack(bt, j), stack(kt, j)], axis=0) for j in groups]
    v2 = [stack(v, j) for j in groups]
    s_pack = [s_ref[j] for j in groups]
    scores = [_nt_dot(ar[j], bk[j]) for j in groups]
    from_state = [_nt_dot(ar[j], _stack_heads(s_pack[j]).astype(bf16)) for j in groups]
    l_ab = [jnp.where(strict, scores[j][:c2, :c2], 0.0) for j in groups]
    l_b = [l_ab[j].astype(bf16) for j in groups]
    power = [mm(l_b[j], l_b[j]) for j in groups]
    w2 = [from_state[j][:c2]
          + mm(jnp.where(strict, scores[j][:c2, c2:], 0.0).astype(bf16), v2[j]) for j in groups]
    y_part = [from_state[j][c2:]
              + mm(jnp.where(incl, scores[j][c2:, c2:], 0.0).astype(bf16), v2[j]) for j in groups]
    inv = [eye + l_ab[j] for j in groups]
    span = 2
    while span < chunk:
        if 2 * span < chunk:
            both = [mm(jnp.concatenate([inv[j], power[j]], axis=0).astype(bf16),
                       power[j].astype(bf16)) for j in groups]
            inv = [inv[j] + both[j][:c2] for j in groups]
            power = [both[j][c2:] for j in groups]
        else:
            inv = [inv[j] + mm(inv[j].astype(bf16), power[j].astype(bf16)) for j in groups]
        span *= 2
    u2 = [mm(inv[j].astype(bf16), w2[j].astype(bf16)).astype(bf16) for j in groups]
    y2 = [y_part[j] + mm(jnp.where(incl, scores[j][c2:, :c2], 0.0).astype(bf16), u2[j])
          for j in groups]
    s_new = [_tn_dot(u2[j], stack(bc, j)) + _tn_dot(v2[j], stack(kc, j)) for j in groups]
    for j in groups:
        s_ref[j] = (s_pack[j] * p_total[:, j * LANES:(j + 1) * LANES]
                    + s_new[j][:HEAD_DIM] + s_new[j][HEAD_DIM:])
    y = jnp.concatenate([y2[j][:chunk] + y2[j][chunk:] for j in groups], axis=1)

    mean = _group_sum_wide(y, ones_blk, 3) * (1.0 / HEAD_DIM)
    ycen = y - mean
    var = _group_sum_wide(ycen * ycen, ones_blk, 3) * (1.0 / HEAD_DIM)
    yn = ycen * lax.rsqrt(var + GN_EPS) * lnw_ref[...] + lnb_ref[...]
    mix_ref[0, :, CONV_DIM:CONV_DIM + RWKV_DIM] = (yn + bonus) * g

    @pl.when(t_idx == n_t - 1)
    def _():
        convout_ref[0] = cu_ref[...]
        wkvout_ref[0] = s_ref[...]


def _chunk_mixer(p, p_prev, conv0, wkv0, ones_blk, lp):
    bsz, t_len, _ = p.shape
    chunk = TIME_CHUNK
    assert t_len % chunk == 0 and chunk & (chunk - 1) == 0
    tri = (jnp.arange(chunk)[:, None] >= jnp.arange(chunk)[None, :]).astype(bf16)
    const = lambda shape: pl.BlockSpec(shape, lambda i, t: (0,) * len(shape))
    in_specs = [
        pl.BlockSpec((1, chunk, IN_PROJ), lambda i, t: (i, t, 0)),
        pl.BlockSpec((1, 1, RWKV_PROJ), lambda i, t: (i, 0, 0)),
        pl.BlockSpec((1, CONV_WIDTH - 1, CONV_DIM), lambda i, t: (i, 0, 0)),
        pl.BlockSpec((1, HEAD_GROUPS, HEAD_DIM, LANES), lambda i, t: (i, 0, 0, 0)),
        const((LANES, LANES)),
        const((chunk, chunk)),
        const((1, RWKV_PROJ)),
        const((CONV_WIDTH, CONV_DIM)),
        const((1, CONV_DIM)),
        const((1, RWKV_DIM)),
        const((LORA_DECAY + LORA_A, RWKV_DIM)),
        const((1, RWKV_DIM)),
        const((LORA_DECAY + LORA_A, RWKV_DIM)),
        const((LORA_GATE, RWKV_DIM)),
        const((1, RWKV_DIM)),
        const((1, RWKV_DIM)),
        const((1, RWKV_DIM)),
        const((1, RWKV_DIM)),
        const((1, RWKV_DIM)),
    ]
    out_specs = [
        pl.BlockSpec((1, chunk, D_MODEL), lambda i, t: (i, t, 0)),
        pl.BlockSpec((1, CONV_WIDTH - 1, CONV_DIM), lambda i, t: (i, 0, 0)),
        pl.BlockSpec((1, HEAD_GROUPS, HEAD_DIM, LANES), lambda i, t: (i, 0, 0, 0)),
    ]
    out_shape = [
        jax.ShapeDtypeStruct((bsz, t_len, D_MODEL), f32),
        jax.ShapeDtypeStruct((bsz, CONV_WIDTH - 1, CONV_DIM), f32),
        jax.ShapeDtypeStruct((bsz, HEAD_GROUPS, HEAD_DIM, LANES), f32),
    ]
    scratch = [
        pltpu.VMEM((HEAD_GROUPS, HEAD_DIM, LANES), f32),
        pltpu.VMEM((1, RWKV_PROJ), f32),
        pltpu.VMEM((CONV_WIDTH - 1, CONV_DIM), f32),
    ]
    return pl.pallas_call(
        functools.partial(_chunk_mixer_kernel, chunk=chunk),
        grid=(bsz, t_len // chunk),
        in_specs=in_specs,
        out_specs=out_specs,
        out_shape=out_shape,
        scratch_shapes=scratch,
        compiler_params=pltpu.CompilerParams(
            dimension_semantics=("arbitrary", "arbitrary"),
            vmem_limit_bytes=VMEM_LIMIT_BYTES),
        name="chunk_mixer",
    )(p, p_prev, conv0, wkv0, ones_blk, tri,
      lp["mu"], lp["conv_w"], lp["conv_norm"], lp["w0"], lp["wdu"], lp["a0"], lp["aup"],
      lp["gup"], lp["k_k"], lp["k_a"], lp["r_k"], lp["ln_w"], lp["ln_b"])


def _ffn_kernel(x_ref, mix_ref, wout_ref, g2_ref, wg_ref, wu_ref, wd_ref, o_ref):
    x = x_ref[...] + jnp.dot(mix_ref[...].astype(bf16), wout_ref[...],
                             preferred_element_type=f32)
    h = x * lax.rsqrt(jnp.mean(x * x, axis=-1, keepdims=True) + RMS_EPS) * g2_ref[...]
    hb = h.astype(bf16)
    gate = jnp.dot(hb, wg_ref[...], preferred_element_type=f32)
    up = jnp.dot(hb, wu_ref[...], preferred_element_type=f32)
    act = gate * _sigmoid(gate) * up
    o_ref[...] = x + jnp.dot(act.astype(bf16), wd_ref[...], preferred_element_type=f32)


def _ffn(x, mix, lp):
    n, d = x.shape
    tile = min(ROW_TILE, n)
    resident = lambda shape: pl.BlockSpec(shape, lambda i: (0, 0),
                                          pipeline_mode=pl.Buffered(1))
    return pl.pallas_call(
        _ffn_kernel,
        grid=(n // tile,),
        in_specs=[
            pl.BlockSpec((tile, d), lambda i: (i, 0)),
            pl.BlockSpec((tile, d), lambda i: (i, 0)),
            resident((d, d)),
            resident((1, d)),
            resident((d, D_FF)),
            resident((d, D_FF)),
            resident((D_FF, d)),
        ],
        out_specs=pl.BlockSpec((tile, d), lambda i: (i, 0)),
        out_shape=jax.ShapeDtypeStruct((n, d), f32),
        compiler_params=pltpu.CompilerParams(
            dimension_semantics=("arbitrary",), vmem_limit_bytes=VMEM_LIMIT_BYTES),
        name="outproj_ffn",
    )(x, mix, lp["w_out"], lp["norm2"], lp["w_gate"], lp["w_up"], lp["w_down"])


def _pack_wkv(s):
    b = s.shape[0]
    s = s.reshape(b, HEAD_GROUPS, HEADS_PER_GROUP, HEAD_DIM, HEAD_DIM)
    return s.transpose(0, 1, 3, 2, 4).reshape(b, HEAD_GROUPS, HEAD_DIM, LANES)


def _unpack_wkv(s):
    b = s.shape[0]
    s = s.reshape(b, HEAD_GROUPS, HEAD_DIM, HEADS_PER_GROUP, HEAD_DIM)
    return s.transpose(0, 1, 3, 2, 4).reshape(b, RWKV_HEADS, HEAD_DIM, HEAD_DIM)


def _prepare_params(norm1, w_in, mu_shift, conv_w, conv_norm, w0, w_decay_up, a0, a_up, g_up,
                    k_k, k_a, r_k, ln_x_w, ln_x_b, w_out, norm2, w_gate, w_up, w_down,
                    final_norm):
    row = lambda a: a.reshape(1, -1)
    zeros_lora = jnp.zeros((LORA_DECAY, RWKV_DIM), f32)
    layers = []
    for l in range(DEPTH):
        w_in_b = w_in[l].astype(bf16)
        layers.append(dict(
            norm1=row(norm1[l]), w_in=w_in_b, w_in_rwkv=w_in_b[:, 3 * CONV_DIM:],
            mu=row(mu_shift[l]), conv_w=conv_w[l], conv_norm=row(conv_norm[l]),
            w0=row(w0[l]),
            wdu=jnp.concatenate([w_decay_up[l], zeros_lora], axis=0).astype(bf16),
            a0=row(a0[l]),
            aup=jnp.concatenate([zeros_lora, a_up[l]], axis=0).astype(bf16),
            gup=g_up[l].astype(bf16),
            k_k=row(k_k[l]), k_a=row(k_a[l]), r_k=row(r_k[l]),
            ln_w=row(ln_x_w[l]), ln_b=row(ln_x_b[l]),
            w_out=w_out[l].astype(bf16), norm2=row(norm2[l]),
            w_gate=w_gate[l].astype(bf16), w_up=w_up[l].astype(bf16),
            w_down=w_down[l].astype(bf16)))
    lane = jnp.arange(LANES) // HEAD_DIM
    ones_blk = (lane[:, None] == lane[None, :]).astype(bf16)
    zero = jnp.zeros((LANES, LANES), bf16)
    pair_w = jnp.block([[ones_blk, zero], [zero, ones_blk]]).astype(f32)
    return layers, row(final_norm), ones_blk, pair_w


def _trunk(x, shift0, conv0, wkv0, prepared):
    layers, final_norm, ones_blk, pair_w = prepared
    bsz, t_len, d = x.shape
    chunked = t_len >= TIME_CHUNK
    xf = x.reshape(bsz * t_len, d) if chunked else x.transpose(1, 0, 2).reshape(t_len * bsz, d)
    shifts, convs, wkvs = [], [], []
    for l, lp in enumerate(layers):
        p = _inproj(xf, lp["norm1"], lp["w_in"], normalize=True)
        last = xf.reshape(bsz, t_len, d)[:, -1] if chunked else xf[(t_len - 1) * bsz:]
        shifts.append(_rmsnorm(last, lp["norm1"]))
        p_prev = _inproj(shift0[l], lp["norm1"], lp["w_in_rwkv"], normalize=False)
        if chunked:
            mix, conv_new, wkv_new = _chunk_mixer(
                p.reshape(bsz, t_len, IN_PROJ), p_prev.reshape(bsz, 1, RWKV_PROJ), conv0[l],
                _pack_wkv(wkv0[l]), ones_blk, lp)
        else:
            mix, conv_new, wkv_new = _mixer(
                p.reshape(t_len, bsz, IN_PROJ), p_prev, conv0[l].transpose(1, 0, 2),
                _pack_wkv(wkv0[l]), ones_blk, pair_w, lp)
            conv_new = conv_new.transpose(1, 0, 2)
        xf = _ffn(xf, mix.reshape(t_len * bsz, d), lp)
        convs.append(conv_new)
        wkvs.append(_unpack_wkv(wkv_new))
    y = _rmsnorm(xf, final_norm)
    y = y.reshape(bsz, t_len, d) if chunked else y.reshape(t_len, bsz, d).transpose(1, 0, 2)
    return y, jnp.stack(shifts), jnp.stack(convs), jnp.stack(wkvs)


def kernel(x_prompt, x_sample, state_shift, state_conv, state_wkv, norm1, w_in, mu_shift, conv_w, conv_norm, w0, w_decay_up, a0, a_up, g_up, k_k, k_a, r_k, ln_x_w, ln_x_b, w_out, norm2, w_gate, w_up, w_down, final_norm):
    prepared = _prepare_params(norm1, w_in, mu_shift, conv_w, conv_norm, w0, w_decay_up, a0,
                               a_up, g_up, k_k, k_a, r_k, ln_x_w, ln_x_b, w_out, norm2,
                               w_gate, w_up, w_down, final_norm)
    bp = x_prompt.shape[0]
    dt = x_prompt.dtype
    p_shift0 = jnp.zeros((DEPTH, bp, D_MODEL), dt)
    p_conv0 = jnp.zeros((DEPTH, bp, CONV_WIDTH - 1, CONV_DIM), dt)
    p_wkv0 = jnp.zeros((DEPTH, bp, RWKV_HEADS, HEAD_DIM, HEAD_DIM), state_wkv.dtype)
    y_p, sh_p, cv_p, wk_p = _trunk(x_prompt, p_shift0, p_conv0, p_wkv0, prepared)
    y_s, sh_s, cv_s, wk_s = _trunk(x_sample, state_shift, state_conv, state_wkv, prepared)
    return (y_p, y_s, sh_p, cv_p, wk_p, sh_s, cv_s, wk_s)
```

```python
import functools

import jax
import jax.numpy as jnp
from jax import lax
from jax.experimental import pallas as pl
from jax.experimental.pallas import tpu as pltpu

D_MODEL = 1024
DEPTH = 4
CONV_DIM = 512
CONV_WIDTH = 3
RWKV_DIM = 512
HEAD_DIM = 64
RWKV_HEADS = RWKV_DIM // HEAD_DIM
LORA_DECAY = 64
LORA_A = 64
LORA_GATE = 128
RWKV_PROJ = 3 * RWKV_DIM + LORA_DECAY + LORA_A + LORA_GATE
IN_PROJ = 3 * CONV_DIM + RWKV_PROJ
D_FF = 2816
RMS_EPS = 1e-6
GN_EPS = 64e-5

LANES = 128
SUBLANES = 8
HEADS_PER_GROUP = LANES // HEAD_DIM
HEAD_GROUPS = RWKV_HEADS // HEADS_PER_GROUP
VMEM_LIMIT_BYTES = 56 * 1024 * 1024
ROW_TILE = 512
SEQ_BLOCK = SUBLANES
TIME_CHUNK = 64
PREP_TOKENS = 8
STEP_SEQS = 2
CHUNK_SEQS = 2

f32 = jnp.float32
bf16 = jnp.bfloat16


def _split_bf16(x, terms):
    parts = []
    rem = x
    for i in range(terms):
        part = rem.astype(bf16)
        parts.append(part)
        if i + 1 < terms:
            rem = rem - part.astype(f32)
    return parts


def _group_sum(x, ones_blk, terms):
    acc = None
    for part in _split_bf16(x, terms):
        d = jnp.dot(part, ones_blk, preferred_element_type=f32)
        acc = d if acc is None else acc + d
    return acc


def _group_sum_wide(x, ones_blk, terms):
    cols = [
        _group_sum(x[:, c * LANES:(c + 1) * LANES], ones_blk, terms)
        for c in range(x.shape[1] // LANES)
    ]
    return jnp.concatenate(cols, axis=1)


def _sigmoid(x):
    return 1.0 / (1.0 + jnp.exp(-x))


def _inproj_kernel(x_ref, g_ref, w_ref, o_ref, *, normalize):
    x = x_ref[...]
    if normalize:
        x = x * lax.rsqrt(jnp.mean(x * x, axis=-1, keepdims=True) + RMS_EPS) * g_ref[...]
    o_ref[...] = jnp.dot(x.astype(bf16), w_ref[...], preferred_element_type=f32)


def _inproj(x, g, w, *, normalize):
    n, d = x.shape
    m = w.shape[1]
    tile = min(ROW_TILE, n)
    return pl.pallas_call(
        functools.partial(_inproj_kernel, normalize=normalize),
        grid=(n // tile,),
        in_specs=[
            pl.BlockSpec((tile, d), lambda i: (i, 0)),
            pl.BlockSpec((1, d), lambda i: (0, 0)),
            pl.BlockSpec((d, m), lambda i: (0, 0)),
        ],
        out_specs=pl.BlockSpec((tile, m), lambda i: (i, 0)),
        out_shape=jax.ShapeDtypeStruct((n, m), f32),
        compiler_params=pltpu.CompilerParams(
            dimension_semantics=("arbitrary",), vmem_limit_bytes=VMEM_LIMIT_BYTES),
        name="inproj" if normalize else "shift_proj",
    )(x, g, w)


def _rmsnorm_kernel(x_ref, g_ref, o_ref):
    x = x_ref[...]
    o_ref[...] = x * lax.rsqrt(jnp.mean(x * x, axis=-1, keepdims=True) + RMS_EPS) * g_ref[...]


def _rmsnorm(x, g):
    n, d = x.shape
    tile = min(ROW_TILE, n)
    return pl.pallas_call(
        _rmsnorm_kernel,
        grid=(n // tile,),
        in_specs=[pl.BlockSpec((tile, d), lambda i: (i, 0)),
                  pl.BlockSpec((1, d), lambda i: (0, 0))],
        out_specs=pl.BlockSpec((tile, d), lambda i: (i, 0)),
        out_shape=jax.ShapeDtypeStruct((n, d), f32),
        compiler_params=pltpu.CompilerParams(dimension_semantics=("arbitrary",)),
        name="rmsnorm",
    )(x, g)


def _mixer_kernel(
        p_ref, pprev_ref, conv0_ref, wkv0_ref, ones_ref, pair_ref,
        mu_ref, convw_ref, convn_ref, w0_ref, wdu_ref, a0_ref, aup_ref, gup_ref,
        kk_ref, ka_ref, rk_ref, lnw_ref, lnb_ref,
        mix_ref, convout_ref, wkvout_ref,
        s_ref, cpr_ref, cu_ref,
        nkk_s, dr_s, d_s, b_s, k_s, vr_s, vkr_s, g_s, bonus_s, y_s,
        *, chunk, sub):
    t_idx = pl.program_id(1)
    n_t = pl.num_programs(1)
    nb = SEQ_BLOCK
    rows = sub * nb
    n_sub = chunk // sub

    @pl.when(t_idx == 0)
    def _():
        s_ref[...] = wkv0_ref[...]
        cpr_ref[...] = pprev_ref[...]
        cu_ref[...] = conv0_ref[...]

    ones_blk = ones_ref[...]

    def flat(ref, t0, lo, hi):
        return ref[pl.ds(t0, sub), :, lo:hi].reshape(rows, hi - lo)

    def unflat(x):
        return x.reshape(sub, nb, x.shape[-1])

    def conv_input(t0):
        return flat(p_ref, t0, 2 * CONV_DIM, 3 * CONV_DIM) * flat(p_ref, t0, 0, CONV_DIM)

    def prep_block(t0, first):
        blk = pl.ds(t0, sub)
        bc = flat(p_ref, t0, CONV_DIM, 2 * CONV_DIM)
        u = conv_input(t0)
        if first:
            um2, um1 = cu_ref[0], cu_ref[1]
            u1 = jnp.concatenate([um1, u[:rows - nb]], axis=0)
            u2 = jnp.concatenate([um2, um1, u[:rows - 2 * nb]], axis=0)
        else:
            u1 = conv_input(t0 - 1)
            u2 = conv_input(t0 - 2)
        conv = u2 * convw_ref[0:1, :] + u1 * convw_ref[1:2, :] + u * convw_ref[2:3, :]
        yc = bc * conv
        yc = yc * lax.rsqrt(jnp.mean(yc * yc, axis=-1, keepdims=True) + RMS_EPS) * convn_ref[...]
        mix_ref[blk, :, 0:CONV_DIM] = unflat(yc)

        pr = flat(p_ref, t0, 3 * CONV_DIM, IN_PROJ)
        if first:
            shifted = jnp.concatenate([cpr_ref[...], pr[:rows - nb]], axis=0)
        else:
            shifted = flat(p_ref, t0 - 1, 3 * CONV_DIM, IN_PROJ)
        m = pr + (shifted - pr) * mu_ref[...]
        o1, o2, o3 = RWKV_DIM, 2 * RWKV_DIM, 3 * RWKV_DIM
        r = m[:, :o1]
        k = m[:, o1:o2]
        v = m[:, o2:o3]
        wa_d = m[:, o3:o3 + LORA_DECAY + LORA_A]
        g_d = m[:, o3 + LORA_DECAY + LORA_A:]
        z = w0_ref[...] + jnp.dot(jnp.tanh(wa_d).astype(bf16), wdu_ref[...],
                                  preferred_element_type=f32)
        nz = -z
        softplus = jnp.maximum(nz, 0.0) + jnp.log1p(jnp.exp(-jnp.abs(nz)))
        w = -softplus - 0.5
        d = jnp.exp(-jnp.exp(w))
        a = _sigmoid(a0_ref[...] + jnp.dot(wa_d.astype(bf16), aup_ref[...],
                                           preferred_element_type=f32))
        g = jnp.dot(_sigmoid(g_d).astype(bf16), gup_ref[...], preferred_element_type=f32)
        kk = k * kk_ref[...]
        kk_norm = jnp.sqrt(_group_sum_wide(kk * kk, ones_blk, 3))
        kk = kk / jnp.maximum(kk_norm, 1e-12)
        k = k * (1.0 + (a - 1.0) * ka_ref[...])
        b = kk * a
        nkk_s[blk] = unflat(-kk)
        dr_s[blk] = unflat(d * r - kk * _group_sum_wide(b * r, ones_blk, 3))
        d_s[blk] = unflat(d)
        b_s[blk] = unflat(b)
        k_s[blk] = unflat(k)
        vr_s[blk] = unflat(v.astype(bf16).astype(f32))
        vkr_s[blk] = unflat(_group_sum_wide(k * r, ones_blk, 3) * v)
        g_s[blk] = unflat(g)
        bonus_s[blk] = unflat(_group_sum_wide(r * k * rk_ref[...], ones_blk, 3) * v)

    prep_block(0, True)
    if n_sub > 1:
        def prep_body(i, carry):
            prep_block(i * sub, False)
            return carry
        lax.fori_loop(1, n_sub, prep_body, 0)

    cu_ref[...] = (p_ref[chunk - 2:chunk, :, 2 * CONV_DIM:3 * CONV_DIM]
                   * p_ref[chunk - 2:chunk, :, 0:CONV_DIM])
    cpr_ref[...] = p_ref[chunk - 1, :, 3 * CONV_DIM:IN_PROJ]

    sub_i = lax.broadcasted_iota(jnp.int32, (HEAD_DIM, LANES), 0)
    lane_i = lax.broadcasted_iota(jnp.int32, (HEAD_DIM, LANES), 1)
    diag = (lane_i % HEAD_DIM) == sub_i

    pair_w = pair_ref[...]
    pair_b = pair_w.astype(bf16)

    def step(t, carry):
        for b0 in range(0, nb, STEP_SEQS):
            streams = [(bi, j) for bi in range(b0, b0 + STEP_SEQS) for j in range(HEAD_GROUPS)]

            def rowv(ref, bi, j):
                return ref[t, bi:bi + 1, j * LANES:(j + 1) * LANES]

            prod_rows = []
            for bi, j in streams:
                s = s_ref[bi, j]
                prod_rows.append(jnp.concatenate(
                    [(s * rowv(nkk_s, bi, j)).astype(bf16), (s * rowv(dr_s, bi, j)).astype(bf16)],
                    axis=1))
            sums = jnp.dot(jnp.concatenate(prod_rows, axis=0), pair_b,
                           preferred_element_type=f32)
            v_rows = []
            for n in range(0, len(streams), 2):
                v_rows.append(jnp.concatenate(
                    [jnp.where(diag, rowv(vr_s, *streams[n + c]), 0.0) for c in range(2)], axis=1))
            vbs = jnp.dot(jnp.concatenate(v_rows, axis=0), pair_w, preferred_element_type=f32)
            for n, (bi, j) in enumerate(streams):
                sa = sums[n * HEAD_DIM:(n + 1) * HEAD_DIM, 0:LANES]
                zz = sums[n * HEAD_DIM:(n + 1) * HEAD_DIM, LANES:2 * LANES]
                vrow0 = (n // 2) * HEAD_DIM
                vb = vbs[vrow0:vrow0 + HEAD_DIM, (n % 2) * LANES:(n % 2 + 1) * LANES]
                s = s_ref[bi, j]
                s_ref[bi, j] = s * rowv(d_s, bi, j) + sa * rowv(b_s, bi, j) + vb * rowv(k_s, bi, j)
                yp = jnp.where(diag, zz, 0.0)
                y_row = jnp.sum(yp, axis=0, keepdims=True) + rowv(vkr_s, bi, j)
                y_s[t, bi:bi + 1, j * LANES:(j + 1) * LANES] = y_row
        return carry

    lax.fori_loop(0, chunk, step, 0)

    def post_body(i, carry):
        blk = pl.ds(i * sub, sub)
        y = y_s[blk].reshape(rows, RWKV_DIM)
        mean = _group_sum_wide(y, ones_blk, 3) * (1.0 / HEAD_DIM)
        yc = y - mean
        var = _group_sum_wide(yc * yc, ones_blk, 3) * (1.0 / HEAD_DIM)
        yn = yc * lax.rsqrt(var + GN_EPS)
        yn = yn * lnw_ref[...] + lnb_ref[...]
        out = (yn + bonus_s[blk].reshape(rows, RWKV_DIM)) * g_s[blk].reshape(rows, RWKV_DIM)
        mix_ref[blk, :, CONV_DIM:CONV_DIM + RWKV_DIM] = unflat(out)
        return carry

    lax.fori_loop(0, n_sub, post_body, 0)

    @pl.when(t_idx == n_t - 1)
    def _():
        convout_ref[...] = cu_ref[...]
        wkvout_ref[...] = s_ref[...]


def _mixer(p, p_prev, conv0, wkv0, ones_blk, pair_w, lp):
    t_len, bsz, _ = p.shape
    nb = SEQ_BLOCK
    chunk = min(TIME_CHUNK, t_len)
    sub = min(PREP_TOKENS, chunk)
    assert sub >= CONV_WIDTH - 1 and chunk % sub == 0 and t_len % chunk == 0 and bsz % nb == 0
    grid = (bsz // nb, t_len // chunk)
    const = lambda shape: pl.BlockSpec(shape, lambda i, t: (0,) * len(shape))
    in_specs = [
        pl.BlockSpec((chunk, nb, IN_PROJ), lambda i, t: (t, i, 0)),
        pl.BlockSpec((nb, RWKV_PROJ), lambda i, t: (i, 0)),
        pl.BlockSpec((CONV_WIDTH - 1, nb, CONV_DIM), lambda i, t: (0, i, 0)),
        pl.BlockSpec((nb, HEAD_GROUPS, HEAD_DIM, LANES), lambda i, t: (i, 0, 0, 0)),
        const((LANES, LANES)),
        const((2 * LANES, 2 * LANES)),
        const((1, RWKV_PROJ)),
        const((CONV_WIDTH, CONV_DIM)),
        const((1, CONV_DIM)),
        const((1, RWKV_DIM)),
        const((LORA_DECAY + LORA_A, RWKV_DIM)),
        const((1, RWKV_DIM)),
        const((LORA_DECAY + LORA_A, RWKV_DIM)),
        const((LORA_GATE, RWKV_DIM)),
        const((1, RWKV_DIM)),
        const((1, RWKV_DIM)),
        const((1, RWKV_DIM)),
        const((1, RWKV_DIM)),
        const((1, RWKV_DIM)),
    ]
    out_specs = [
        pl.BlockSpec((chunk, nb, D_MODEL), lambda i, t: (t, i, 0)),
        pl.BlockSpec((CONV_WIDTH - 1, nb, CONV_DIM), lambda i, t: (0, i, 0)),
        pl.BlockSpec((nb, HEAD_GROUPS, HEAD_DIM, LANES), lambda i, t: (i, 0, 0, 0)),
    ]
    out_shape = [
        jax.ShapeDtypeStruct((t_len, bsz, D_MODEL), f32),
        jax.ShapeDtypeStruct((CONV_WIDTH - 1, bsz, CONV_DIM), f32),
        jax.ShapeDtypeStruct((bsz, HEAD_GROUPS, HEAD_DIM, LANES), f32),
    ]
    scratch = [
        pltpu.VMEM((nb, HEAD_GROUPS, HEAD_DIM, LANES), f32),
        pltpu.VMEM((nb, RWKV_PROJ), f32),
        pltpu.VMEM((CONV_WIDTH - 1, nb, CONV_DIM), f32),
    ] + [pltpu.VMEM((chunk, nb, RWKV_DIM), f32) for _ in range(10)]
    return pl.pallas_call(
        functools.partial(_mixer_kernel, chunk=chunk, sub=sub),
        grid=grid,
        in_specs=in_specs,
        out_specs=out_specs,
        out_shape=out_shape,
        scratch_shapes=scratch,
        compiler_params=pltpu.CompilerParams(
            dimension_semantics=("arbitrary", "arbitrary"),
            vmem_limit_bytes=VMEM_LIMIT_BYTES),
        name="mixer",
    )(p, p_prev, conv0, wkv0, ones_blk, pair_w,
      lp["mu"], lp["conv_w"], lp["conv_norm"], lp["w0"], lp["wdu"], lp["a0"], lp["aup"],
      lp["gup"], lp["k_k"], lp["k_a"], lp["r_k"], lp["ln_w"], lp["ln_b"])


def _nt_dot(a, b):
    return lax.dot_general(a, b, (((1,), (1,)), ((), ())), preferred_element_type=f32)


def _tn_dot(a, b):
    return lax.dot_general(a, b, (((0,), (0,)), ((), ())), preferred_element_type=f32)


def _stack_heads(x):
    lane = lax.broadcasted_iota(jnp.int32, x.shape, 1)
    return jnp.concatenate(
        [jnp.where(lane < HEAD_DIM, x, 0.0), jnp.where(lane >= HEAD_DIM, x, 0.0)], axis=0)


def _drain(gen):
    for _ in gen:
        pass


def _interleave(main, sides):
    for _ in main:
        for side in sides:
            next(side, None)
    for side in sides:
        _drain(side)


def _chunk_mixer_kernel(
        p_ref, pprev_ref, conv0_ref, wkv0_ref, ones_ref, tri_ref,
        mu_ref, convw_ref, convn_ref, w0_ref, wdu_ref, a0_ref, aup_ref, gup_ref,
        kk_ref, ka_ref, rk_ref, lnw_ref, lnb_ref,
        mix_ref, convout_ref, wkvout_ref,
        s_ref, cpr_ref, cu_ref,
        *, chunk, n_seq):
    t_idx = pl.program_id(1)
    n_t = pl.num_programs(1)
    c2 = 2 * chunk

    @pl.when(t_idx == 0)
    def _():
        s_ref[...] = wkv0_ref[...]
        cpr_ref[...] = pprev_ref[...]
        cu_ref[...] = conv0_ref[...]

    ones_blk = ones_ref[...]
    row = lax.broadcasted_iota(jnp.int32, (chunk, 1), 0)
    ri = lax.broadcasted_iota(jnp.int32, (c2, c2), 0) % chunk
    ci = lax.broadcasted_iota(jnp.int32, (c2, c2), 1) % chunk
    strict = ci < ri
    incl = ci <= ri
    eye = (lax.broadcasted_iota(jnp.int32, (c2, c2), 0)
           == lax.broadcasted_iota(jnp.int32, (c2, c2), 1)).astype(f32)
    groups = range(HEAD_GROUPS)
    mm = functools.partial(jnp.dot, preferred_element_type=f32)

    def shift_rows(x, prev_rows):
        out = pltpu.roll(x, len(prev_rows), axis=0)
        for i, pr in enumerate(prev_rows):
            out = jnp.where(row == i, pr, out)
        return out

    def per_token(q, ops):
        u = p_ref[q, :, 2 * CONV_DIM:3 * CONV_DIM] * p_ref[q, :, 0:CONV_DIM]
        um2, um1 = cu_ref[q, 0:1, :], cu_ref[q, 1:2, :]
        conv = (shift_rows(u, [um2, um1]) * convw_ref[0:1, :]
                + shift_rows(u, [um1]) * convw_ref[1:2, :] + u * convw_ref[2:3, :])
        cu_ref[q] = u[chunk - 2:chunk, :]
        yc = p_ref[q, :, CONV_DIM:2 * CONV_DIM] * conv
        yc = yc * lax.rsqrt(jnp.mean(yc * yc, axis=-1, keepdims=True) + RMS_EPS) * convn_ref[...]
        mix_ref[q, :, 0:CONV_DIM] = yc

        pr = p_ref[q, :, 3 * CONV_DIM:IN_PROJ]
        shifted = shift_rows(pr, [cpr_ref[q]])
        cpr_ref[q] = pr[chunk - 1:chunk, :]
        m = pr + (shifted - pr) * mu_ref[...]
        o1, o2, o3 = RWKV_DIM, 2 * RWKV_DIM, 3 * RWKV_DIM
        r = m[:, :o1]
        k = m[:, o1:o2]
        v = m[:, o2:o3]
        wa_d = m[:, o3:o3 + LORA_DECAY + LORA_A]
        g_d = m[:, o3 + LORA_DECAY + LORA_A:]
        z = w0_ref[...] + mm(jnp.tanh(wa_d).astype(bf16), wdu_ref[...])
        a_lin = a0_ref[...] + mm(wa_d.astype(bf16), aup_ref[...])
        ops["g"] = mm(_sigmoid(g_d).astype(bf16), gup_ref[...])
        yield
        nz = -z
        softplus = jnp.maximum(nz, 0.0) + jnp.log1p(jnp.exp(-jnp.abs(nz)))
        log_d = -jnp.exp(-softplus - 0.5)
        a = _sigmoid(a_lin)
        kk = k * kk_ref[...]
        kk = kk / jnp.maximum(jnp.sqrt(_group_sum_wide(kk * kk, ones_blk, 3)), 1e-12)
        k = k * (1.0 + (a - 1.0) * ka_ref[...])
        yield
        tri = tri_ref[...]
        cum = None
        for part in _split_bf16(log_d, 3):
            term = mm(tri, part)
            cum = term if cum is None else cum + term
        yield
        ops["bonus"] = _group_sum_wide(r * k * rk_ref[...], ones_blk, 3) * v
        total = cum[chunk - 1:chunk, :]
        p_inv = jnp.exp(-cum)
        p_rest = jnp.exp(total - cum)
        beta = kk * a
        ops.update(at=-kk * jnp.exp(cum - log_d), rt=r * jnp.exp(cum), bt=beta * p_inv,
                   kt=k * p_inv, bc=beta * p_rest, kc=k * p_rest, v=v, p_total=jnp.exp(total))
        yield

    def chains(q, ops):
        def stack(x, j):
            return _stack_heads(x[:, j * LANES:(j + 1) * LANES]).astype(bf16)

        ar = [jnp.concatenate([stack(ops["at"], j), stack(ops["rt"], j)], axis=0) for j in groups]
        bk = [jnp.concatenate([stack(ops["bt"], j), stack(ops["kt"], j)], axis=0) for j in groups]
        v2 = [stack(ops["v"], j) for j in groups]
        s_pack = [s_ref[q, j] for j in groups]
        scores = [_nt_dot(ar[j], bk[j]) for j in groups]
        yield
        from_state = [_nt_dot(ar[j], _stack_heads(s_pack[j]).astype(bf16)) for j in groups]
        yield
        l_ab = [jnp.where(strict, scores[j][:c2, :c2], 0.0) for j in groups]
        l_b = [l_ab[j].astype(bf16) for j in groups]
        power = [mm(l_b[j], l_b[j]) for j in groups]
        yield
        w2 = [from_state[j][:c2]
              + mm(jnp.where(strict, scores[j][:c2, c2:], 0.0).astype(bf16), v2[j])
              for j in groups]
        yield
        y_part = [from_state[j][c2:]
                  + mm(jnp.where(incl, scores[j][c2:, c2:], 0.0).astype(bf16), v2[j])
                  for j in groups]
        yield
        inv = [eye + l_ab[j] for j in groups]
        span = 2
        while span < chunk:
            if 2 * span < chunk:
                both = [mm(jnp.concatenate([inv[j], power[j]], axis=0).astype(bf16),
                           power[j].astype(bf16)) for j in groups]
                inv = [inv[j] + both[j][:c2] for j in groups]
                power = [both[j][c2:] for j in groups]
            else:
                inv = [inv[j] + mm(inv[j].astype(bf16), power[j].astype(bf16)) for j in groups]
            span *= 2
            yield
        u2 = [mm(inv[j].astype(bf16), w2[j].astype(bf16)).astype(bf16) for j in groups]
        yield
        y2 = [y_part[j] + mm(jnp.where(incl, scores[j][c2:, :c2], 0.0).astype(bf16), u2[j])
              for j in groups]
        ops["y"] = jnp.concatenate([y2[j][:chunk] + y2[j][chunk:] for j in groups], axis=1)
        yield
        s_new = [_tn_dot(u2[j], stack(ops["bc"], j)) + _tn_dot(v2[j], stack(ops["kc"], j))
                 for j in groups]
        for j in groups:
            s_ref[q, j] = (s_pack[j] * ops["p_total"][:, j * LANES:(j + 1) * LANES]
                           + s_new[j][:HEAD_DIM] + s_new[j][HEAD_DIM:])
        yield

    def norm_gate(q, ops):
        y = ops["y"]
        mean = _group_sum_wide(y, ones_blk, 3) * (1.0 / HEAD_DIM)
        yield
        ycen = y - mean
        var = _group_sum_wide(ycen * ycen, ones_blk, 3) * (1.0 / HEAD_DIM)
        yield
        yn = ycen * lax.rsqrt(var + GN_EPS) * lnw_ref[...] + lnb_ref[...]
        mix_ref[q, :, CONV_DIM:CONV_DIM + RWKV_DIM] = (yn + ops["bonus"]) * ops["g"]
        yield

    ops = [dict() for _ in range(n_seq)]
    _drain(per_token(0, ops[0]))
    for q in range(n_seq):
        sides = []
        if q + 1 < n_seq:
            sides.append(per_token(q + 1, ops[q + 1]))
        if q > 0:
            sides.append(norm_gate(q - 1, ops[q - 1]))
        _interleave(chains(q, ops[q]), sides)
    _drain(norm_gate(n_seq - 1, ops[n_seq - 1]))

    @pl.when(t_idx == n_t - 1)
    def _():
        convout_ref[...] = cu_ref[...]
        wkvout_ref[...] = s_ref[...]


def _chunk_mixer(p, p_prev, conv0, wkv0, ones_blk, lp):
    bsz, t_len, _ = p.shape
    chunk = TIME_CHUNK
    ns = CHUNK_SEQS
    assert t_len % chunk == 0 and chunk & (chunk - 1) == 0 and bsz % ns == 0
    tri = (jnp.arange(chunk)[:, None] >= jnp.arange(chunk)[None, :]).astype(bf16)
    const = lambda shape: pl.BlockSpec(shape, lambda i, t: (0,) * len(shape))
    in_specs = [
        pl.BlockSpec((ns, chunk, IN_PROJ), lambda i, t: (i, t, 0)),
        pl.BlockSpec((ns, 1, RWKV_PROJ), lambda i, t: (i, 0, 0)),
        pl.BlockSpec((ns, CONV_WIDTH - 1, CONV_DIM), lambda i, t: (i, 0, 0)),
        pl.BlockSpec((ns, HEAD_GROUPS, HEAD_DIM, LANES), lambda i, t: (i, 0, 0, 0)),
        const((LANES, LANES)),
        const((chunk, chunk)),
        const((1, RWKV_PROJ)),
        const((CONV_WIDTH, CONV_DIM)),
        const((1, CONV_DIM)),
        const((1, RWKV_DIM)),
        const((LORA_DECAY + LORA_A, RWKV_DIM)),
        const((1, RWKV_DIM)),
        const((LORA_DECAY + LORA_A, RWKV_DIM)),
        const((LORA_GATE, RWKV_DIM)),
        const((1, RWKV_DIM)),
        const((1, RWKV_DIM)),
        const((1, RWKV_DIM)),
        const((1, RWKV_DIM)),
        const((1, RWKV_DIM)),
    ]
    out_specs = [
        pl.BlockSpec((ns, chunk, D_MODEL), lambda i, t: (i, t, 0)),
        pl.BlockSpec((ns, CONV_WIDTH - 1, CONV_DIM), lambda i, t: (i, 0, 0)),
        pl.BlockSpec((ns, HEAD_GROUPS, HEAD_DIM, LANES), lambda i, t: (i, 0, 0, 0)),
    ]
    out_shape = [
        jax.ShapeDtypeStruct((bsz, t_len, D_MODEL), f32),
        jax.ShapeDtypeStruct((bsz, CONV_WIDTH - 1, CONV_DIM), f32),
        jax.ShapeDtypeStruct((bsz, HEAD_GROUPS, HEAD_DIM, LANES), f32),
    ]
    scratch = [
        pltpu.VMEM((ns, HEAD_GROUPS, HEAD_DIM, LANES), f32),
        pltpu.VMEM((ns, 1, RWKV_PROJ), f32),
        pltpu.VMEM((ns, CONV_WIDTH - 1, CONV_DIM), f32),
    ]
    return pl.pallas_call(
        functools.partial(_chunk_mixer_kernel, chunk=chunk, n_seq=ns),
        grid=(bsz // ns, t_len // chunk),
        in_specs=in_specs,
        out_specs=out_specs,
        out_shape=out_shape,
        scratch_shapes=scratch,
        compiler_params=pltpu.CompilerParams(
            dimension_semantics=("arbitrary", "arbitrary"),
            vmem_limit_bytes=VMEM_LIMIT_BYTES),
        name="chunk_mixer",
    )(p, p_prev, conv0, wkv0, ones_blk, tri,
      lp["mu"], lp["conv_w"], lp["conv_norm"], lp["w0"], lp["wdu"], lp["a0"], lp["aup"],
      lp["gup"], lp["k_k"], lp["k_a"], lp["r_k"], lp["ln_w"], lp["ln_b"])


def _ffn_kernel(x_ref, mix_ref, wout_ref, g2_ref, wg_ref, wu_ref, wd_ref, o_ref):
    x = x_ref[...] + jnp.dot(mix_ref[...].astype(bf16), wout_ref[...],
                             preferred_element_type=f32)
    h = x * lax.rsqrt(jnp.mean(x * x, axis=-1, keepdims=True) + RMS_EPS) * g2_ref[...]
    hb = h.astype(bf16)
    gate = jnp.dot(hb, wg_ref[...], preferred_element_type=f32)
    up = jnp.dot(hb, wu_ref[...], preferred_element_type=f32)
    act = gate * _sigmoid(gate) * up
    o_ref[...] = x + jnp.dot(act.astype(bf16), wd_ref[...], preferred_element_type=f32)


def _ffn(x, mix, lp):
    n, d = x.shape
    tile = min(ROW_TILE, n)
    resident = lambda shape: pl.BlockSpec(shape, lambda i: (0, 0),
                                          pipeline_mode=pl.Buffered(1))
    return pl.pallas_call(
        _ffn_kernel,
        grid=(n // tile,),
        in_specs=[
            pl.BlockSpec((tile, d), lambda i: (i, 0)),
            pl.BlockSpec((tile, d), lambda i: (i, 0)),
            resident((d, d)),
            resident((1, d)),
            resident((d, D_FF)),
            resident((d, D_FF)),
            resident((D_FF, d)),
        ],
        out_specs=pl.BlockSpec((tile, d), lambda i: (i, 0)),
        out_shape=jax.ShapeDtypeStruct((n, d), f32),
        compiler_params=pltpu.CompilerParams(
            dimension_semantics=("arbitrary",), vmem_limit_bytes=VMEM_LIMIT_BYTES),
        name="outproj_ffn",
    )(x, mix, lp["w_out"], lp["norm2"], lp["w_gate"], lp["w_up"], lp["w_down"])


def _pack_wkv(s):
    b = s.shape[0]
    s = s.reshape(b, HEAD_GROUPS, HEADS_PER_GROUP, HEAD_DIM, HEAD_DIM)
    return s.transpose(0, 1, 3, 2, 4).reshape(b, HEAD_GROUPS, HEAD_DIM, LANES)


def _unpack_wkv(s):
    b = s.shape[0]
    s = s.reshape(b, HEAD_GROUPS, HEAD_DIM, HEADS_PER_GROUP, HEAD_DIM)
    return s.transpose(0, 1, 3, 2, 4).reshape(b, RWKV_HEADS, HEAD_DIM, HEAD_DIM)


def _prepare_params(norm1, w_in, mu_shift, conv_w, conv_norm, w0, w_decay_up, a0, a_up, g_up,
                    k_k, k_a, r_k, ln_x_w, ln_x_b, w_out, norm2, w_gate, w_up, w_down,
                    final_norm):
    row = lambda a: a.reshape(1, -1)
    zeros_lora = jnp.zeros((LORA_DECAY, RWKV_DIM), f32)
    layers = []
    for l in range(DEPTH):
        w_in_b = w_in[l].astype(bf16)
        layers.append(dict(
            norm1=row(norm1[l]), w_in=w_in_b, w_in_rwkv=w_in_b[:, 3 * CONV_DIM:],
            mu=row(mu_shift[l]), conv_w=conv_w[l], conv_norm=row(conv_norm[l]),
            w0=row(w0[l]),
            wdu=jnp.concatenate([w_decay_up[l], zeros_lora], axis=0).astype(bf16),
            a0=row(a0[l]),
            aup=jnp.concatenate([zeros_lora, a_up[l]], axis=0).astype(bf16),
            gup=g_up[l].astype(bf16),
            k_k=row(k_k[l]), k_a=row(k_a[l]), r_k=row(r_k[l]),
            ln_w=row(ln_x_w[l]), ln_b=row(ln_x_b[l]),
            w_out=w_out[l].astype(bf16), norm2=row(norm2[l]),
            w_gate=w_gate[l].astype(bf16), w_up=w_up[l].astype(bf16),
            w_down=w_down[l].astype(bf16)))
    lane = jnp.arange(LANES) // HEAD_DIM
    ones_blk = (lane[:, None] == lane[None, :]).astype(bf16)
    zero = jnp.zeros((LANES, LANES), bf16)
    pair_w = jnp.block([[ones_blk, zero], [zero, ones_blk]]).astype(f32)
    return layers, row(final_norm), ones_blk, pair_w


def _trunk(x, shift0, conv0, wkv0, prepared):
    layers, final_norm, ones_blk, pair_w = prepared
    bsz, t_len, d = x.shape
    chunked = t_len >= TIME_CHUNK
    xf = x.reshape(bsz * t_len, d) if chunked else x.transpose(1, 0, 2).reshape(t_len * bsz, d)
    shifts, convs, wkvs = [], [], []
    for l, lp in enumerate(layers):
        p = _inproj(xf, lp["norm1"], lp["w_in"], normalize=True)
        last = xf.reshape(bsz, t_len, d)[:, -1] if chunked else xf[(t_len - 1) * bsz:]
        shifts.append(_rmsnorm(last, lp["norm1"]))
        p_prev = _inproj(shift0[l], lp["norm1"], lp["w_in_rwkv"], normalize=False)
        if chunked:
            mix, conv_new, wkv_new = _chunk_mixer(
                p.reshape(bsz, t_len, IN_PROJ), p_prev.reshape(bsz, 1, RWKV_PROJ), conv0[l],
                _pack_wkv(wkv0[l]), ones_blk, lp)
        else:
            mix, conv_new, wkv_new = _mixer(
                p.reshape(t_len, bsz, IN_PROJ), p_prev, conv0[l].transpose(1, 0, 2),
                _pack_wkv(wkv0[l]), ones_blk, pair_w, lp)
            conv_new = conv_new.transpose(1, 0, 2)
        xf = _ffn(xf, mix.reshape(t_len * bsz, d), lp)
        convs.append(conv_new)
        wkvs.append(_unpack_wkv(wkv_new))
    y = _rmsnorm(xf, final_norm)
    y = y.reshape(bsz, t_len, d) if chunked else y.reshape(t_len, bsz, d).transpose(1, 0, 2)
    return y, jnp.stack(shifts), jnp.stack(convs), jnp.stack(wkvs)


def kernel(x_prompt, x_sample, state_shift, state_conv, state_wkv, norm1, w_in, mu_shift, conv_w, conv_norm, w0, w_decay_up, a0, a_up, g_up, k_k, k_a, r_k, ln_x_w, ln_x_b, w_out, norm2, w_gate, w_up, w_down, final_norm):
    prepared = _prepare_params(norm1, w_in, mu_shift, conv_w, conv_norm, w0, w_decay_up, a0,
                               a_up, g_up, k_k, k_a, r_k, ln_x_w, ln_x_b, w_out, norm2,
                               w_gate, w_up, w_down, final_norm)
    bp = x_prompt.shape[0]
    dt = x_prompt.dtype
    p_shift0 = jnp.zeros((DEPTH, bp, D_MODEL), dt)
    p_conv0 = jnp.zeros((DEPTH, bp, CONV_WIDTH - 1, CONV_DIM), dt)
    p_wkv0 = jnp.zeros((DEPTH, bp, RWKV_HEADS, HEAD_DIM, HEAD_DIM), state_wkv.dtype)
    y_p, sh_p, cv_p, wk_p = _trunk(x_prompt, p_shift0, p_conv0, p_wkv0, prepared)
    y_s, sh_s, cv_s, wk_s = _trunk(x_sample, state_shift, state_conv, state_wkv, prepared)
    return (y_p, y_s, sh_p, cv_p, wk_p, sh_s, cv_s, wk_s)
```

```python
import functools

import jax
import jax.numpy as jnp
from jax import lax
from jax.experimental import pallas as pl
from jax.experimental.pallas import tpu as pltpu

D_MODEL = 1024
DEPTH = 4
CONV_DIM = 512
CONV_WIDTH = 3
RWKV_DIM = 512
HEAD_DIM = 64
RWKV_HEADS = RWKV_DIM // HEAD_DIM
LORA_DECAY = 64
LORA_A = 64
LORA_GATE = 128
RWKV_PROJ = 3 * RWKV_DIM + LORA_DECAY + LORA_A + LORA_GATE
IN_PROJ = 3 * CONV_DIM + RWKV_PROJ
D_FF = 2816
RMS_EPS = 1e-6
GN_EPS = 64e-5

LANES = 128
SUBLANES = 8
HEADS_PER_GROUP = LANES // HEAD_DIM
HEAD_GROUPS = RWKV_HEADS // HEADS_PER_GROUP
VMEM_LIMIT_BYTES = 56 * 1024 * 1024
ROW_TILE = 512
SEQ_BLOCK = SUBLANES
TIME_CHUNK = 64
PREP_TOKENS = 8
STEP_SEQS = 2
CHUNK_SEQS = 8
CHAIN_SEQS = 2

f32 = jnp.float32
bf16 = jnp.bfloat16


def _split_bf16(x, terms):
    parts = []
    rem = x
    for i in range(terms):
        part = rem.astype(bf16)
        parts.append(part)
        if i + 1 < terms:
            rem = rem - part.astype(f32)
    return parts


def _group_sum(x, ones_blk, terms):
    acc = None
    for part in _split_bf16(x, terms):
        d = jnp.dot(part, ones_blk, preferred_element_type=f32)
        acc = d if acc is None else acc + d
    return acc


def _group_sum_wide(x, ones_blk, terms):
    rows = x.shape[0]
    n_blk = x.shape[1] // LANES
    lhs = jnp.concatenate(
        [part[:, c * LANES:(c + 1) * LANES] for part in _split_bf16(x, terms)
         for c in range(n_blk)], axis=0)
    out = jnp.dot(lhs, ones_blk, preferred_element_type=f32)
    cols = []
    for c in range(n_blk):
        acc = out[c * rows:(c + 1) * rows]
        for t in range(1, terms):
            acc = acc + out[(t * n_blk + c) * rows:(t * n_blk + c + 1) * rows]
        cols.append(acc)
    return jnp.concatenate(cols, axis=1)


def _sigmoid(x):
    return 1.0 / (1.0 + jnp.exp(-x))


def _inproj_kernel(x_ref, g_ref, w_ref, o_ref, *, normalize):
    x = x_ref[...]
    if normalize:
        x = x * lax.rsqrt(jnp.mean(x * x, axis=-1, keepdims=True) + RMS_EPS) * g_ref[...]
    o_ref[...] = jnp.dot(x.astype(bf16), w_ref[...], preferred_element_type=f32)


def _inproj(x, g, w, layer, *, normalize):
    n, d = x.shape
    m = w.shape[2]
    tile = min(ROW_TILE, n)
    return pl.pallas_call(
        functools.partial(_inproj_kernel, normalize=normalize),
        grid=(n // tile,),
        in_specs=[
            pl.BlockSpec((tile, d), lambda i: (i, 0)),
            pl.BlockSpec((1, d), lambda i: (0, 0)),
            pl.BlockSpec((None, d, m), lambda i: (layer, 0, 0)),
        ],
        out_specs=pl.BlockSpec((tile, m), lambda i: (i, 0)),
        out_shape=jax.ShapeDtypeStruct((n, m), f32),
        compiler_params=pltpu.CompilerParams(
            dimension_semantics=("arbitrary",), vmem_limit_bytes=VMEM_LIMIT_BYTES),
        name="inproj" if normalize else "shift_proj",
    )(x, g, w)


def _rmsnorm_kernel(x_ref, g_ref, o_ref):
    x = x_ref[...]
    o_ref[...] = x * lax.rsqrt(jnp.mean(x * x, axis=-1, keepdims=True) + RMS_EPS) * g_ref[...]


def _rmsnorm(x, g):
    n, d = x.shape
    tile = min(ROW_TILE, n)
    return pl.pallas_call(
        _rmsnorm_kernel,
        grid=(n // tile,),
        in_specs=[pl.BlockSpec((tile, d), lambda i: (i, 0)),
                  pl.BlockSpec((1, d), lambda i: (0, 0))],
        out_specs=pl.BlockSpec((tile, d), lambda i: (i, 0)),
        out_shape=jax.ShapeDtypeStruct((n, d), f32),
        compiler_params=pltpu.CompilerParams(dimension_semantics=("arbitrary",)),
        name="rmsnorm",
    )(x, g)


def _mixer_kernel(
        p_ref, pprev_ref, conv0_ref, wkv0_ref, ones_ref, pair_ref,
        mu_ref, convw_ref, convn_ref, w0_ref, wdu_ref, a0_ref, aup_ref, gup_ref,
        kk_ref, ka_ref, rk_ref, lnw_ref, lnb_ref,
        mix_ref, convout_ref, wkvout_ref,
        s_ref, cpr_ref, cu_ref,
        nkk_s, dr_s, d_s, b_s, k_s, vr_s, vkr_s, g_s, bonus_s, y_s,
        *, chunk, sub):
    t_idx = pl.program_id(1)
    n_t = pl.num_programs(1)
    nb = SEQ_BLOCK
    rows = sub * nb
    n_sub = chunk // sub

    @pl.when(t_idx == 0)
    def _():
        s_ref[...] = wkv0_ref[...]
        cpr_ref[...] = pprev_ref[...]
        cu_ref[...] = conv0_ref[...]

    ones_blk = ones_ref[...]

    def flat(ref, t0, lo, hi):
        return ref[pl.ds(t0, sub), :, lo:hi].reshape(rows, hi - lo)

    def unflat(x):
        return x.reshape(sub, nb, x.shape[-1])

    def conv_input(t0):
        return flat(p_ref, t0, 2 * CONV_DIM, 3 * CONV_DIM) * flat(p_ref, t0, 0, CONV_DIM)

    def prep_block(t0, first):
        blk = pl.ds(t0, sub)
        bc = flat(p_ref, t0, CONV_DIM, 2 * CONV_DIM)
        u = conv_input(t0)
        if first:
            um2, um1 = cu_ref[0], cu_ref[1]
            u1 = jnp.concatenate([um1, u[:rows - nb]], axis=0)
            u2 = jnp.concatenate([um2, um1, u[:rows - 2 * nb]], axis=0)
        else:
            u1 = conv_input(t0 - 1)
            u2 = conv_input(t0 - 2)
        conv = u2 * convw_ref[0:1, :] + u1 * convw_ref[1:2, :] + u * convw_ref[2:3, :]
        yc = bc * conv
        yc = yc * lax.rsqrt(jnp.mean(yc * yc, axis=-1, keepdims=True) + RMS_EPS) * convn_ref[...]
        mix_ref[blk, :, 0:CONV_DIM] = unflat(yc)

        pr = flat(p_ref, t0, 3 * CONV_DIM, IN_PROJ)
        if first:
            shifted = jnp.concatenate([cpr_ref[...], pr[:rows - nb]], axis=0)
        else:
            shifted = flat(p_ref, t0 - 1, 3 * CONV_DIM, IN_PROJ)
        m = pr + (shifted - pr) * mu_ref[...]
        o1, o2, o3 = RWKV_DIM, 2 * RWKV_DIM, 3 * RWKV_DIM
        r = m[:, :o1]
        k = m[:, o1:o2]
        v = m[:, o2:o3]
        wa_d = m[:, o3:o3 + LORA_DECAY + LORA_A]
        g_d = m[:, o3 + LORA_DECAY + LORA_A:]
        z = w0_ref[...] + jnp.dot(jnp.tanh(wa_d).astype(bf16), wdu_ref[...],
                                  preferred_element_type=f32)
        nz = -z
        softplus = jnp.maximum(nz, 0.0) + jnp.log1p(jnp.exp(-jnp.abs(nz)))
        w = -softplus - 0.5
        d = jnp.exp(-jnp.exp(w))
        a = _sigmoid(a0_ref[...] + jnp.dot(wa_d.astype(bf16), aup_ref[...],
                                           preferred_element_type=f32))
        g = jnp.dot(_sigmoid(g_d).astype(bf16), gup_ref[...], preferred_element_type=f32)
        kk = k * kk_ref[...]
        kk_norm = jnp.sqrt(_group_sum_wide(kk * kk, ones_blk, 3))
        kk = kk / jnp.maximum(kk_norm, 1e-12)
        k = k * (1.0 + (a - 1.0) * ka_ref[...])
        b = kk * a
        nkk_s[blk] = unflat(-kk)
        dr_s[blk] = unflat(d * r - kk * _group_sum_wide(b * r, ones_blk, 3))
        d_s[blk] = unflat(d)
        b_s[blk] = unflat(b)
        k_s[blk] = unflat(k)
        vr_s[blk] = unflat(v.astype(bf16).astype(f32))
        vkr_s[blk] = unflat(_group_sum_wide(k * r, ones_blk, 3) * v)
        g_s[blk] = unflat(g)
        bonus_s[blk] = unflat(_group_sum_wide(r * k * rk_ref[...], ones_blk, 3) * v)

    prep_block(0, True)
    if n_sub > 1:
        def prep_body(i, carry):
            prep_block(i * sub, False)
            return carry
        lax.fori_loop(1, n_sub, prep_body, 0)

    cu_ref[...] = (p_ref[chunk - 2:chunk, :, 2 * CONV_DIM:3 * CONV_DIM]
                   * p_ref[chunk - 2:chunk, :, 0:CONV_DIM])
    cpr_ref[...] = p_ref[chunk - 1, :, 3 * CONV_DIM:IN_PROJ]

    sub_i = lax.broadcasted_iota(jnp.int32, (HEAD_DIM, LANES), 0)
    lane_i = lax.broadcasted_iota(jnp.int32, (HEAD_DIM, LANES), 1)
    diag = (lane_i % HEAD_DIM) == sub_i

    pair_w = pair_ref[...]
    pair_b = pair_w.astype(bf16)

    def step(t, carry):
        for b0 in range(0, nb, STEP_SEQS):
            streams = [(bi, j) for bi in range(b0, b0 + STEP_SEQS) for j in range(HEAD_GROUPS)]

            def rowv(ref, bi, j):
                return ref[t, bi:bi + 1, j * LANES:(j + 1) * LANES]

            prod_rows = []
            for bi, j in streams:
                s = s_ref[bi, j]
                prod_rows.append(jnp.concatenate(
                    [(s * rowv(nkk_s, bi, j)).astype(bf16), (s * rowv(dr_s, bi, j)).astype(bf16)],
                    axis=1))
            sums = jnp.dot(jnp.concatenate(prod_rows, axis=0), pair_b,
                           preferred_element_type=f32)
            v_rows = []
            for n in range(0, len(streams), 2):
                v_rows.append(jnp.concatenate(
                    [jnp.where(diag, rowv(vr_s, *streams[n + c]), 0.0) for c in range(2)], axis=1))
            vbs = jnp.dot(jnp.concatenate(v_rows, axis=0), pair_w, preferred_element_type=f32)
            for n, (bi, j) in enumerate(streams):
                sa = sums[n * HEAD_DIM:(n + 1) * HEAD_DIM, 0:LANES]
                zz = sums[n * HEAD_DIM:(n + 1) * HEAD_DIM, LANES:2 * LANES]
                vrow0 = (n // 2) * HEAD_DIM
                vb = vbs[vrow0:vrow0 + HEAD_DIM, (n % 2) * LANES:(n % 2 + 1) * LANES]
                s = s_ref[bi, j]
                s_ref[bi, j] = s * rowv(d_s, bi, j) + sa * rowv(b_s, bi, j) + vb * rowv(k_s, bi, j)
                yp = jnp.where(diag, zz, 0.0)
                y_row = jnp.sum(yp, axis=0, keepdims=True) + rowv(vkr_s, bi, j)
                y_s[t, bi:bi + 1, j * LANES:(j + 1) * LANES] = y_row
        return carry

    lax.fori_loop(0, chunk, step, 0)

    def post_body(i, carry):
        blk = pl.ds(i * sub, sub)
        y = y_s[blk].reshape(rows, RWKV_DIM)
        mean = _group_sum_wide(y, ones_blk, 3) * (1.0 / HEAD_DIM)
        yc = y - mean
        var = _group_sum_wide(yc * yc, ones_blk, 3) * (1.0 / HEAD_DIM)
        yn = yc * lax.rsqrt(var + GN_EPS)
        yn = yn * lnw_ref[...] + lnb_ref[...]
        out = (yn + bonus_s[blk].reshape(rows, RWKV_DIM)) * g_s[blk].reshape(rows, RWKV_DIM)
        mix_ref[blk, :, CONV_DIM:CONV_DIM + RWKV_DIM] = unflat(out)
        return carry

    lax.fori_loop(0, n_sub, post_body, 0)

    @pl.when(t_idx == n_t - 1)
    def _():
        convout_ref[...] = cu_ref[...]
        wkvout_ref[...] = s_ref[...]


def _mixer(p, p_prev, conv0, wkv0, ones_blk, pair_w, lp):
    t_len, bsz, _ = p.shape
    nb = SEQ_BLOCK
    chunk = min(TIME_CHUNK, t_len)
    sub = min(PREP_TOKENS, chunk)
    assert sub >= CONV_WIDTH - 1 and chunk % sub == 0 and t_len % chunk == 0 and bsz % nb == 0
    grid = (bsz // nb, t_len // chunk)
    const = lambda shape: pl.BlockSpec(shape, lambda i, t: (0,) * len(shape))
    in_specs = [
        pl.BlockSpec((chunk, nb, IN_PROJ), lambda i, t: (t, i, 0)),
        pl.BlockSpec((nb, RWKV_PROJ), lambda i, t: (i, 0)),
        pl.BlockSpec((CONV_WIDTH - 1, nb, CONV_DIM), lambda i, t: (0, i, 0)),
        pl.BlockSpec((nb, HEAD_GROUPS, HEAD_DIM, LANES), lambda i, t: (i, 0, 0, 0)),
        const((LANES, LANES)),
        const((2 * LANES, 2 * LANES)),
        const((1, RWKV_PROJ)),
        const((CONV_WIDTH, CONV_DIM)),
        const((1, CONV_DIM)),
        const((1, RWKV_DIM)),
        const((LORA_DECAY + LORA_A, RWKV_DIM)),
        const((1, RWKV_DIM)),
        const((LORA_DECAY + LORA_A, RWKV_DIM)),
        const((LORA_GATE, RWKV_DIM)),
        const((1, RWKV_DIM)),
        const((1, RWKV_DIM)),
        const((1, RWKV_DIM)),
        const((1, RWKV_DIM)),
        const((1, RWKV_DIM)),
    ]
    out_specs = [
        pl.BlockSpec((chunk, nb, D_MODEL), lambda i, t: (t, i, 0)),
        pl.BlockSpec((CONV_WIDTH - 1, nb, CONV_DIM), lambda i, t: (0, i, 0)),
        pl.BlockSpec((nb, HEAD_GROUPS, HEAD_DIM, LANES), lambda i, t: (i, 0, 0, 0)),
    ]
    out_shape = [
        jax.ShapeDtypeStruct((t_len, bsz, D_MODEL), f32),
        jax.ShapeDtypeStruct((CONV_WIDTH - 1, bsz, CONV_DIM), f32),
        jax.ShapeDtypeStruct((bsz, HEAD_GROUPS, HEAD_DIM, LANES), f32),
    ]
    scratch = [
        pltpu.VMEM((nb, HEAD_GROUPS, HEAD_DIM, LANES), f32),
        pltpu.VMEM((nb, RWKV_PROJ), f32),
        pltpu.VMEM((CONV_WIDTH - 1, nb, CONV_DIM), f32),
    ] + [pltpu.VMEM((chunk, nb, RWKV_DIM), f32) for _ in range(10)]
    return pl.pallas_call(
        functools.partial(_mixer_kernel, chunk=chunk, sub=sub),
        grid=grid,
        in_specs=in_specs,
        out_specs=out_specs,
        out_shape=out_shape,
        scratch_shapes=scratch,
        compiler_params=pltpu.CompilerParams(
            dimension_semantics=("arbitrary", "arbitrary"),
            vmem_limit_bytes=VMEM_LIMIT_BYTES),
        name="mixer",
    )(p, p_prev, conv0, wkv0, ones_blk, pair_w,
      lp["mu"], lp["conv_w"], lp["conv_norm"], lp["w0"], lp["wdu"], lp["a0"], lp["aup"],
      lp["gup"], lp["k_k"], lp["k_a"], lp["r_k"], lp["ln_w"], lp["ln_b"])


def _nt_dot(a, b):
    return lax.dot_general(a, b, (((1,), (1,)), ((), ())), preferred_element_type=f32)


def _tn_dot(a, b):
    return lax.dot_general(a, b, (((0,), (0,)), ((), ())), preferred_element_type=f32)


def _stack_heads(x):
    lane = lax.broadcasted_iota(jnp.int32, x.shape, 1)
    return jnp.concatenate(
        [jnp.where(lane < HEAD_DIM, x, 0.0), jnp.where(lane >= HEAD_DIM, x, 0.0)], axis=0)


def _drain(gen):
    for _ in gen:
        pass


def _interleave(main, sides):
    for _ in main:
        for side in sides:
            next(side, None)
    for side in sides:
        _drain(side)


def _chunk_mixer_kernel(
        p_ref, pprev_ref, conv0_ref, wkv0_ref, ones_ref, tri_ref,
        mu_ref, convw_ref, convn_ref, w0_ref, wdu_ref, a0_ref, aup_ref, gup_ref,
        kk_ref, ka_ref, rk_ref, lnw_ref, lnb_ref,
        mix_ref, convout_ref, wkvout_ref,
        s_ref, cpr_ref, cu_ref,
        *, chunk, n_seq):
    t_idx = pl.program_id(1)
    n_t = pl.num_programs(1)
    c2 = 2 * chunk

    @pl.when(t_idx == 0)
    def _():
        s_ref[...] = wkv0_ref[...]
        cpr_ref[...] = pprev_ref[...]
        cu_ref[...] = conv0_ref[...]

    ones_blk = ones_ref[...]
    row = lax.broadcasted_iota(jnp.int32, (chunk, 1), 0)
    ri = lax.broadcasted_iota(jnp.int32, (c2, c2), 0) % chunk
    ci = lax.broadcasted_iota(jnp.int32, (c2, c2), 1) % chunk
    strict = ci < ri
    incl_wide = jnp.concatenate([ci <= ri, ci <= ri], axis=1)
    eye = (lax.broadcasted_iota(jnp.int32, (c2, c2), 0)
           == lax.broadcasted_iota(jnp.int32, (c2, c2), 1)).astype(f32)
    groups = range(HEAD_GROUPS)
    mm = functools.partial(jnp.dot, preferred_element_type=f32)

    def shift_rows(x, prev_rows):
        out = pltpu.roll(x, len(prev_rows), axis=0)
        for i, pr in enumerate(prev_rows):
            out = jnp.where(row == i, pr, out)
        return out

    def per_token(q, ops):
        u = p_ref[q, :, 2 * CONV_DIM:3 * CONV_DIM] * p_ref[q, :, 0:CONV_DIM]
        um2, um1 = cu_ref[q, 0:1, :], cu_ref[q, 1:2, :]
        conv = (shift_rows(u, [um2, um1]) * convw_ref[0:1, :]
                + shift_rows(u, [um1]) * convw_ref[1:2, :] + u * convw_ref[2:3, :])
        cu_ref[q] = u[chunk - 2:chunk, :]
        yc = p_ref[q, :, CONV_DIM:2 * CONV_DIM] * conv
        yc = yc * lax.rsqrt(jnp.mean(yc * yc, axis=-1, keepdims=True) + RMS_EPS) * convn_ref[...]
        mix_ref[q, :, 0:CONV_DIM] = yc

        pr = p_ref[q, :, 3 * CONV_DIM:IN_PROJ]
        shifted = shift_rows(pr, [cpr_ref[q]])
        cpr_ref[q] = pr[chunk - 1:chunk, :]
        m = pr + (shifted - pr) * mu_ref[...]
        o1, o2, o3 = RWKV_DIM, 2 * RWKV_DIM, 3 * RWKV_DIM
        r = m[:, :o1]
        k = m[:, o1:o2]
        v = m[:, o2:o3]
        wa_d = m[:, o3:o3 + LORA_DECAY + LORA_A]
        g_d = m[:, o3 + LORA_DECAY + LORA_A:]
        z = w0_ref[...] + mm(jnp.tanh(wa_d).astype(bf16), wdu_ref[...])
        a_lin = a0_ref[...] + mm(wa_d.astype(bf16), aup_ref[...])
        ops["g"] = mm(_sigmoid(g_d).astype(bf16), gup_ref[...])
        yield
        nz = -z
        softplus = jnp.maximum(nz, 0.0) + jnp.log1p(jnp.exp(-jnp.abs(nz)))
        log_d = -jnp.exp(-softplus - 0.5)
        a = _sigmoid(a_lin)
        kk = k * kk_ref[...]
        kk = kk / jnp.maximum(jnp.sqrt(_group_sum_wide(kk * kk, ones_blk, 3)), 1e-12)
        k = k * (1.0 + (a - 1.0) * ka_ref[...])
        yield
        tri = tri_ref[...]
        cum = None
        for part in _split_bf16(log_d, 3):
            term = mm(tri, part)
            cum = term if cum is None else cum + term
        yield
        ops["bonus"] = _group_sum_wide(r * k * rk_ref[...], ones_blk, 3) * v
        total = cum[chunk - 1:chunk, :]
        p_inv = jnp.exp(-cum)
        p_rest = jnp.exp(total - cum)
        beta = kk * a
        ops.update(at=-kk * jnp.exp(cum - log_d), rt=r * jnp.exp(cum), bt=beta * p_inv,
                   kt=k * p_inv, bc=beta * p_rest, kc=k * p_rest, v=v, p_total=jnp.exp(total))
        yield

    def chains(qs):
        units = [(q, j) for q in qs for j in groups]
        idx = range(len(units))

        def stack(name, n):
            q, j = units[n]
            return _stack_heads(ops[q][name][:, j * LANES:(j + 1) * LANES]).astype(bf16)

        ar = [jnp.concatenate([stack("at", n), stack("rt", n)], axis=0) for n in idx]
        bk = [jnp.concatenate([stack("bt", n), stack("kt", n)], axis=0) for n in idx]
        v2 = [stack("v", n) for n in idx]
        s_pack = [s_ref[q, j] for q, j in units]
        scores = [_nt_dot(ar[n], bk[n]) for n in idx]
        yield
        from_state = [_nt_dot(ar[n], _stack_heads(s_pack[n]).astype(bf16)) for n in idx]
        yield
        l_ab = [jnp.where(strict, scores[n][:c2, :c2], 0.0) for n in idx]
        l_b = [l_ab[n].astype(bf16) for n in idx]
        power = [mm(l_b[n], l_b[n]) for n in idx]
        yield
        w2 = [from_state[n][:c2]
              + mm(jnp.where(strict, scores[n][:c2, c2:], 0.0).astype(bf16), v2[n]) for n in idx]
        yield
        inv = [eye + l_ab[n] for n in idx]
        span = 2
        while span < chunk:
            if 2 * span < chunk:
                both = [mm(jnp.concatenate([inv[n], power[n]], axis=0).astype(bf16),
                           power[n].astype(bf16)) for n in idx]
                inv = [inv[n] + both[n][:c2] for n in idx]
                power = [both[n][c2:] for n in idx]
            else:
                inv = [inv[n] + mm(inv[n].astype(bf16), power[n].astype(bf16)) for n in idx]
            span *= 2
            yield
        u2 = [mm(inv[n].astype(bf16), w2[n].astype(bf16)).astype(bf16) for n in idx]
        yield
        uv = [jnp.concatenate([u2[n], v2[n]], axis=0) for n in idx]
        y2 = [from_state[n][c2:]
              + mm(jnp.where(incl_wide, scores[n][c2:], 0.0).astype(bf16), uv[n]) for n in idx]
        for q in qs:
            ops[q]["y"] = jnp.concatenate(
                [y2[n][:chunk] + y2[n][chunk:] for n in idx if units[n][0] == q], axis=1)
        yield
        s_new = [_tn_dot(uv[n], jnp.concatenate([stack("bc", n), stack("kc", n)], axis=0))
                 for n in idx]
        for n, (q, j) in enumerate(units):
            s_ref[q, j] = (s_pack[n] * ops[q]["p_total"][:, j * LANES:(j + 1) * LANES]
                           + s_new[n][:HEAD_DIM] + s_new[n][HEAD_DIM:])
        yield

    def norm_gate(q, ops):
        y = ops["y"]
        mean = _group_sum_wide(y, ones_blk, 3) * (1.0 / HEAD_DIM)
        yield
        ycen = y - mean
        var = _group_sum_wide(ycen * ycen, ones_blk, 3) * (1.0 / HEAD_DIM)
        yield
        yn = ycen * lax.rsqrt(var + GN_EPS) * lnw_ref[...] + lnb_ref[...]
        mix_ref[q, :, CONV_DIM:CONV_DIM + RWKV_DIM] = (yn + ops["bonus"]) * ops["g"]
        yield

    ops = [dict() for _ in range(n_seq)]
    waves = [range(w, w + CHAIN_SEQS) for w in range(0, n_seq, CHAIN_SEQS)]
    for q in waves[0]:
        _drain(per_token(q, ops[q]))
    for w, wave in enumerate(waves):
        sides = [per_token(q, ops[q]) for q in (waves[w + 1] if w + 1 < len(waves) else ())]
        sides += [norm_gate(q, ops[q]) for q in (waves[w - 1] if w > 0 else ())]
        _interleave(chains(wave), sides)
    for q in waves[-1]:
        _drain(norm_gate(q, ops[q]))

    @pl.when(t_idx == n_t - 1)
    def _():
        convout_ref[...] = cu_ref[...]
        wkvout_ref[...] = s_ref[...]


def _chunk_mixer(p, p_prev, conv0, wkv0, ones_blk, lp):
    bsz, t_len, _ = p.shape
    chunk = TIME_CHUNK
    ns = CHUNK_SEQS
    assert t_len % chunk == 0 and chunk & (chunk - 1) == 0 and bsz % ns == 0
    tri = (jnp.arange(chunk)[:, None] >= jnp.arange(chunk)[None, :]).astype(bf16)
    const = lambda shape: pl.BlockSpec(shape, lambda i, t: (0,) * len(shape))
    in_specs = [
        pl.BlockSpec((ns, chunk, IN_PROJ), lambda i, t: (i, t, 0)),
        pl.BlockSpec((ns, 1, RWKV_PROJ), lambda i, t: (i, 0, 0)),
        pl.BlockSpec((ns, CONV_WIDTH - 1, CONV_DIM), lambda i, t: (i, 0, 0)),
        pl.BlockSpec((ns, HEAD_GROUPS, HEAD_DIM, LANES), lambda i, t: (i, 0, 0, 0)),
        const((LANES, LANES)),
        const((chunk, chunk)),
        const((1, RWKV_PROJ)),
        const((CONV_WIDTH, CONV_DIM)),
        const((1, CONV_DIM)),
        const((1, RWKV_DIM)),
        const((LORA_DECAY + LORA_A, RWKV_DIM)),
        const((1, RWKV_DIM)),
        const((LORA_DECAY + LORA_A, RWKV_DIM)),
        const((LORA_GATE, RWKV_DIM)),
        const((1, RWKV_DIM)),
        const((1, RWKV_DIM)),
        const((1, RWKV_DIM)),
        const((1, RWKV_DIM)),
        const((1, RWKV_DIM)),
    ]
    out_specs = [
        pl.BlockSpec((ns, chunk, D_MODEL), lambda i, t: (i, t, 0)),
        pl.BlockSpec((ns, CONV_WIDTH - 1, CONV_DIM), lambda i, t: (i, 0, 0)),
        pl.BlockSpec((ns, HEAD_GROUPS, HEAD_DIM, LANES), lambda i, t: (i, 0, 0, 0)),
    ]
    out_shape = [
        jax.ShapeDtypeStruct((bsz, t_len, D_MODEL), f32),
        jax.ShapeDtypeStruct((bsz, CONV_WIDTH - 1, CONV_DIM), f32),
        jax.ShapeDtypeStruct((bsz, HEAD_GROUPS, HEAD_DIM, LANES), f32),
    ]
    scratch = [
        pltpu.VMEM((ns, HEAD_GROUPS, HEAD_DIM, LANES), f32),
        pltpu.VMEM((ns, 1, RWKV_PROJ), f32),
        pltpu.VMEM((ns, CONV_WIDTH - 1, CONV_DIM), f32),
    ]
    return pl.pallas_call(
        functools.partial(_chunk_mixer_kernel, chunk=chunk, n_seq=ns),
        grid=(bsz // ns, t_len // chunk),
        in_specs=in_specs,
        out_specs=out_specs,
        out_shape=out_shape,
        scratch_shapes=scratch,
        compiler_params=pltpu.CompilerParams(
            dimension_semantics=("arbitrary", "arbitrary"),
            vmem_limit_bytes=VMEM_LIMIT_BYTES),
        name="chunk_mixer",
    )(p, p_prev, conv0, wkv0, ones_blk, tri,
      lp["mu"], lp["conv_w"], lp["conv_norm"], lp["w0"], lp["wdu"], lp["a0"], lp["aup"],
      lp["gup"], lp["k_k"], lp["k_a"], lp["r_k"], lp["ln_w"], lp["ln_b"])


def _rms(x, g):
    return x * lax.rsqrt(jnp.mean(x * x, axis=-1, keepdims=True) + RMS_EPS) * g


def _ffn_kernel(x_ref, mix_ref, wout_ref, g2_ref, wg_ref, wu_ref, wd_ref, gf_ref, o_ref, *,
                final):
    x = x_ref[...] + jnp.dot(mix_ref[...].astype(bf16), wout_ref[...],
                             preferred_element_type=f32)
    hb = _rms(x, g2_ref[...]).astype(bf16)
    gate = jnp.dot(hb, wg_ref[...], preferred_element_type=f32)
    up = jnp.dot(hb, wu_ref[...], preferred_element_type=f32)
    act = gate * _sigmoid(gate) * up
    out = x + jnp.dot(act.astype(bf16), wd_ref[...], preferred_element_type=f32)
    o_ref[...] = _rms(out, gf_ref[...]) if final else out


def _ffn(x, mix, lp, final_norm, *, final):
    n, d = x.shape
    tile = min(ROW_TILE, n)
    layer = lp["layer"]
    resident = lambda rows, cols: pl.BlockSpec((None, rows, cols), lambda i: (layer, 0, 0),
                                               pipeline_mode=pl.Buffered(1))
    return pl.pallas_call(
        functools.partial(_ffn_kernel, final=final),
        grid=(n // tile,),
        in_specs=[
            pl.BlockSpec((tile, d), lambda i: (i, 0)),
            pl.BlockSpec((tile, d), lambda i: (i, 0)),
            resident(d, d),
            pl.BlockSpec((1, d), lambda i: (0, 0)),
            resident(d, D_FF),
            resident(d, D_FF),
            resident(D_FF, d),
            pl.BlockSpec((1, d), lambda i: (0, 0)),
        ],
        out_specs=pl.BlockSpec((tile, d), lambda i: (i, 0)),
        out_shape=jax.ShapeDtypeStruct((n, d), f32),
        compiler_params=pltpu.CompilerParams(
            dimension_semantics=("arbitrary",), vmem_limit_bytes=VMEM_LIMIT_BYTES),
        name="outproj_ffn",
    )(x, mix, lp["w_out"], lp["norm2"], lp["w_gate"], lp["w_up"], lp["w_down"], final_norm)


def _pack_wkv(s):
    b = s.shape[0]
    s = s.reshape(b, HEAD_GROUPS, HEADS_PER_GROUP, HEAD_DIM, HEAD_DIM)
    return s.transpose(0, 1, 3, 2, 4).reshape(b, HEAD_GROUPS, HEAD_DIM, LANES)


def _unpack_wkv(s):
    b = s.shape[0]
    s = s.reshape(b, HEAD_GROUPS, HEAD_DIM, HEADS_PER_GROUP, HEAD_DIM)
    return s.transpose(0, 1, 3, 2, 4).reshape(b, RWKV_HEADS, HEAD_DIM, HEAD_DIM)


def _prepare_params(norm1, w_in, mu_shift, conv_w, conv_norm, w0, w_decay_up, a0, a_up, g_up,
                    k_k, k_a, r_k, ln_x_w, ln_x_b, w_out, norm2, w_gate, w_up, w_down,
                    final_norm):
    row = lambda a: a.reshape(1, -1)
    zeros_lora = jnp.zeros((LORA_DECAY, RWKV_DIM), f32)
    w_in_b = w_in.astype(bf16)
    stacked = dict(w_in=w_in_b, w_in_rwkv=w_in_b[:, :, 3 * CONV_DIM:],
                   w_out=w_out.astype(bf16), w_gate=w_gate.astype(bf16),
                   w_up=w_up.astype(bf16), w_down=w_down.astype(bf16))
    layers = []
    for l in range(DEPTH):
        layers.append(dict(
            stacked, layer=l, norm1=row(norm1[l]),
            mu=row(mu_shift[l]), conv_w=conv_w[l], conv_norm=row(conv_norm[l]),
            w0=row(w0[l]),
            wdu=jnp.concatenate([w_decay_up[l], zeros_lora], axis=0).astype(bf16),
            a0=row(a0[l]),
            aup=jnp.concatenate([zeros_lora, a_up[l]], axis=0).astype(bf16),
            gup=g_up[l].astype(bf16),
            k_k=row(k_k[l]), k_a=row(k_a[l]), r_k=row(r_k[l]),
            ln_w=row(ln_x_w[l]), ln_b=row(ln_x_b[l]), norm2=row(norm2[l])))
    lane = jnp.arange(LANES) // HEAD_DIM
    ones_blk = (lane[:, None] == lane[None, :]).astype(bf16)
    zero = jnp.zeros((LANES, LANES), bf16)
    pair_w = jnp.block([[ones_blk, zero], [zero, ones_blk]]).astype(f32)
    return layers, row(final_norm), ones_blk, pair_w


def _trunk(x, shift0, conv0, wkv0, prepared):
    layers, final_norm, ones_blk, pair_w = prepared
    bsz, t_len, d = x.shape
    chunked = t_len >= TIME_CHUNK
    xf = x.reshape(bsz * t_len, d) if chunked else x.transpose(1, 0, 2).reshape(t_len * bsz, d)
    shifts, convs, wkvs = [], [], []
    for l, lp in enumerate(layers):
        p = _inproj(xf, lp["norm1"], lp["w_in"], l, normalize=True)
        last = xf.reshape(bsz, t_len, d)[:, -1] if chunked else xf[(t_len - 1) * bsz:]
        shifts.append(_rmsnorm(last, lp["norm1"]))
        p_prev = _inproj(shift0[l], lp["norm1"], lp["w_in_rwkv"], l, normalize=False)
        if chunked:
            mix, conv_new, wkv_new = _chunk_mixer(
                p.reshape(bsz, t_len, IN_PROJ), p_prev.reshape(bsz, 1, RWKV_PROJ), conv0[l],
                _pack_wkv(wkv0[l]), ones_blk, lp)
        else:
            mix, conv_new, wkv_new = _mixer(
                p.reshape(t_len, bsz, IN_PROJ), p_prev, conv0[l].transpose(1, 0, 2),
                _pack_wkv(wkv0[l]), ones_blk, pair_w, lp)
            conv_new = conv_new.transpose(1, 0, 2)
        xf = _ffn(xf, mix.reshape(t_len * bsz, d), lp, final_norm, final=(l == DEPTH - 1))
        convs.append(conv_new)
        wkvs.append(_unpack_wkv(wkv_new))
    y = xf.reshape(bsz, t_len, d) if chunked else xf.reshape(t_len, bsz, d).transpose(1, 0, 2)
    return y, jnp.stack(shifts), jnp.stack(convs), jnp.stack(wkvs)


def kernel(x_prompt, x_sample, state_shift, state_conv, state_wkv, norm1, w_in, mu_shift, conv_w, conv_norm, w0, w_decay_up, a0, a_up, g_up, k_k, k_a, r_k, ln_x_w, ln_x_b, w_out, norm2, w_gate, w_up, w_down, final_norm):
    prepared = _prepare_params(norm1, w_in, mu_shift, conv_w, conv_norm, w0, w_decay_up, a0,
                               a_up, g_up, k_k, k_a, r_k, ln_x_w, ln_x_b, w_out, norm2,
                               w_gate, w_up, w_down, final_norm)
    bp = x_prompt.shape[0]
    dt = x_prompt.dtype
    p_shift0 = jnp.zeros((DEPTH, bp, D_MODEL), dt)
    p_conv0 = jnp.zeros((DEPTH, bp, CONV_WIDTH - 1, CONV_DIM), dt)
    p_wkv0 = jnp.zeros((DEPTH, bp, RWKV_HEADS, HEAD_DIM, HEAD_DIM), state_wkv.dtype)
    y_p, sh_p, cv_p, wk_p = _trunk(x_prompt, p_shift0, p_conv0, p_wkv0, prepared)
    y_s, sh_s, cv_s, wk_s = _trunk(x_sample, state_shift, state_conv, state_wkv, prepared)
    return (y_p, y_s, sh_p, cv_p, wk_p, sh_s, cv_s, wk_s)
```

```python
import functools

import jax
import jax.numpy as jnp
from jax import lax
from jax.experimental import pallas as pl
from jax.experimental.pallas import tpu as pltpu

D_MODEL = 1024
DEPTH = 4
CONV_DIM = 512
CONV_WIDTH = 3
RWKV_DIM = 512
HEAD_DIM = 64
RWKV_HEADS = RWKV_DIM // HEAD_DIM
LORA_DECAY = 64
LORA_A = 64
LORA_GATE = 128
RWKV_PROJ = 3 * RWKV_DIM + LORA_DECAY + LORA_A + LORA_GATE
IN_PROJ = 3 * CONV_DIM + RWKV_PROJ
D_FF = 2816
RMS_EPS = 1e-6
GN_EPS = 64e-5

LANES = 128
SUBLANES = 8
HEADS_PER_GROUP = LANES // HEAD_DIM
HEAD_GROUPS = RWKV_HEADS // HEADS_PER_GROUP
VMEM_LIMIT_BYTES = 56 * 1024 * 1024
ROW_TILE = 512
SLAB_ROWS = 64
LONG_SLABS = 8
SHORT_SLABS = 4
CHAIN_SLABS = 2

f32 = jnp.float32
bf16 = jnp.bfloat16


def _split_bf16(x, terms):
    parts = []
    rem = x
    for i in range(terms):
        part = rem.astype(bf16)
        parts.append(part)
        if i + 1 < terms:
            rem = rem - part.astype(f32)
    return parts


def _group_sum_wide(x, ones_blk, terms):
    rows = x.shape[0]
    n_blk = x.shape[1] // LANES
    lhs = jnp.concatenate(
        [part[:, c * LANES:(c + 1) * LANES] for part in _split_bf16(x, terms)
         for c in range(n_blk)], axis=0)
    out = jnp.dot(lhs, ones_blk, preferred_element_type=f32)
    cols = []
    for c in range(n_blk):
        acc = out[c * rows:(c + 1) * rows]
        for t in range(1, terms):
            acc = acc + out[(t * n_blk + c) * rows:(t * n_blk + c + 1) * rows]
        cols.append(acc)
    return jnp.concatenate(cols, axis=1)


def _sigmoid(x):
    return 1.0 / (1.0 + jnp.exp(-x))


def _rms(x, g):
    return x * lax.rsqrt(jnp.mean(x * x, axis=-1, keepdims=True) + RMS_EPS) * g


def _inproj_kernel(x_ref, g_ref, w_ref, o_ref, *, normalize):
    x = x_ref[...]
    if normalize:
        x = _rms(x, g_ref[...])
    o_ref[...] = jnp.dot(x.astype(bf16), w_ref[...], preferred_element_type=f32)


def _inproj(x, g, w, layer, *, normalize):
    n, d = x.shape
    m = w.shape[2]
    tile = min(ROW_TILE, n)
    return pl.pallas_call(
        functools.partial(_inproj_kernel, normalize=normalize),
        grid=(n // tile,),
        in_specs=[
            pl.BlockSpec((tile, d), lambda i: (i, 0)),
            pl.BlockSpec((1, d), lambda i: (0, 0)),
            pl.BlockSpec((None, d, m), lambda i: (layer, 0, 0)),
        ],
        out_specs=pl.BlockSpec((tile, m), lambda i: (i, 0)),
        out_shape=jax.ShapeDtypeStruct((n, m), f32),
        compiler_params=pltpu.CompilerParams(
            dimension_semantics=("arbitrary",), vmem_limit_bytes=VMEM_LIMIT_BYTES),
        name="inproj" if normalize else "shift_proj",
    )(x, g, w)


def _rmsnorm_kernel(x_ref, g_ref, o_ref):
    o_ref[...] = _rms(x_ref[...], g_ref[...])


def _rmsnorm(x, g):
    n, d = x.shape
    tile = min(ROW_TILE, n)
    return pl.pallas_call(
        _rmsnorm_kernel,
        grid=(n // tile,),
        in_specs=[pl.BlockSpec((tile, d), lambda i: (i, 0)),
                  pl.BlockSpec((1, d), lambda i: (0, 0))],
        out_specs=pl.BlockSpec((tile, d), lambda i: (i, 0)),
        out_shape=jax.ShapeDtypeStruct((n, d), f32),
        compiler_params=pltpu.CompilerParams(dimension_semantics=("arbitrary",)),
        name="rmsnorm",
    )(x, g)


def _nt_dot(a, b):
    return lax.dot_general(a, b, (((1,), (1,)), ((), ())), preferred_element_type=f32)


def _tn_dot(a, b):
    return lax.dot_general(a, b, (((0,), (0,)), ((), ())), preferred_element_type=f32)


def _stack_heads(x):
    lane = lax.broadcasted_iota(jnp.int32, x.shape, 1)
    return jnp.concatenate(
        [jnp.where(lane < HEAD_DIM, x, 0.0), jnp.where(lane >= HEAD_DIM, x, 0.0)], axis=0)


def _drain(gen):
    for _ in gen:
        pass


def _interleave(main, sides):
    for _ in main:
        for side in sides:
            next(side, None)
    for side in sides:
        _drain(side)


def _chunk_mixer_kernel(
        p_ref, pprev_ref, conv0_ref, wkv0_ref, ones_ref, tri_ref,
        mu_ref, convw_ref, convn_ref, w0_ref, wdu_ref, a0_ref, aup_ref, gup_ref,
        kk_ref, ka_ref, rk_ref, lnw_ref, lnb_ref,
        mix_ref, convout_ref, wkvout_ref,
        s_ref, cpr_ref, cu_ref,
        *, n_slab, c_tok):
    t_idx = pl.program_id(1)
    n_t = pl.num_programs(1)
    rows = SLAB_ROWS
    n_seq = rows // c_tok
    c2 = 2 * rows

    @pl.when(t_idx == 0)
    def _():
        s_ref[...] = wkv0_ref[...]
        cpr_ref[...] = pprev_ref[...]
        cu_ref[...] = conv0_ref[...]

    ones_blk = ones_ref[...]
    t_pos = lax.broadcasted_iota(jnp.int32, (rows, 1), 0) % c_tok
    ri = lax.broadcasted_iota(jnp.int32, (c2, c2), 0) % rows
    ci = lax.broadcasted_iota(jnp.int32, (c2, c2), 1) % rows
    same_seq = (ri // c_tok) == (ci // c_tok)
    strict = same_seq & (ci % c_tok < ri % c_tok)
    incl = same_seq & (ci % c_tok <= ri % c_tok)
    incl_wide = jnp.concatenate([incl, incl], axis=1)
    eye = (lax.broadcasted_iota(jnp.int32, (c2, c2), 0)
           == lax.broadcasted_iota(jnp.int32, (c2, c2), 1)).astype(f32)
    groups = range(HEAD_GROUPS)
    seqs = range(n_seq)
    mm = functools.partial(jnp.dot, preferred_element_type=f32)

    def per_seq_rows(load_row):
        return jnp.concatenate(
            [jnp.broadcast_to(load_row(b), (c_tok, load_row(b).shape[1])) for b in seqs], axis=0)

    def shift_rows(x, fills):
        out = pltpu.roll(x, len(fills), axis=0)
        for i, fill in enumerate(fills):
            out = jnp.where(t_pos == i, fill, out)
        return out

    def last_rows(x, back):
        return [x[b * c_tok + c_tok - back:b * c_tok + c_tok - back + 1, :] for b in seqs]

    def per_token(q, ops):
        u = p_ref[q, :, 2 * CONV_DIM:3 * CONV_DIM] * p_ref[q, :, 0:CONV_DIM]
        um2 = per_seq_rows(lambda b: cu_ref[q, b, 0:1, :])
        um1 = per_seq_rows(lambda b: cu_ref[q, b, 1:2, :])
        conv = (shift_rows(u, [um2, um1]) * convw_ref[0:1, :]
                + shift_rows(u, [um1]) * convw_ref[1:2, :] + u * convw_ref[2:3, :])
        for b, (r2, r1) in enumerate(zip(last_rows(u, 2), last_rows(u, 1))):
            cu_ref[q, b, 0:1, :] = r2
            cu_ref[q, b, 1:2, :] = r1
        yc = p_ref[q, :, CONV_DIM:2 * CONV_DIM] * conv
        mix_ref[q, :, 0:CONV_DIM] = _rms(yc, convn_ref[...])

        pr = p_ref[q, :, 3 * CONV_DIM:IN_PROJ]
        shifted = shift_rows(pr, [per_seq_rows(lambda b: cpr_ref[q, b:b + 1, :])])
        for b, r1 in enumerate(last_rows(pr, 1)):
            cpr_ref[q, b:b + 1, :] = r1
        m = pr + (shifted - pr) * mu_ref[...]
        o1, o2, o3 = RWKV_DIM, 2 * RWKV_DIM, 3 * RWKV_DIM
        r = m[:, :o1]
        k = m[:, o1:o2]
        v = m[:, o2:o3]
        wa_d = m[:, o3:o3 + LORA_DECAY + LORA_A]
        g_d = m[:, o3 + LORA_DECAY + LORA_A:]
        z = w0_ref[...] + mm(jnp.tanh(wa_d).astype(bf16), wdu_ref[...])
        a_lin = a0_ref[...] + mm(wa_d.astype(bf16), aup_ref[...])
        ops["g"] = mm(_sigmoid(g_d).astype(bf16), gup_ref[...])
        yield
        nz = -z
        softplus = jnp.maximum(nz, 0.0) + jnp.log1p(jnp.exp(-jnp.abs(nz)))
        log_d = -jnp.exp(-softplus - 0.5)
        a = _sigmoid(a_lin)
        kk = k * kk_ref[...]
        kk = kk / jnp.maximum(jnp.sqrt(_group_sum_wide(kk * kk, ones_blk, 3)), 1e-12)
        k = k * (1.0 + (a - 1.0) * ka_ref[...])
        yield
        tri = tri_ref[...]
        sums = None
        for part in _split_bf16(log_d, 3):
            term = mm(tri, part)
            sums = term if sums is None else sums + term
        cum, total = sums[:rows], sums[rows:]
        yield
        ops["bonus"] = _group_sum_wide(r * k * rk_ref[...], ones_blk, 3) * v
        p_inv = jnp.exp(-cum)
        p_rest = jnp.exp(total - cum)
        beta = kk * a
        ops.update(at=-kk * jnp.exp(cum - log_d), rt=r * jnp.exp(cum), bt=beta * p_inv,
                   kt=k * p_inv, bc=beta * p_rest, kc=k * p_rest, v=v, p_total=jnp.exp(total))
        yield

    def seq_rows(x2, b):
        return [x2[h * rows + b * c_tok:h * rows + (b + 1) * c_tok] for h in range(HEADS_PER_GROUP)]

    def chains(slabs):
        units = [(q, j) for q in slabs for j in groups]
        idx = range(len(units))

        def stack(name, n):
            q, j = units[n]
            return _stack_heads(ops[q][name][:, j * LANES:(j + 1) * LANES])

        at2 = [stack("at", n) for n in idx]
        rt2 = [stack("rt", n) for n in idx]
        v2f = [stack("v", n) for n in idx]
        v2 = [x.astype(bf16) for x in v2f]
        ar = [jnp.concatenate([at2[n], rt2[n]], axis=0).astype(bf16) for n in idx]
        bk = [jnp.concatenate([stack("bt", n), stack("kt", n)], axis=0).astype(bf16) for n in idx]
        s_pack = [[s_ref[q, b, j] for b in seqs] for q, j in units]
        scores = [_nt_dot(ar[n], bk[n]) for n in idx]
        yield
        from_a, from_r = [], []
        for n in idx:
            per_seq = [
                _nt_dot(jnp.concatenate(seq_rows(at2[n], b) + seq_rows(rt2[n], b),
                                        axis=0).astype(bf16),
                        _stack_heads(s_pack[n][b]).astype(bf16)) for b in seqs]
            pieces = lambda part: jnp.concatenate(
                [per_seq[b][(part * HEADS_PER_GROUP + h) * c_tok:
                            (part * HEADS_PER_GROUP + h + 1) * c_tok]
                 for h in range(HEADS_PER_GROUP) for b in seqs], axis=0)
            from_a.append(pieces(0))
            from_r.append(pieces(1))
        yield
        l_ab = [jnp.where(strict, scores[n][:c2, :c2], 0.0) for n in idx]
        l_b = [l_ab[n].astype(bf16) for n in idx]
        power = [mm(l_b[n], l_b[n]) for n in idx]
        yield
        w2 = [from_a[n]
              + mm(jnp.where(strict, scores[n][:c2, c2:], 0.0).astype(bf16), v2[n]) for n in idx]
        yield
        inv = [eye + l_ab[n] for n in idx]
        span = 2
        while span < c_tok:
            if 2 * span < c_tok:
                both = [mm(jnp.concatenate([inv[n], power[n]], axis=0).astype(bf16),
                           power[n].astype(bf16)) for n in idx]
                inv = [inv[n] + both[n][:c2] for n in idx]
                power = [both[n][c2:] for n in idx]
            else:
                inv = [inv[n] + mm(inv[n].astype(bf16), power[n].astype(bf16)) for n in idx]
            span *= 2
            yield
        u2f = [mm(inv[n].astype(bf16), w2[n].astype(bf16)) for n in idx]
        yield
        uv = [jnp.concatenate([u2f[n].astype(bf16), v2[n]], axis=0) for n in idx]
        y2 = [from_r[n]
              + mm(jnp.where(incl_wide, scores[n][c2:], 0.0).astype(bf16), uv[n]) for n in idx]
        for q in slabs:
            ops[q]["y"] = jnp.concatenate(
                [y2[n][:rows] + y2[n][rows:] for n in idx if units[n][0] == q], axis=1)
        yield
        for n, (q, j) in enumerate(units):
            bc2, kc2 = stack("bc", n), stack("kc", n)
            for b in seqs:
                lhs = jnp.concatenate(seq_rows(u2f[n], b) + seq_rows(v2f[n], b), axis=0)
                rhs = jnp.concatenate(seq_rows(bc2, b) + seq_rows(kc2, b), axis=0)
                s_new = _tn_dot(lhs.astype(bf16), rhs.astype(bf16))
                decay = ops[q]["p_total"][b * c_tok:b * c_tok + 1, j * LANES:(j + 1) * LANES]
                s_ref[q, b, j] = s_pack[n][b] * decay + s_new[:HEAD_DIM] + s_new[HEAD_DIM:]
        yield

    def norm_gate(q, ops):
        y = ops["y"]
        mean = _group_sum_wide(y, ones_blk, 3) * (1.0 / HEAD_DIM)
        yield
        ycen = y - mean
        var = _group_sum_wide(ycen * ycen, ones_blk, 3) * (1.0 / HEAD_DIM)
        yield
        yn = ycen * lax.rsqrt(var + GN_EPS) * lnw_ref[...] + lnb_ref[...]
        mix_ref[q, :, CONV_DIM:CONV_DIM + RWKV_DIM] = (yn + ops["bonus"]) * ops["g"]
        yield

    ops = [dict() for _ in range(n_slab)]
    waves = [range(w, min(w + CHAIN_SLABS, n_slab)) for w in range(0, n_slab, CHAIN_SLABS)]
    for q in waves[0]:
        _drain(per_token(q, ops[q]))
    for w, wave in enumerate(waves):
        sides = [per_token(q, ops[q]) for q in (waves[w + 1] if w + 1 < len(waves) else ())]
        sides += [norm_gate(q, ops[q]) for q in (waves[w - 1] if w > 0 else ())]
        _interleave(chains(wave), sides)
    for q in waves[-1]:
        _drain(norm_gate(q, ops[q]))

    @pl.when(t_idx == n_t - 1)
    def _():
        convout_ref[...] = cu_ref[...]
        wkvout_ref[...] = s_ref[...]


def _chunk_mixer(p, p_prev, conv0, wkv0, ones_blk, lp, *, c_tok):
    n_total, t_rows, _ = p.shape
    rows = SLAB_ROWS
    n_seq = rows // c_tok
    n_slab = min(LONG_SLABS if n_seq == 1 else SHORT_SLABS, n_total)
    assert c_tok & (c_tok - 1) == 0 and c_tok >= CONV_WIDTH - 1
    assert t_rows % rows == 0 and n_total % n_slab == 0 and (n_seq == 1 or t_rows == rows)
    r = jnp.arange(rows)
    same = (r[:, None] // c_tok) == (r[None, :] // c_tok)
    tri = jnp.concatenate([same & (r[None, :] <= r[:, None]), same], axis=0).astype(bf16)
    const = lambda shape: pl.BlockSpec(shape, lambda i, t: (0,) * len(shape))
    state = lambda shape: pl.BlockSpec((n_slab,) + shape, lambda i, t: (i,) + (0,) * len(shape))
    in_specs = [
        pl.BlockSpec((n_slab, rows, IN_PROJ), lambda i, t: (i, t, 0)),
        state((n_seq, RWKV_PROJ)),
        state((n_seq, CONV_WIDTH - 1, CONV_DIM)),
        state((n_seq, HEAD_GROUPS, HEAD_DIM, LANES)),
        const((LANES, LANES)),
        const((2 * rows, rows)),
        const((1, RWKV_PROJ)),
        const((CONV_WIDTH, CONV_DIM)),
        const((1, CONV_DIM)),
        const((1, RWKV_DIM)),
        const((LORA_DECAY + LORA_A, RWKV_DIM)),
        const((1, RWKV_DIM)),
        const((LORA_DECAY + LORA_A, RWKV_DIM)),
        const((LORA_GATE, RWKV_DIM)),
        const((1, RWKV_DIM)),
        const((1, RWKV_DIM)),
        const((1, RWKV_DIM)),
        const((1, RWKV_DIM)),
        const((1, RWKV_DIM)),
    ]
    out_specs = [
        pl.BlockSpec((n_slab, rows, D_MODEL), lambda i, t: (i, t, 0)),
        state((n_seq, CONV_WIDTH - 1, CONV_DIM)),
        state((n_seq, HEAD_GROUPS, HEAD_DIM, LANES)),
    ]
    out_shape = [
        jax.ShapeDtypeStruct((n_total, t_rows, D_MODEL), f32),
        jax.ShapeDtypeStruct(conv0.shape, f32),
        jax.ShapeDtypeStruct(wkv0.shape, f32),
    ]
    scratch = [
        pltpu.VMEM((n_slab, n_seq, HEAD_GROUPS, HEAD_DIM, LANES), f32),
        pltpu.VMEM((n_slab, n_seq, RWKV_PROJ), f32),
        pltpu.VMEM((n_slab, n_seq, CONV_WIDTH - 1, CONV_DIM), f32),
    ]
    return pl.pallas_call(
        functools.partial(_chunk_mixer_kernel, n_slab=n_slab, c_tok=c_tok),
        grid=(n_total // n_slab, t_rows // rows),
        in_specs=in_specs,
        out_specs=out_specs,
        out_shape=out_shape,
        scratch_shapes=scratch,
        compiler_params=pltpu.CompilerParams(
            dimension_semantics=("arbitrary", "arbitrary"),
            vmem_limit_bytes=VMEM_LIMIT_BYTES),
        name="chunk_mixer",
    )(p, p_prev, conv0, wkv0, ones_blk, tri,
      lp["mu"], lp["conv_w"], lp["conv_norm"], lp["w0"], lp["wdu"], lp["a0"], lp["aup"],
      lp["gup"], lp["k_k"], lp["k_a"], lp["r_k"], lp["ln_w"], lp["ln_b"])


def _ffn_kernel(x_ref, mix_ref, wout_ref, g2_ref, wg_ref, wu_ref, wd_ref, gf_ref, o_ref, *,
                final):
    x = x_ref[...] + jnp.dot(mix_ref[...].astype(bf16), wout_ref[...],
                             preferred_element_type=f32)
    hb = _rms(x, g2_ref[...]).astype(bf16)
    gate = jnp.dot(hb, wg_ref[...], preferred_element_type=f32)
    up = jnp.dot(hb, wu_ref[...], preferred_element_type=f32)
    act = gate * _sigmoid(gate) * up
    out = x + jnp.dot(act.astype(bf16), wd_ref[...], preferred_element_type=f32)
    o_ref[...] = _rms(out, gf_ref[...]) if final else out


def _ffn(x, mix, lp, final_norm, *, final):
    n, d = x.shape
    tile = min(ROW_TILE, n)
    layer = lp["layer"]
    resident = lambda rows, cols: pl.BlockSpec((None, rows, cols), lambda i: (layer, 0, 0),
                                               pipeline_mode=pl.Buffered(1))
    return pl.pallas_call(
        functools.partial(_ffn_kernel, final=final),
        grid=(n // tile,),
        in_specs=[
            pl.BlockSpec((tile, d), lambda i: (i, 0)),
            pl.BlockSpec((tile, d), lambda i: (i, 0)),
            resident(d, d),
            pl.BlockSpec((1, d), lambda i: (0, 0)),
            resident(d, D_FF),
            resident(d, D_FF),
            resident(D_FF, d),
            pl.BlockSpec((1, d), lambda i: (0, 0)),
        ],
        out_specs=pl.BlockSpec((tile, d), lambda i: (i, 0)),
        out_shape=jax.ShapeDtypeStruct((n, d), f32),
        compiler_params=pltpu.CompilerParams(
            dimension_semantics=("arbitrary",), vmem_limit_bytes=VMEM_LIMIT_BYTES),
        name="outproj_ffn",
    )(x, mix, lp["w_out"], lp["norm2"], lp["w_gate"], lp["w_up"], lp["w_down"], final_norm)


def _pack_wkv(s):
    b = s.shape[0]
    s = s.reshape(b, HEAD_GROUPS, HEADS_PER_GROUP, HEAD_DIM, HEAD_DIM)
    return s.transpose(0, 1, 3, 2, 4).reshape(b, HEAD_GROUPS, HEAD_DIM, LANES)


def _unpack_wkv(s):
    b = s.shape[0]
    s = s.reshape(b, HEAD_GROUPS, HEAD_DIM, HEADS_PER_GROUP, HEAD_DIM)
    return s.transpose(0, 1, 3, 2, 4).reshape(b, RWKV_HEADS, HEAD_DIM, HEAD_DIM)


def _prepare_params(norm1, w_in, mu_shift, conv_w, conv_norm, w0, w_decay_up, a0, a_up, g_up,
                    k_k, k_a, r_k, ln_x_w, ln_x_b, w_out, norm2, w_gate, w_up, w_down,
                    final_norm):
    row = lambda a: a.reshape(1, -1)
    zeros_lora = jnp.zeros((LORA_DECAY, RWKV_DIM), f32)
    w_in_b = w_in.astype(bf16)
    stacked = dict(w_in=w_in_b, w_in_rwkv=w_in_b[:, :, 3 * CONV_DIM:],
                   w_out=w_out.astype(bf16), w_gate=w_gate.astype(bf16),
                   w_up=w_up.astype(bf16), w_down=w_down.astype(bf16))
    layers = []
    for l in range(DEPTH):
        layers.append(dict(
            stacked, layer=l, norm1=row(norm1[l]),
            mu=row(mu_shift[l]), conv_w=conv_w[l], conv_norm=row(conv_norm[l]),
            w0=row(w0[l]),
            wdu=jnp.concatenate([w_decay_up[l], zeros_lora], axis=0).astype(bf16),
            a0=row(a0[l]),
            aup=jnp.concatenate([zeros_lora, a_up[l]], axis=0).astype(bf16),
            gup=g_up[l].astype(bf16),
            k_k=row(k_k[l]), k_a=row(k_a[l]), r_k=row(r_k[l]),
            ln_w=row(ln_x_w[l]), ln_b=row(ln_x_b[l]), norm2=row(norm2[l])))
    lane = jnp.arange(LANES) // HEAD_DIM
    ones_blk = (lane[:, None] == lane[None, :]).astype(bf16)
    return layers, row(final_norm), ones_blk


def _trunk(x, shift0, conv0, wkv0, prepared):
    layers, final_norm, ones_blk = prepared
    bsz, t_len, d = x.shape
    c_tok = min(t_len, SLAB_ROWS)
    n_seq = SLAB_ROWS // c_tok
    slabs = bsz // n_seq
    assert t_len % c_tok == 0 and bsz % n_seq == 0 and (n_seq == 1 or t_len == c_tok)
    xf = x.reshape(bsz * t_len, d)
    shifts, convs, wkvs = [], [], []
    for l, lp in enumerate(layers):
        p = _inproj(xf, lp["norm1"], lp["w_in"], l, normalize=True)
        shifts.append(_rmsnorm(xf.reshape(bsz, t_len, d)[:, -1], lp["norm1"]))
        p_prev = _inproj(shift0[l], lp["norm1"], lp["w_in_rwkv"], l, normalize=False)
        mix, conv_new, wkv_new = _chunk_mixer(
            p.reshape(slabs, n_seq * t_len, IN_PROJ), p_prev.reshape(slabs, n_seq, RWKV_PROJ),
            conv0[l].reshape(slabs, n_seq, CONV_WIDTH - 1, CONV_DIM),
            _pack_wkv(wkv0[l]).reshape(slabs, n_seq, HEAD_GROUPS, HEAD_DIM, LANES),
            ones_blk, lp, c_tok=c_tok)
        xf = _ffn(xf, mix.reshape(bsz * t_len, d), lp, final_norm, final=(l == DEPTH - 1))
        convs.append(conv_new.reshape(bsz, CONV_WIDTH - 1, CONV_DIM))
        wkvs.append(_unpack_wkv(wkv_new.reshape(bsz, HEAD_GROUPS, HEAD_DIM, LANES)))
    return xf.reshape(bsz, t_len, d), jnp.stack(shifts), jnp.stack(convs), jnp.stack(wkvs)


def kernel(x_prompt, x_sample, state_shift, state_conv, state_wkv, norm1, w_in, mu_shift, conv_w, conv_norm, w0, w_decay_up, a0, a_up, g_up, k_k, k_a, r_k, ln_x_w, ln_x_b, w_out, norm2, w_gate, w_up, w_down, final_norm):
    prepared = _prepare_params(norm1, w_in, mu_shift, conv_w, conv_norm, w0, w_decay_up, a0,
                               a_up, g_up, k_k, k_a, r_k, ln_x_w, ln_x_b, w_out, norm2,
                               w_gate, w_up, w_down, final_norm)
    bp = x_prompt.shape[0]
    dt = x_prompt.dtype
    p_shift0 = jnp.zeros((DEPTH, bp, D_MODEL), dt)
    p_conv0 = jnp.zeros((DEPTH, bp, CONV_WIDTH - 1, CONV_DIM), dt)
    p_wkv0 = jnp.zeros((DEPTH, bp, RWKV_HEADS, HEAD_DIM, HEAD_DIM), state_wkv.dtype)
    y_p, sh_p, cv_p, wk_p = _trunk(x_prompt, p_shift0, p_conv0, p_wkv0, prepared)
    y_s, sh_s, cv_s, wk_s = _trunk(x_sample, state_shift, state_conv, state_wkv, prepared)
    return (y_p, y_s, sh_p, cv_p, wk_p, sh_s, cv_s, wk_s)
```

```python
import functools

import jax
import jax.numpy as jnp
from jax import lax
from jax.experimental import pallas as pl
from jax.experimental.pallas import tpu as pltpu

D_MODEL = 1024
DEPTH = 4
CONV_DIM = 512
CONV_WIDTH = 3
RWKV_DIM = 512
HEAD_DIM = 64
RWKV_HEADS = RWKV_DIM // HEAD_DIM
LORA_DECAY = 64
LORA_A = 64
LORA_GATE = 128
RWKV_PROJ = 3 * RWKV_DIM + LORA_DECAY + LORA_A + LORA_GATE
IN_PROJ = 3 * CONV_DIM + RWKV_PROJ
D_FF = 2816
RMS_EPS = 1e-6
GN_EPS = 64e-5

LANES = 128
SUBLANES = 8
HEADS_PER_GROUP = LANES // HEAD_DIM
HEAD_GROUPS = RWKV_HEADS // HEADS_PER_GROUP
VMEM_LIMIT_BYTES = 56 * 1024 * 1024
ROW_TILE = 512
SLAB_ROWS = 64
LONG_SLABS = 8
SHORT_SLABS = 4
CHAIN_SLABS = 2

f32 = jnp.float32
bf16 = jnp.bfloat16


def _split_bf16(x, terms):
    parts = []
    rem = x
    for i in range(terms):
        part = rem.astype(bf16)
        parts.append(part)
        if i + 1 < terms:
            rem = rem - part.astype(f32)
    return parts


def _group_sum_wide(x, ones_blk, terms):
    rows = x.shape[0]
    n_blk = x.shape[1] // LANES
    lhs = jnp.concatenate(
        [part[:, c * LANES:(c + 1) * LANES] for part in _split_bf16(x, terms)
         for c in range(n_blk)], axis=0)
    out = jnp.dot(lhs, ones_blk, preferred_element_type=f32)
    cols = []
    for c in range(n_blk):
        acc = out[c * rows:(c + 1) * rows]
        for t in range(1, terms):
            acc = acc + out[(t * n_blk + c) * rows:(t * n_blk + c + 1) * rows]
        cols.append(acc)
    return jnp.concatenate(cols, axis=1)


def _sigmoid(x):
    return 1.0 / (1.0 + jnp.exp(-x))


def _rms(x, g):
    return x * lax.rsqrt(jnp.mean(x * x, axis=-1, keepdims=True) + RMS_EPS) * g


def _inproj_kernel(x_ref, g_ref, w_ref, o_ref, *, normalize):
    x = x_ref[...]
    if normalize:
        x = _rms(x, g_ref[...])
    o_ref[...] = jnp.dot(x.astype(bf16), w_ref[...], preferred_element_type=f32)


def _inproj(x, g, w, layer, *, normalize):
    n, d = x.shape
    m = w.shape[2]
    tile = min(ROW_TILE, n)
    return pl.pallas_call(
        functools.partial(_inproj_kernel, normalize=normalize),
        grid=(n // tile,),
        in_specs=[
            pl.BlockSpec((tile, d), lambda i: (i, 0)),
            pl.BlockSpec((1, d), lambda i: (0, 0)),
            pl.BlockSpec((None, d, m), lambda i: (layer, 0, 0)),
        ],
        out_specs=pl.BlockSpec((tile, m), lambda i: (i, 0)),
        out_shape=jax.ShapeDtypeStruct((n, m), f32),
        compiler_params=pltpu.CompilerParams(
            dimension_semantics=("arbitrary",), vmem_limit_bytes=VMEM_LIMIT_BYTES),
        name="inproj" if normalize else "shift_proj",
    )(x, g, w)


def _rmsnorm_kernel(x_ref, g_ref, o_ref):
    o_ref[...] = _rms(x_ref[...], g_ref[...])


def _rmsnorm(x, g):
    n, d = x.shape
    tile = min(ROW_TILE, n)
    return pl.pallas_call(
        _rmsnorm_kernel,
        grid=(n // tile,),
        in_specs=[pl.BlockSpec((tile, d), lambda i: (i, 0)),
                  pl.BlockSpec((1, d), lambda i: (0, 0))],
        out_specs=pl.BlockSpec((tile, d), lambda i: (i, 0)),
        out_shape=jax.ShapeDtypeStruct((n, d), f32),
        compiler_params=pltpu.CompilerParams(dimension_semantics=("arbitrary",)),
        name="rmsnorm",
    )(x, g)


def _nt_dot(a, b):
    return lax.dot_general(a, b, (((1,), (1,)), ((), ())), preferred_element_type=f32)


def _tn_dot(a, b):
    return lax.dot_general(a, b, (((0,), (0,)), ((), ())), preferred_element_type=f32)


def _stack_heads(x):
    lane = lax.broadcasted_iota(jnp.int32, x.shape, 1)
    return jnp.concatenate(
        [jnp.where(lane < HEAD_DIM, x, 0.0), jnp.where(lane >= HEAD_DIM, x, 0.0)], axis=0)


def _drain(gen):
    for _ in gen:
        pass


def _interleave(main, sides):
    for _ in main:
        for side in sides:
            next(side, None)
    for side in sides:
        _drain(side)


def _chunk_mixer_kernel(
        p_ref, pprev_ref, conv0_ref, wkv0_ref, ones_ref, tri_ref,
        mu_ref, convw_ref, convn_ref, w0_ref, wdu_ref, a0_ref, aup_ref, gup_ref,
        kk_ref, ka_ref, rk_ref, lnw_ref, lnb_ref,
        mix_ref, convout_ref, wkvout_ref,
        s_ref, cpr_ref, cu_ref,
        *, n_slab, c_tok):
    t_idx = pl.program_id(1)
    n_t = pl.num_programs(1)
    rows = SLAB_ROWS
    n_seq = rows // c_tok
    c2 = 2 * rows

    def for_each_state(body):
        def run(i, carry):
            body(i // n_seq, i % n_seq)
            return carry
        lax.fori_loop(0, n_slab * n_seq, run, 0)

    @pl.when(t_idx == 0)
    def _():
        def pack(q, b):
            for j in range(HEAD_GROUPS):
                s_ref[q, b, j] = jnp.concatenate(
                    [wkv0_ref[q, b, HEADS_PER_GROUP * j + h] for h in range(HEADS_PER_GROUP)],
                    axis=1)
        for_each_state(pack)
        cpr_ref[...] = pprev_ref[...]
        cu_ref[...] = conv0_ref[...]

    ones_blk = ones_ref[...]
    t_pos = lax.broadcasted_iota(jnp.int32, (rows, 1), 0) % c_tok
    ri = lax.broadcasted_iota(jnp.int32, (c2, c2), 0) % rows
    ci = lax.broadcasted_iota(jnp.int32, (c2, c2), 1) % rows
    same_seq = (ri // c_tok) == (ci // c_tok)
    strict = same_seq & (ci % c_tok < ri % c_tok)
    incl = same_seq & (ci % c_tok <= ri % c_tok)
    incl_wide = jnp.concatenate([incl, incl], axis=1)
    eye = (lax.broadcasted_iota(jnp.int32, (c2, c2), 0)
           == lax.broadcasted_iota(jnp.int32, (c2, c2), 1)).astype(f32)
    groups = range(HEAD_GROUPS)
    seqs = range(n_seq)
    mm = functools.partial(jnp.dot, preferred_element_type=f32)

    def per_seq_rows(load_row):
        return jnp.concatenate(
            [jnp.broadcast_to(load_row(b), (c_tok, load_row(b).shape[1])) for b in seqs], axis=0)

    def shift_rows(x, fills):
        out = pltpu.roll(x, len(fills), axis=0)
        for i, fill in enumerate(fills):
            out = jnp.where(t_pos == i, fill, out)
        return out

    def last_rows(x, back):
        return [x[b * c_tok + c_tok - back:b * c_tok + c_tok - back + 1, :] for b in seqs]

    def per_token(q, ops):
        u = p_ref[q, :, 2 * CONV_DIM:3 * CONV_DIM] * p_ref[q, :, 0:CONV_DIM]
        um2 = per_seq_rows(lambda b: cu_ref[q, b, 0:1, :])
        um1 = per_seq_rows(lambda b: cu_ref[q, b, 1:2, :])
        conv = (shift_rows(u, [um2, um1]) * convw_ref[0:1, :]
                + shift_rows(u, [um1]) * convw_ref[1:2, :] + u * convw_ref[2:3, :])
        for b, (r2, r1) in enumerate(zip(last_rows(u, 2), last_rows(u, 1))):
            cu_ref[q, b, 0:1, :] = r2
            cu_ref[q, b, 1:2, :] = r1
        yc = p_ref[q, :, CONV_DIM:2 * CONV_DIM] * conv
        mix_ref[q, :, 0:CONV_DIM] = _rms(yc, convn_ref[...])

        pr = p_ref[q, :, 3 * CONV_DIM:IN_PROJ]
        shifted = shift_rows(pr, [per_seq_rows(lambda b: cpr_ref[q, b:b + 1, :])])
        for b, r1 in enumerate(last_rows(pr, 1)):
            cpr_ref[q, b:b + 1, :] = r1
        m = pr + (shifted - pr) * mu_ref[...]
        o1, o2, o3 = RWKV_DIM, 2 * RWKV_DIM, 3 * RWKV_DIM
        r = m[:, :o1]
        k = m[:, o1:o2]
        v = m[:, o2:o3]
        wa_d = m[:, o3:o3 + LORA_DECAY + LORA_A]
        g_d = m[:, o3 + LORA_DECAY + LORA_A:]
        z = w0_ref[...] + mm(jnp.tanh(wa_d).astype(bf16), wdu_ref[...])
        a_lin = a0_ref[...] + mm(wa_d.astype(bf16), aup_ref[...])
        ops["g"] = mm(_sigmoid(g_d).astype(bf16), gup_ref[...])
        yield
        nz = -z
        softplus = jnp.maximum(nz, 0.0) + jnp.log1p(jnp.exp(-jnp.abs(nz)))
        log_d = -jnp.exp(-softplus - 0.5)
        a = _sigmoid(a_lin)
        kk = k * kk_ref[...]
        kk = kk / jnp.maximum(jnp.sqrt(_group_sum_wide(kk * kk, ones_blk, 3)), 1e-12)
        k = k * (1.0 + (a - 1.0) * ka_ref[...])
        yield
        tri = tri_ref[...]
        sums = None
        for part in _split_bf16(log_d, 3):
            term = mm(tri, part)
            sums = term if sums is None else sums + term
        cum, total = sums[:rows], sums[rows:]
        yield
        ops["bonus"] = _group_sum_wide(r * k * rk_ref[...], ones_blk, 3) * v
        p_inv = jnp.exp(-cum)
        p_rest = jnp.exp(total - cum)
        beta = kk * a
        ops.update(at=-kk * jnp.exp(cum - log_d), rt=r * jnp.exp(cum), bt=beta * p_inv,
                   kt=k * p_inv, bc=beta * p_rest, kc=k * p_rest, v=v, p_total=jnp.exp(total))
        yield

    def seq_rows(x2, b):
        return [x2[h * rows + b * c_tok:h * rows + (b + 1) * c_tok] for h in range(HEADS_PER_GROUP)]

    def chains(slabs):
        units = [(q, j) for q in slabs for j in groups]
        idx = range(len(units))

        def stack(name, n):
            q, j = units[n]
            return _stack_heads(ops[q][name][:, j * LANES:(j + 1) * LANES])

        at2 = [stack("at", n) for n in idx]
        rt2 = [stack("rt", n) for n in idx]
        v2f = [stack("v", n) for n in idx]
        v2 = [x.astype(bf16) for x in v2f]
        ar = [jnp.concatenate([at2[n], rt2[n]], axis=0).astype(bf16) for n in idx]
        bk = [jnp.concatenate([stack("bt", n), stack("kt", n)], axis=0).astype(bf16) for n in idx]
        s_pack = [[s_ref[q, b, j] for b in seqs] for q, j in units]
        scores = [_nt_dot(ar[n], bk[n]) for n in idx]
        yield
        from_a, from_r = [], []
        for n in idx:
            per_seq = [
                _nt_dot(jnp.concatenate(seq_rows(at2[n], b) + seq_rows(rt2[n], b),
                                        axis=0).astype(bf16),
                        _stack_heads(s_pack[n][b]).astype(bf16)) for b in seqs]
            pieces = lambda part: jnp.concatenate(
                [per_seq[b][(part * HEADS_PER_GROUP + h) * c_tok:
                            (part * HEADS_PER_GROUP + h + 1) * c_tok]
                 for h in range(HEADS_PER_GROUP) for b in seqs], axis=0)
            from_a.append(pieces(0))
            from_r.append(pieces(1))
        yield
        l_ab = [jnp.where(strict, scores[n][:c2, :c2], 0.0) for n in idx]
        l_b = [l_ab[n].astype(bf16) for n in idx]
        power = [mm(l_b[n], l_b[n]) for n in idx]
        yield
        w2 = [from_a[n]
              + mm(jnp.where(strict, scores[n][:c2, c2:], 0.0).astype(bf16), v2[n]) for n in idx]
        yield
        inv = [eye + l_ab[n] for n in idx]
        span = 2
        while span < c_tok:
            if 2 * span < c_tok:
                both = [mm(jnp.concatenate([inv[n], power[n]], axis=0).astype(bf16),
                           power[n].astype(bf16)) for n in idx]
                inv = [inv[n] + both[n][:c2] for n in idx]
                power = [both[n][c2:] for n in idx]
            else:
                inv = [inv[n] + mm(inv[n].astype(bf16), power[n].astype(bf16)) for n in idx]
            span *= 2
            yield
        u2f = [mm(inv[n].astype(bf16), w2[n].astype(bf16)) for n in idx]
        yield
        uv = [jnp.concatenate([u2f[n].astype(bf16), v2[n]], axis=0) for n in idx]
        y2 = [from_r[n]
              + mm(jnp.where(incl_wide, scores[n][c2:], 0.0).astype(bf16), uv[n]) for n in idx]
        for q in slabs:
            ops[q]["y"] = jnp.concatenate(
                [y2[n][:rows] + y2[n][rows:] for n in idx if units[n][0] == q], axis=1)
        yield
        for n, (q, j) in enumerate(units):
            bc2, kc2 = stack("bc", n), stack("kc", n)
            for b in seqs:
                lhs = jnp.concatenate(seq_rows(u2f[n], b) + seq_rows(v2f[n], b), axis=0)
                rhs = jnp.concatenate(seq_rows(bc2, b) + seq_rows(kc2, b), axis=0)
                s_new = _tn_dot(lhs.astype(bf16), rhs.astype(bf16))
                decay = ops[q]["p_total"][b * c_tok:b * c_tok + 1, j * LANES:(j + 1) * LANES]
                s_ref[q, b, j] = s_pack[n][b] * decay + s_new[:HEAD_DIM] + s_new[HEAD_DIM:]
        yield

    def norm_gate(q, ops):
        y = ops["y"]
        mean = _group_sum_wide(y, ones_blk, 3) * (1.0 / HEAD_DIM)
        yield
        ycen = y - mean
        var = _group_sum_wide(ycen * ycen, ones_blk, 3) * (1.0 / HEAD_DIM)
        yield
        yn = ycen * lax.rsqrt(var + GN_EPS) * lnw_ref[...] + lnb_ref[...]
        mix_ref[q, :, CONV_DIM:CONV_DIM + RWKV_DIM] = (yn + ops["bonus"]) * ops["g"]
        yield

    ops = [dict() for _ in range(n_slab)]
    waves = [range(w, min(w + CHAIN_SLABS, n_slab)) for w in range(0, n_slab, CHAIN_SLABS)]
    for q in waves[0]:
        _drain(per_token(q, ops[q]))
    for w, wave in enumerate(waves):
        sides = [per_token(q, ops[q]) for q in (waves[w + 1] if w + 1 < len(waves) else ())]
        sides += [norm_gate(q, ops[q]) for q in (waves[w - 1] if w > 0 else ())]
        _interleave(chains(wave), sides)
    for q in waves[-1]:
        _drain(norm_gate(q, ops[q]))

    @pl.when(t_idx == n_t - 1)
    def _():
        convout_ref[...] = cu_ref[...]

        def unpack(q, b):
            for j in range(HEAD_GROUPS):
                s = s_ref[q, b, j]
                for h in range(HEADS_PER_GROUP):
                    wkvout_ref[q, b, HEADS_PER_GROUP * j + h] = s[:, h * HEAD_DIM:(h + 1) * HEAD_DIM]
        for_each_state(unpack)


def _chunk_mixer(p, p_prev, conv0, wkv0, ones_blk, lp, *, c_tok):
    n_total, t_rows, _ = p.shape
    rows = SLAB_ROWS
    n_seq = rows // c_tok
    n_slab = min(LONG_SLABS if n_seq == 1 else SHORT_SLABS, n_total)
    assert c_tok & (c_tok - 1) == 0 and c_tok >= CONV_WIDTH - 1
    assert t_rows % rows == 0 and n_total % n_slab == 0 and (n_seq == 1 or t_rows == rows)
    r = jnp.arange(rows)
    same = (r[:, None] // c_tok) == (r[None, :] // c_tok)
    tri = jnp.concatenate([same & (r[None, :] <= r[:, None]), same], axis=0).astype(bf16)
    const = lambda shape: pl.BlockSpec(shape, lambda i, t: (0,) * len(shape))
    state = lambda shape, **kw: pl.BlockSpec(
        (n_slab,) + shape, lambda i, t: (i,) + (0,) * len(shape), **kw)
    wkv_spec = state((n_seq, RWKV_HEADS, HEAD_DIM, HEAD_DIM), pipeline_mode=pl.Buffered(1))
    in_specs = [
        pl.BlockSpec((n_slab, rows, IN_PROJ), lambda i, t: (i, t, 0)),
        state((n_seq, RWKV_PROJ)),
        state((n_seq, CONV_WIDTH - 1, CONV_DIM)),
        wkv_spec,
        const((LANES, LANES)),
        const((2 * rows, rows)),
        const((1, RWKV_PROJ)),
        const((CONV_WIDTH, CONV_DIM)),
        const((1, CONV_DIM)),
        const((1, RWKV_DIM)),
        const((LORA_DECAY + LORA_A, RWKV_DIM)),
        const((1, RWKV_DIM)),
        const((LORA_DECAY + LORA_A, RWKV_DIM)),
        const((LORA_GATE, RWKV_DIM)),
        const((1, RWKV_DIM)),
        const((1, RWKV_DIM)),
        const((1, RWKV_DIM)),
        const((1, RWKV_DIM)),
        const((1, RWKV_DIM)),
    ]
    out_specs = [
        pl.BlockSpec((n_slab, rows, D_MODEL), lambda i, t: (i, t, 0)),
        state((n_seq, CONV_WIDTH - 1, CONV_DIM)),
        wkv_spec,
    ]
    out_shape = [
        jax.ShapeDtypeStruct((n_total, t_rows, D_MODEL), f32),
        jax.ShapeDtypeStruct(conv0.shape, f32),
        jax.ShapeDtypeStruct(wkv0.shape, f32),
    ]
    scratch = [
        pltpu.VMEM((n_slab, n_seq, HEAD_GROUPS, HEAD_DIM, LANES), f32),
        pltpu.VMEM((n_slab, n_seq, RWKV_PROJ), f32),
        pltpu.VMEM((n_slab, n_seq, CONV_WIDTH - 1, CONV_DIM), f32),
    ]
    return pl.pallas_call(
        functools.partial(_chunk_mixer_kernel, n_slab=n_slab, c_tok=c_tok),
        grid=(n_total // n_slab, t_rows // rows),
        in_specs=in_specs,
        out_specs=out_specs,
        out_shape=out_shape,
        scratch_shapes=scratch,
        compiler_params=pltpu.CompilerParams(
            dimension_semantics=("arbitrary", "arbitrary"),
            vmem_limit_bytes=VMEM_LIMIT_BYTES),
        name="chunk_mixer",
    )(p, p_prev, conv0, wkv0, ones_blk, tri,
      lp["mu"], lp["conv_w"], lp["conv_norm"], lp["w0"], lp["wdu"], lp["a0"], lp["aup"],
      lp["gup"], lp["k_k"], lp["k_a"], lp["r_k"], lp["ln_w"], lp["ln_b"])


def _ffn_kernel(x_ref, mix_ref, wout_ref, g2_ref, wg_ref, wu_ref, wd_ref, gf_ref, o_ref, *,
                final):
    x = x_ref[...] + jnp.dot(mix_ref[...].astype(bf16), wout_ref[...],
                             preferred_element_type=f32)
    hb = _rms(x, g2_ref[...]).astype(bf16)
    gate = jnp.dot(hb, wg_ref[...], preferred_element_type=f32)
    up = jnp.dot(hb, wu_ref[...], preferred_element_type=f32)
    act = gate * _sigmoid(gate) * up
    out = x + jnp.dot(act.astype(bf16), wd_ref[...], preferred_element_type=f32)
    o_ref[...] = _rms(out, gf_ref[...]) if final else out


def _ffn(x, mix, lp, final_norm, *, final):
    n, d = x.shape
    tile = min(ROW_TILE, n)
    layer = lp["layer"]
    resident = lambda rows, cols: pl.BlockSpec((None, rows, cols), lambda i: (layer, 0, 0),
                                               pipeline_mode=pl.Buffered(1))
    return pl.pallas_call(
        functools.partial(_ffn_kernel, final=final),
        grid=(n // tile,),
        in_specs=[
            pl.BlockSpec((tile, d), lambda i: (i, 0)),
            pl.BlockSpec((tile, d), lambda i: (i, 0)),
            resident(d, d),
            pl.BlockSpec((1, d), lambda i: (0, 0)),
            resident(d, D_FF),
            resident(d, D_FF),
            resident(D_FF, d),
            pl.BlockSpec((1, d), lambda i: (0, 0)),
        ],
        out_specs=pl.BlockSpec((tile, d), lambda i: (i, 0)),
        out_shape=jax.ShapeDtypeStruct((n, d), f32),
        compiler_params=pltpu.CompilerParams(
            dimension_semantics=("arbitrary",), vmem_limit_bytes=VMEM_LIMIT_BYTES),
        name="outproj_ffn",
    )(x, mix, lp["w_out"], lp["norm2"], lp["w_gate"], lp["w_up"], lp["w_down"], final_norm)


def _prepare_params(norm1, w_in, mu_shift, conv_w, conv_norm, w0, w_decay_up, a0, a_up, g_up,
                    k_k, k_a, r_k, ln_x_w, ln_x_b, w_out, norm2, w_gate, w_up, w_down,
                    final_norm):
    row = lambda a: a.reshape(1, -1)
    zeros_lora = jnp.zeros((LORA_DECAY, RWKV_DIM), f32)
    w_in_b = w_in.astype(bf16)
    stacked = dict(w_in=w_in_b, w_in_rwkv=w_in_b[:, :, 3 * CONV_DIM:],
                   w_out=w_out.astype(bf16), w_gate=w_gate.astype(bf16),
                   w_up=w_up.astype(bf16), w_down=w_down.astype(bf16))
    layers = []
    for l in range(DEPTH):
        layers.append(dict(
            stacked, layer=l, norm1=row(norm1[l]),
            mu=row(mu_shift[l]), conv_w=conv_w[l], conv_norm=row(conv_norm[l]),
            w0=row(w0[l]),
            wdu=jnp.concatenate([w_decay_up[l], zeros_lora], axis=0).astype(bf16),
            a0=row(a0[l]),
            aup=jnp.concatenate([zeros_lora, a_up[l]], axis=0).astype(bf16),
            gup=g_up[l].astype(bf16),
            k_k=row(k_k[l]), k_a=row(k_a[l]), r_k=row(r_k[l]),
            ln_w=row(ln_x_w[l]), ln_b=row(ln_x_b[l]), norm2=row(norm2[l])))
    lane = jnp.arange(LANES) // HEAD_DIM
    ones_blk = (lane[:, None] == lane[None, :]).astype(bf16)
    return layers, row(final_norm), ones_blk


def _trunk(x, shift0, conv0, wkv0, prepared):
    layers, final_norm, ones_blk = prepared
    bsz, t_len, d = x.shape
    c_tok = min(t_len, SLAB_ROWS)
    n_seq = SLAB_ROWS // c_tok
    slabs = bsz // n_seq
    assert t_len % c_tok == 0 and bsz % n_seq == 0 and (n_seq == 1 or t_len == c_tok)
    xf = x.reshape(bsz * t_len, d)
    shifts, convs, wkvs = [], [], []
    for l, lp in enumerate(layers):
        p = _inproj(xf, lp["norm1"], lp["w_in"], l, normalize=True)
        shifts.append(_rmsnorm(xf.reshape(bsz, t_len, d)[:, -1], lp["norm1"]))
        p_prev = _inproj(shift0[l], lp["norm1"], lp["w_in_rwkv"], l, normalize=False)
        mix, conv_new, wkv_new = _chunk_mixer(
            p.reshape(slabs, n_seq * t_len, IN_PROJ), p_prev.reshape(slabs, n_seq, RWKV_PROJ),
            conv0[l].reshape(slabs, n_seq, CONV_WIDTH - 1, CONV_DIM),
            wkv0[l].reshape(slabs, n_seq, RWKV_HEADS, HEAD_DIM, HEAD_DIM), ones_blk, lp,
            c_tok=c_tok)
        xf = _ffn(xf, mix.reshape(bsz * t_len, d), lp, final_norm, final=(l == DEPTH - 1))
        convs.append(conv_new.reshape(bsz, CONV_WIDTH - 1, CONV_DIM))
        wkvs.append(wkv_new.reshape(bsz, RWKV_HEADS, HEAD_DIM, HEAD_DIM))
    return xf.reshape(bsz, t_len, d), jnp.stack(shifts), jnp.stack(convs), jnp.stack(wkvs)


def kernel(x_prompt, x_sample, state_shift, state_conv, state_wkv, norm1, w_in, mu_shift, conv_w, conv_norm, w0, w_decay_up, a0, a_up, g_up, k_k, k_a, r_k, ln_x_w, ln_x_b, w_out, norm2, w_gate, w_up, w_down, final_norm):
    prepared = _prepare_params(norm1, w_in, mu_shift, conv_w, conv_norm, w0, w_decay_up, a0,
                               a_up, g_up, k_k, k_a, r_k, ln_x_w, ln_x_b, w_out, norm2,
                               w_gate, w_up, w_down, final_norm)
    bp = x_prompt.shape[0]
    dt = x_prompt.dtype
    p_shift0 = jnp.zeros((DEPTH, bp, D_MODEL), dt)
    p_conv0 = jnp.zeros((DEPTH, bp, CONV_WIDTH - 1, CONV_DIM), dt)
    p_wkv0 = jnp.zeros((DEPTH, bp, RWKV_HEADS, HEAD_DIM, HEAD_DIM), state_wkv.dtype)
    y_p, sh_p, cv_p, wk_p = _trunk(x_prompt, p_shift0, p_conv0, p_wkv0, prepared)
    y_s, sh_s, cv_s, wk_s = _trunk(x_sample, state_shift, state_conv, state_wkv, prepared)
    return (y_p, y_s, sh_p, cv_p, wk_p, sh_s, cv_s, wk_s)
```

```python
import functools

import jax
import jax.numpy as jnp
from jax import lax
from jax.experimental import pallas as pl
from jax.experimental.pallas import tpu as pltpu

D_MODEL = 1024
DEPTH = 4
CONV_DIM = 512
CONV_WIDTH = 3
RWKV_DIM = 512
HEAD_DIM = 64
RWKV_HEADS = RWKV_DIM // HEAD_DIM
LORA_DECAY = 64
LORA_A = 64
LORA_GATE = 128
RWKV_PROJ = 3 * RWKV_DIM + LORA_DECAY + LORA_A + LORA_GATE
IN_PROJ = 3 * CONV_DIM + RWKV_PROJ
D_FF = 2816
RMS_EPS = 1e-6
GN_EPS = 64e-5

LANES = 128
SUBLANES = 8
HEADS_PER_GROUP = LANES // HEAD_DIM
HEAD_GROUPS = RWKV_HEADS // HEADS_PER_GROUP
VMEM_LIMIT_BYTES = 56 * 1024 * 1024
ROW_TILE = 512
SLAB_ROWS = 64
LONG_SLABS = 8
SHORT_SLABS = 4
CHAIN_SLABS = 2

f32 = jnp.float32
bf16 = jnp.bfloat16


def _split_bf16(x, terms):
    parts = []
    rem = x
    for i in range(terms):
        part = rem.astype(bf16)
        parts.append(part)
        if i + 1 < terms:
            rem = rem - part.astype(f32)
    return parts


def _group_sum_wide(x, ones_blk, terms):
    rows = x.shape[0]
    n_blk = x.shape[1] // LANES
    lhs = jnp.concatenate(
        [part[:, c * LANES:(c + 1) * LANES] for part in _split_bf16(x, terms)
         for c in range(n_blk)], axis=0)
    out = jnp.dot(lhs, ones_blk, preferred_element_type=f32)
    cols = []
    for c in range(n_blk):
        acc = out[c * rows:(c + 1) * rows]
        for t in range(1, terms):
            acc = acc + out[(t * n_blk + c) * rows:(t * n_blk + c + 1) * rows]
        cols.append(acc)
    return jnp.concatenate(cols, axis=1)


def _sigmoid(x):
    return 1.0 / (1.0 + jnp.exp(-x))


def _rms(x, g):
    return x * lax.rsqrt(jnp.mean(x * x, axis=-1, keepdims=True) + RMS_EPS) * g


def _inproj_kernel(x_ref, g_ref, w_ref, o_ref, *, normalize):
    x = x_ref[...]
    if normalize:
        x = _rms(x, g_ref[...])
    o_ref[...] = jnp.dot(x.astype(bf16), w_ref[...], preferred_element_type=f32)


def _inproj(x, g, w, layer, *, normalize):
    n, d = x.shape
    m = w.shape[2]
    tile = min(ROW_TILE, n)
    return pl.pallas_call(
        functools.partial(_inproj_kernel, normalize=normalize),
        grid=(n // tile,),
        in_specs=[
            pl.BlockSpec((tile, d), lambda i: (i, 0)),
            pl.BlockSpec((1, d), lambda i: (0, 0)),
            pl.BlockSpec((None, d, m), lambda i: (layer, 0, 0)),
        ],
        out_specs=pl.BlockSpec((tile, m), lambda i: (i, 0)),
        out_shape=jax.ShapeDtypeStruct((n, m), f32),
        compiler_params=pltpu.CompilerParams(
            dimension_semantics=("arbitrary",), vmem_limit_bytes=VMEM_LIMIT_BYTES),
        name="inproj" if normalize else "shift_proj",
    )(x, g, w)


def _rmsnorm_kernel(x_ref, g_ref, o_ref):
    o_ref[...] = _rms(x_ref[...], g_ref[...])


def _rmsnorm(x, g):
    n, d = x.shape
    tile = min(ROW_TILE, n)
    return pl.pallas_call(
        _rmsnorm_kernel,
        grid=(n // tile,),
        in_specs=[pl.BlockSpec((tile, d), lambda i: (i, 0)),
                  pl.BlockSpec((1, d), lambda i: (0, 0))],
        out_specs=pl.BlockSpec((tile, d), lambda i: (i, 0)),
        out_shape=jax.ShapeDtypeStruct((n, d), f32),
        compiler_params=pltpu.CompilerParams(dimension_semantics=("arbitrary",)),
        name="rmsnorm",
    )(x, g)


def _nt_dot(a, b):
    return lax.dot_general(a, b, (((1,), (1,)), ((), ())), preferred_element_type=f32)


def _tn_dot(a, b):
    return lax.dot_general(a, b, (((0,), (0,)), ((), ())), preferred_element_type=f32)


def _stack_heads(x):
    lane = lax.broadcasted_iota(jnp.int32, x.shape, 1)
    return jnp.concatenate(
        [jnp.where(lane < HEAD_DIM, x, 0.0), jnp.where(lane >= HEAD_DIM, x, 0.0)], axis=0)


def _drain(gen):
    for _ in gen:
        pass


def _interleave(main, sides):
    for _ in main:
        for side in sides:
            next(side, None)
    for side in sides:
        _drain(side)


def _chunk_mixer_kernel(
        p_ref, pprev_ref, conv0_ref, wkv0_ref, ones_ref, tri_ref,
        mu_ref, convw_ref, convn_ref, w0_ref, wdu_ref, a0_ref, aup_ref, gup_ref,
        kk_ref, ka_ref, rk_ref, lnw_ref, lnb_ref,
        mix_ref, convout_ref, wkvout_ref,
        s_ref, cpr_ref, cu_ref,
        *, n_slab, c_tok, single_step):
    t_idx = pl.program_id(1)
    n_t = pl.num_programs(1)
    rows = SLAB_ROWS
    n_seq = rows // c_tok
    c2 = 2 * rows

    def load_packed(ref, q, b, j):
        return jnp.concatenate(
            [ref[q, b, HEADS_PER_GROUP * j + h] for h in range(HEADS_PER_GROUP)], axis=1)

    def store_unpacked(ref, q, b, j, s):
        for h in range(HEADS_PER_GROUP):
            ref[q, b, HEADS_PER_GROUP * j + h] = s[:, h * HEAD_DIM:(h + 1) * HEAD_DIM]

    def load_state(q, b, j):
        return load_packed(wkv0_ref, q, b, j) if single_step else s_ref[q, b, j]

    def store_state(q, b, j, s):
        if single_step:
            store_unpacked(wkvout_ref, q, b, j, s)
        else:
            s_ref[q, b, j] = s

    def for_each_state(body):
        def run(i, carry):
            for j in range(HEAD_GROUPS):
                body(i // n_seq, i % n_seq, j)
            return carry
        lax.fori_loop(0, n_slab * n_seq, run, 0)

    @pl.when(t_idx == 0)
    def _():
        if not single_step:
            def pack(q, b, j):
                s_ref[q, b, j] = load_packed(wkv0_ref, q, b, j)
            for_each_state(pack)
        cpr_ref[...] = pprev_ref[...]
        cu_ref[...] = conv0_ref[...]

    ones_blk = ones_ref[...]
    t_pos = lax.broadcasted_iota(jnp.int32, (rows, 1), 0) % c_tok
    ri = lax.broadcasted_iota(jnp.int32, (c2, c2), 0) % rows
    ci = lax.broadcasted_iota(jnp.int32, (c2, c2), 1) % rows
    same_seq = (ri // c_tok) == (ci // c_tok)
    strict = same_seq & (ci % c_tok < ri % c_tok)
    incl = same_seq & (ci % c_tok <= ri % c_tok)
    incl_wide = jnp.concatenate([incl, incl], axis=1)
    eye = (lax.broadcasted_iota(jnp.int32, (c2, c2), 0)
           == lax.broadcasted_iota(jnp.int32, (c2, c2), 1)).astype(f32)
    groups = range(HEAD_GROUPS)
    seqs = range(n_seq)
    mm = functools.partial(jnp.dot, preferred_element_type=f32)

    def per_seq_rows(load_row):
        return jnp.concatenate(
            [jnp.broadcast_to(load_row(b), (c_tok, load_row(b).shape[1])) for b in seqs], axis=0)

    def shift_rows(x, fills):
        out = pltpu.roll(x, len(fills), axis=0)
        for i, fill in enumerate(fills):
            out = jnp.where(t_pos == i, fill, out)
        return out

    def last_rows(x, back):
        return [x[b * c_tok + c_tok - back:b * c_tok + c_tok - back + 1, :] for b in seqs]

    def per_token(q, ops):
        u = p_ref[q, :, 2 * CONV_DIM:3 * CONV_DIM] * p_ref[q, :, 0:CONV_DIM]
        um2 = per_seq_rows(lambda b: cu_ref[q, b, 0:1, :])
        um1 = per_seq_rows(lambda b: cu_ref[q, b, 1:2, :])
        conv = (shift_rows(u, [um2, um1]) * convw_ref[0:1, :]
                + shift_rows(u, [um1]) * convw_ref[1:2, :] + u * convw_ref[2:3, :])
        for b, (r2, r1) in enumerate(zip(last_rows(u, 2), last_rows(u, 1))):
            cu_ref[q, b, 0:1, :] = r2
            cu_ref[q, b, 1:2, :] = r1
        yc = p_ref[q, :, CONV_DIM:2 * CONV_DIM] * conv
        mix_ref[q, :, 0:CONV_DIM] = _rms(yc, convn_ref[...])

        pr = p_ref[q, :, 3 * CONV_DIM:IN_PROJ]
        shifted = shift_rows(pr, [per_seq_rows(lambda b: cpr_ref[q, b:b + 1, :])])
        for b, r1 in enumerate(last_rows(pr, 1)):
            cpr_ref[q, b:b + 1, :] = r1
        m = pr + (shifted - pr) * mu_ref[...]
        o1, o2, o3 = RWKV_DIM, 2 * RWKV_DIM, 3 * RWKV_DIM
        r = m[:, :o1]
        k = m[:, o1:o2]
        v = m[:, o2:o3]
        wa_d = m[:, o3:o3 + LORA_DECAY + LORA_A]
        g_d = m[:, o3 + LORA_DECAY + LORA_A:]
        z = w0_ref[...] + mm(jnp.tanh(wa_d).astype(bf16), wdu_ref[...])
        a_lin = a0_ref[...] + mm(wa_d.astype(bf16), aup_ref[...])
        ops["g"] = mm(_sigmoid(g_d).astype(bf16), gup_ref[...])
        yield
        nz = -z
        softplus = jnp.maximum(nz, 0.0) + jnp.log1p(jnp.exp(-jnp.abs(nz)))
        log_d = -jnp.exp(-softplus - 0.5)
        a = _sigmoid(a_lin)
        kk = k * kk_ref[...]
        kk = kk / jnp.maximum(jnp.sqrt(_group_sum_wide(kk * kk, ones_blk, 3)), 1e-12)
        k = k * (1.0 + (a - 1.0) * ka_ref[...])
        yield
        tri = tri_ref[...]
        sums = None
        for part in _split_bf16(log_d, 3):
            term = mm(tri, part)
            sums = term if sums is None else sums + term
        cum, total = sums[:rows], sums[rows:]
        yield
        ops["bonus"] = _group_sum_wide(r * k * rk_ref[...], ones_blk, 3) * v
        p_inv = jnp.exp(-cum)
        p_rest = jnp.exp(total - cum)
        beta = kk * a
        ops.update(at=-kk * jnp.exp(cum - log_d), rt=r * jnp.exp(cum), bt=beta * p_inv,
                   kt=k * p_inv, bc=beta * p_rest, kc=k * p_rest, v=v, p_total=jnp.exp(total))
        yield

    def seq_rows(x2, b):
        return [x2[h * rows + b * c_tok:h * rows + (b + 1) * c_tok] for h in range(HEADS_PER_GROUP)]

    def chains(slabs):
        units = [(q, j) for q in slabs for j in groups]
        idx = range(len(units))

        def stack(name, n):
            q, j = units[n]
            return _stack_heads(ops[q][name][:, j * LANES:(j + 1) * LANES])

        at2 = [stack("at", n) for n in idx]
        rt2 = [stack("rt", n) for n in idx]
        v2f = [stack("v", n) for n in idx]
        v2 = [x.astype(bf16) for x in v2f]
        ar = [jnp.concatenate([at2[n], rt2[n]], axis=0).astype(bf16) for n in idx]
        bk = [jnp.concatenate([stack("bt", n), stack("kt", n)], axis=0).astype(bf16) for n in idx]
        s_pack = [[load_state(q, b, j) for b in seqs] for q, j in units]
        scores = [_nt_dot(ar[n], bk[n]) for n in idx]
        yield
        from_a, from_r = [], []
        for n in idx:
            per_seq = [
                _nt_dot(jnp.concatenate(seq_rows(at2[n], b) + seq_rows(rt2[n], b),
                                        axis=0).astype(bf16),
                        _stack_heads(s_pack[n][b]).astype(bf16)) for b in seqs]
            pieces = lambda part: jnp.concatenate(
                [per_seq[b][(part * HEADS_PER_GROUP + h) * c_tok:
                            (part * HEADS_PER_GROUP + h + 1) * c_tok]
                 for h in range(HEADS_PER_GROUP) for b in seqs], axis=0)
            from_a.append(pieces(0))
            from_r.append(pieces(1))
        yield
        l_ab = [jnp.where(strict, scores[n][:c2, :c2], 0.0) for n in idx]
        l_b = [l_ab[n].astype(bf16) for n in idx]
        power = [mm(l_b[n], l_b[n]) for n in idx]
        yield
        w2 = [from_a[n]
              + mm(jnp.where(strict, scores[n][:c2, c2:], 0.0).astype(bf16), v2[n]) for n in idx]
        yield
        inv = [eye + l_ab[n] for n in idx]
        span = 2
        while span < c_tok:
            if 2 * span < c_tok:
                both = [mm(jnp.concatenate([inv[n], power[n]], axis=0).astype(bf16),
                           power[n].astype(bf16)) for n in idx]
                inv = [inv[n] + both[n][:c2] for n in idx]
                power = [both[n][c2:] for n in idx]
            else:
                inv = [inv[n] + mm(inv[n].astype(bf16), power[n].astype(bf16)) for n in idx]
            span *= 2
            yield
        u2f = [mm(inv[n].astype(bf16), w2[n].astype(bf16)) for n in idx]
        yield
        uv = [jnp.concatenate([u2f[n].astype(bf16), v2[n]], axis=0) for n in idx]
        y2 = [from_r[n]
              + mm(jnp.where(incl_wide, scores[n][c2:], 0.0).astype(bf16), uv[n]) for n in idx]
        for q in slabs:
            ops[q]["y"] = jnp.concatenate(
                [y2[n][:rows] + y2[n][rows:] for n in idx if units[n][0] == q], axis=1)
        yield
        for n, (q, j) in enumerate(units):
            bc2, kc2 = stack("bc", n), stack("kc", n)
            for b in seqs:
                lhs = jnp.concatenate(seq_rows(u2f[n], b) + seq_rows(v2f[n], b), axis=0)
                rhs = jnp.concatenate(seq_rows(bc2, b) + seq_rows(kc2, b), axis=0)
                s_new = _tn_dot(lhs.astype(bf16), rhs.astype(bf16))
                decay = ops[q]["p_total"][b * c_tok:b * c_tok + 1, j * LANES:(j + 1) * LANES]
                store_state(q, b, j,
                            s_pack[n][b] * decay + s_new[:HEAD_DIM] + s_new[HEAD_DIM:])
        yield

    def norm_gate(q, ops):
        y = ops["y"]
        mean = _group_sum_wide(y, ones_blk, 3) * (1.0 / HEAD_DIM)
        yield
        ycen = y - mean
        var = _group_sum_wide(ycen * ycen, ones_blk, 3) * (1.0 / HEAD_DIM)
        yield
        yn = ycen * lax.rsqrt(var + GN_EPS) * lnw_ref[...] + lnb_ref[...]
        mix_ref[q, :, CONV_DIM:CONV_DIM + RWKV_DIM] = (yn + ops["bonus"]) * ops["g"]
        yield

    ops = [dict() for _ in range(n_slab)]
    waves = [range(w, min(w + CHAIN_SLABS, n_slab)) for w in range(0, n_slab, CHAIN_SLABS)]
    for q in waves[0]:
        _drain(per_token(q, ops[q]))
    for w, wave in enumerate(waves):
        sides = [per_token(q, ops[q]) for q in (waves[w + 1] if w + 1 < len(waves) else ())]
        sides += [norm_gate(q, ops[q]) for q in (waves[w - 1] if w > 0 else ())]
        _interleave(chains(wave), sides)
    for q in waves[-1]:
        _drain(norm_gate(q, ops[q]))

    @pl.when(t_idx == n_t - 1)
    def _():
        convout_ref[...] = cu_ref[...]
        if not single_step:
            def unpack(q, b, j):
                store_unpacked(wkvout_ref, q, b, j, s_ref[q, b, j])
            for_each_state(unpack)


def _chunk_mixer(p, p_prev, conv0, wkv0, ones_blk, lp, *, c_tok):
    n_total, t_rows, _ = p.shape
    rows = SLAB_ROWS
    n_seq = rows // c_tok
    n_slab = min(LONG_SLABS if n_seq == 1 else SHORT_SLABS, n_total)
    assert c_tok & (c_tok - 1) == 0 and c_tok >= CONV_WIDTH - 1
    assert t_rows % rows == 0 and n_total % n_slab == 0 and (n_seq == 1 or t_rows == rows)
    r = jnp.arange(rows)
    same = (r[:, None] // c_tok) == (r[None, :] // c_tok)
    tri = jnp.concatenate([same & (r[None, :] <= r[:, None]), same], axis=0).astype(bf16)
    const = lambda shape: pl.BlockSpec(shape, lambda i, t: (0,) * len(shape))
    state = lambda shape, **kw: pl.BlockSpec(
        (n_slab,) + shape, lambda i, t: (i,) + (0,) * len(shape), **kw)
    wkv_block = (n_slab, n_seq, RWKV_HEADS, HEAD_DIM, HEAD_DIM)
    layer = lp["layer"]
    wkv_in = pl.BlockSpec((None,) + wkv_block, lambda i, t: (layer, i, 0, 0, 0, 0),
                          pipeline_mode=pl.Buffered(1))
    wkv_out = pl.BlockSpec(wkv_block, lambda i, t: (i, 0, 0, 0, 0), pipeline_mode=pl.Buffered(1))
    in_specs = [
        pl.BlockSpec((n_slab, rows, IN_PROJ), lambda i, t: (i, t, 0)),
        state((n_seq, RWKV_PROJ)),
        state((n_seq, CONV_WIDTH - 1, CONV_DIM)),
        wkv_in,
        const((LANES, LANES)),
        const((2 * rows, rows)),
        const((1, RWKV_PROJ)),
        const((CONV_WIDTH, CONV_DIM)),
        const((1, CONV_DIM)),
        const((1, RWKV_DIM)),
        const((LORA_DECAY + LORA_A, RWKV_DIM)),
        const((1, RWKV_DIM)),
        const((LORA_DECAY + LORA_A, RWKV_DIM)),
        const((LORA_GATE, RWKV_DIM)),
        const((1, RWKV_DIM)),
        const((1, RWKV_DIM)),
        const((1, RWKV_DIM)),
        const((1, RWKV_DIM)),
        const((1, RWKV_DIM)),
    ]
    out_specs = [
        pl.BlockSpec((n_slab, rows, D_MODEL), lambda i, t: (i, t, 0)),
        state((n_seq, CONV_WIDTH - 1, CONV_DIM)),
        wkv_out,
    ]
    out_shape = [
        jax.ShapeDtypeStruct((n_total, t_rows, D_MODEL), f32),
        jax.ShapeDtypeStruct(conv0.shape, f32),
        jax.ShapeDtypeStruct(wkv0.shape[1:], f32),
    ]
    scratch = [
        pltpu.VMEM((n_slab, n_seq, HEAD_GROUPS, HEAD_DIM, LANES), f32),
        pltpu.VMEM((n_slab, n_seq, RWKV_PROJ), f32),
        pltpu.VMEM((n_slab, n_seq, CONV_WIDTH - 1, CONV_DIM), f32),
    ]
    return pl.pallas_call(
        functools.partial(_chunk_mixer_kernel, n_slab=n_slab, c_tok=c_tok,
                          single_step=(t_rows == rows)),
        grid=(n_total // n_slab, t_rows // rows),
        in_specs=in_specs,
        out_specs=out_specs,
        out_shape=out_shape,
        scratch_shapes=scratch,
        compiler_params=pltpu.CompilerParams(
            dimension_semantics=("arbitrary", "arbitrary"),
            vmem_limit_bytes=VMEM_LIMIT_BYTES),
        name="chunk_mixer",
    )(p, p_prev, conv0, wkv0, ones_blk, tri,
      lp["mu"], lp["conv_w"], lp["conv_norm"], lp["w0"], lp["wdu"], lp["a0"], lp["aup"],
      lp["gup"], lp["k_k"], lp["k_a"], lp["r_k"], lp["ln_w"], lp["ln_b"])


def _ffn_kernel(x_ref, mix_ref, wout_ref, g2_ref, wg_ref, wu_ref, wd_ref, gf_ref, o_ref, *,
                final):
    x = x_ref[...] + jnp.dot(mix_ref[...].astype(bf16), wout_ref[...],
                             preferred_element_type=f32)
    hb = _rms(x, g2_ref[...]).astype(bf16)
    gate = jnp.dot(hb, wg_ref[...], preferred_element_type=f32)
    up = jnp.dot(hb, wu_ref[...], preferred_element_type=f32)
    act = gate * _sigmoid(gate) * up
    out = x + jnp.dot(act.astype(bf16), wd_ref[...], preferred_element_type=f32)
    o_ref[...] = _rms(out, gf_ref[...]) if final else out


def _ffn(x, mix, lp, final_norm, *, final):
    n, d = x.shape
    tile = min(ROW_TILE, n)
    layer = lp["layer"]
    resident = lambda rows, cols: pl.BlockSpec((None, rows, cols), lambda i: (layer, 0, 0),
                                               pipeline_mode=pl.Buffered(1))
    return pl.pallas_call(
        functools.partial(_ffn_kernel, final=final),
        grid=(n // tile,),
        in_specs=[
            pl.BlockSpec((tile, d), lambda i: (i, 0)),
            pl.BlockSpec((tile, d), lambda i: (i, 0)),
            resident(d, d),
            pl.BlockSpec((1, d), lambda i: (0, 0)),
            resident(d, D_FF),
            resident(d, D_FF),
            resident(D_FF, d),
            pl.BlockSpec((1, d), lambda i: (0, 0)),
        ],
        out_specs=pl.BlockSpec((tile, d), lambda i: (i, 0)),
        out_shape=jax.ShapeDtypeStruct((n, d), f32),
        compiler_params=pltpu.CompilerParams(
            dimension_semantics=("arbitrary",), vmem_limit_bytes=VMEM_LIMIT_BYTES),
        name="outproj_ffn",
    )(x, mix, lp["w_out"], lp["norm2"], lp["w_gate"], lp["w_up"], lp["w_down"], final_norm)


def _prepare_params(norm1, w_in, mu_shift, conv_w, conv_norm, w0, w_decay_up, a0, a_up, g_up,
                    k_k, k_a, r_k, ln_x_w, ln_x_b, w_out, norm2, w_gate, w_up, w_down,
                    final_norm):
    row = lambda a: a.reshape(1, -1)
    zeros_lora = jnp.zeros((LORA_DECAY, RWKV_DIM), f32)
    w_in_b = w_in.astype(bf16)
    stacked = dict(w_in=w_in_b, w_in_rwkv=w_in_b[:, :, 3 * CONV_DIM:],
                   w_out=w_out.astype(bf16), w_gate=w_gate.astype(bf16),
                   w_up=w_up.astype(bf16), w_down=w_down.astype(bf16))
    layers = []
    for l in range(DEPTH):
        layers.append(dict(
            stacked, layer=l, norm1=row(norm1[l]),
            mu=row(mu_shift[l]), conv_w=conv_w[l], conv_norm=row(conv_norm[l]),
            w0=row(w0[l]),
            wdu=jnp.concatenate([w_decay_up[l], zeros_lora], axis=0).astype(bf16),
            a0=row(a0[l]),
            aup=jnp.concatenate([zeros_lora, a_up[l]], axis=0).astype(bf16),
            gup=g_up[l].astype(bf16),
            k_k=row(k_k[l]), k_a=row(k_a[l]), r_k=row(r_k[l]),
            ln_w=row(ln_x_w[l]), ln_b=row(ln_x_b[l]), norm2=row(norm2[l])))
    lane = jnp.arange(LANES) // HEAD_DIM
    ones_blk = (lane[:, None] == lane[None, :]).astype(bf16)
    return layers, row(final_norm), ones_blk


def _trunk(x, shift0, conv0, wkv0, prepared):
    layers, final_norm, ones_blk = prepared
    bsz, t_len, d = x.shape
    c_tok = min(t_len, SLAB_ROWS)
    n_seq = SLAB_ROWS // c_tok
    slabs = bsz // n_seq
    assert t_len % c_tok == 0 and bsz % n_seq == 0 and (n_seq == 1 or t_len == c_tok)
    xf = x.reshape(bsz * t_len, d)
    shifts, convs, wkvs = [], [], []
    for l, lp in enumerate(layers):
        p = _inproj(xf, lp["norm1"], lp["w_in"], l, normalize=True)
        shifts.append(_rmsnorm(xf.reshape(bsz, t_len, d)[:, -1], lp["norm1"]))
        p_prev = _inproj(shift0[l], lp["norm1"], lp["w_in_rwkv"], l, normalize=False)
        mix, conv_new, wkv_new = _chunk_mixer(
            p.reshape(slabs, n_seq * t_len, IN_PROJ), p_prev.reshape(slabs, n_seq, RWKV_PROJ),
            conv0[l].reshape(slabs, n_seq, CONV_WIDTH - 1, CONV_DIM),
            wkv0.reshape(DEPTH, slabs, n_seq, RWKV_HEADS, HEAD_DIM, HEAD_DIM), ones_blk, lp,
            c_tok=c_tok)
        xf = _ffn(xf, mix.reshape(bsz * t_len, d), lp, final_norm, final=(l == DEPTH - 1))
        convs.append(conv_new.reshape(bsz, CONV_WIDTH - 1, CONV_DIM))
        wkvs.append(wkv_new.reshape(bsz, RWKV_HEADS, HEAD_DIM, HEAD_DIM))
    return xf.reshape(bsz, t_len, d), jnp.stack(shifts), jnp.stack(convs), jnp.stack(wkvs)


def kernel(x_prompt, x_sample, state_shift, state_conv, state_wkv, norm1, w_in, mu_shift, conv_w, conv_norm, w0, w_decay_up, a0, a_up, g_up, k_k, k_a, r_k, ln_x_w, ln_x_b, w_out, norm2, w_gate, w_up, w_down, final_norm):
    prepared = _prepare_params(norm1, w_in, mu_shift, conv_w, conv_norm, w0, w_decay_up, a0,
                               a_up, g_up, k_k, k_a, r_k, ln_x_w, ln_x_b, w_out, norm2,
                               w_gate, w_up, w_down, final_norm)
    bp = x_prompt.shape[0]
    dt = x_prompt.dtype
    p_shift0 = jnp.zeros((DEPTH, bp, D_MODEL), dt)
    p_conv0 = jnp.zeros((DEPTH, bp, CONV_WIDTH - 1, CONV_DIM), dt)
    p_wkv0 = jnp.zeros((DEPTH, bp, RWKV_HEADS, HEAD_DIM, HEAD_DIM), state_wkv.dtype)
    y_p, sh_p, cv_p, wk_p = _trunk(x_prompt, p_shift0, p_conv0, p_wkv0, prepared)
    y_s, sh_s, cv_s, wk_s = _trunk(x_sample, state_shift, state_conv, state_wkv, prepared)
    return (y_p, y_s, sh_p, cv_p, wk_p, sh_s, cv_s, wk_s)
```

```python
import functools

import jax
import jax.numpy as jnp
from jax import lax
from jax.experimental import pallas as pl
from jax.experimental.pallas import tpu as pltpu

D_MODEL = 1024
DEPTH = 4
CONV_DIM = 512
CONV_WIDTH = 3
RWKV_DIM = 512
HEAD_DIM = 64
RWKV_HEADS = RWKV_DIM // HEAD_DIM
LORA_DECAY = 64
LORA_A = 64
LORA_GATE = 128
RWKV_PROJ = 3 * RWKV_DIM + LORA_DECAY + LORA_A + LORA_GATE
IN_PROJ = 3 * CONV_DIM + RWKV_PROJ
D_FF = 2816
RMS_EPS = 1e-6
GN_EPS = 64e-5

LANES = 128
SUBLANES = 8
HEADS_PER_GROUP = LANES // HEAD_DIM
HEAD_GROUPS = RWKV_HEADS // HEADS_PER_GROUP
VMEM_LIMIT_BYTES = 56 * 1024 * 1024
ROW_TILE = 512
INPROJ_TILE = 1024
SLAB_ROWS = 64
LONG_SLABS = 8
SHORT_SLABS = 4
CHAIN_SLABS = 2

f32 = jnp.float32
bf16 = jnp.bfloat16


def _split_bf16(x, terms):
    parts = []
    rem = x
    for i in range(terms):
        part = rem.astype(bf16)
        parts.append(part)
        if i + 1 < terms:
            rem = rem - part.astype(f32)
    return parts


def _group_sum_wide(x, ones_blk, terms):
    rows = x.shape[0]
    n_blk = x.shape[1] // LANES
    lhs = jnp.concatenate(
        [part[:, c * LANES:(c + 1) * LANES] for part in _split_bf16(x, terms)
         for c in range(n_blk)], axis=0)
    out = jnp.dot(lhs, ones_blk, preferred_element_type=f32)
    cols = []
    for c in range(n_blk):
        acc = out[c * rows:(c + 1) * rows]
        for t in range(1, terms):
            acc = acc + out[(t * n_blk + c) * rows:(t * n_blk + c + 1) * rows]
        cols.append(acc)
    return jnp.concatenate(cols, axis=1)


def _sigmoid(x):
    return 1.0 / (1.0 + jnp.exp(-x))


def _rms(x, g):
    return x * lax.rsqrt(jnp.mean(x * x, axis=-1, keepdims=True) + RMS_EPS) * g


def _inproj_kernel(x_ref, g_ref, w_ref, o_ref, *, normalize):
    x = x_ref[...]
    if normalize:
        x = _rms(x, g_ref[...])
    o_ref[...] = jnp.dot(x.astype(bf16), w_ref[...], preferred_element_type=f32)


def _inproj(x, g, w, layer, *, normalize):
    n, d = x.shape
    m = w.shape[2]
    tile = min(INPROJ_TILE, n)
    return pl.pallas_call(
        functools.partial(_inproj_kernel, normalize=normalize),
        grid=(n // tile,),
        in_specs=[
            pl.BlockSpec((tile, d), lambda i: (i, 0)),
            pl.BlockSpec((1, d), lambda i: (0, 0)),
            pl.BlockSpec((None, d, m), lambda i: (layer, 0, 0), pipeline_mode=pl.Buffered(1)),
        ],
        out_specs=pl.BlockSpec((tile, m), lambda i: (i, 0)),
        out_shape=jax.ShapeDtypeStruct((n, m), f32),
        compiler_params=pltpu.CompilerParams(
            dimension_semantics=("arbitrary",), vmem_limit_bytes=VMEM_LIMIT_BYTES),
        name="inproj" if normalize else "shift_proj",
    )(x, g, w)


def _rmsnorm_kernel(x_ref, g_ref, o_ref):
    o_ref[...] = _rms(x_ref[...], g_ref[...])


def _rmsnorm(x, g):
    n, d = x.shape
    tile = min(ROW_TILE, n)
    return pl.pallas_call(
        _rmsnorm_kernel,
        grid=(n // tile,),
        in_specs=[pl.BlockSpec((tile, d), lambda i: (i, 0)),
                  pl.BlockSpec((1, d), lambda i: (0, 0))],
        out_specs=pl.BlockSpec((tile, d), lambda i: (i, 0)),
        out_shape=jax.ShapeDtypeStruct((n, d), f32),
        compiler_params=pltpu.CompilerParams(dimension_semantics=("arbitrary",)),
        name="rmsnorm",
    )(x, g)


def _nt_dot(a, b):
    return lax.dot_general(a, b, (((1,), (1,)), ((), ())), preferred_element_type=f32)


def _tn_dot(a, b):
    return lax.dot_general(a, b, (((0,), (0,)), ((), ())), preferred_element_type=f32)


def _stack_heads(x):
    lane = lax.broadcasted_iota(jnp.int32, x.shape, 1)
    return jnp.concatenate(
        [jnp.where(lane < HEAD_DIM, x, 0.0), jnp.where(lane >= HEAD_DIM, x, 0.0)], axis=0)


def _drain(gen):
    for _ in gen:
        pass


def _interleave(main, sides):
    for _ in main:
        for side in sides:
            next(side, None)
    for side in sides:
        _drain(side)


def _chunk_mixer_kernel(
        p_ref, pprev_ref, conv0_ref, wkv0_ref, ones_ref, tri_ref,
        mu_ref, convw_ref, convn_ref, w0_ref, wdu_ref, a0_ref, aup_ref, gup_ref,
        kk_ref, ka_ref, rk_ref, lnw_ref, lnb_ref,
        mix_ref, convout_ref, wkvout_ref,
        s_ref, cpr_ref, cu_ref,
        *, n_slab, c_tok, single_step):
    t_idx = pl.program_id(1)
    n_t = pl.num_programs(1)
    rows = SLAB_ROWS
    n_seq = rows // c_tok
    c2 = 2 * rows

    def load_packed(ref, q, b, j):
        return jnp.concatenate(
            [ref[q, b, HEADS_PER_GROUP * j + h] for h in range(HEADS_PER_GROUP)], axis=1)

    def store_unpacked(ref, q, b, j, s):
        for h in range(HEADS_PER_GROUP):
            ref[q, b, HEADS_PER_GROUP * j + h] = s[:, h * HEAD_DIM:(h + 1) * HEAD_DIM]

    def load_state(q, b, j):
        return load_packed(wkv0_ref, q, b, j) if single_step else s_ref[q, b, j]

    def store_state(q, b, j, s):
        if single_step:
            store_unpacked(wkvout_ref, q, b, j, s)
        else:
            s_ref[q, b, j] = s

    def for_each_state(body):
        def run(i, carry):
            for j in range(HEAD_GROUPS):
                body(i // n_seq, i % n_seq, j)
            return carry
        lax.fori_loop(0, n_slab * n_seq, run, 0)

    @pl.when(t_idx == 0)
    def _():
        if not single_step:
            def pack(q, b, j):
                s_ref[q, b, j] = load_packed(wkv0_ref, q, b, j)
            for_each_state(pack)
        cpr_ref[...] = pprev_ref[...]
        cu_ref[...] = conv0_ref[...]

    ones_blk = ones_ref[...]
    t_pos = lax.broadcasted_iota(jnp.int32, (rows, 1), 0) % c_tok
    ri = lax.broadcasted_iota(jnp.int32, (c2, c2), 0) % rows
    ci = lax.broadcasted_iota(jnp.int32, (c2, c2), 1) % rows
    same_seq = (ri // c_tok) == (ci // c_tok)
    strict = same_seq & (ci % c_tok < ri % c_tok)
    incl = same_seq & (ci % c_tok <= ri % c_tok)
    incl_wide = jnp.concatenate([incl, incl], axis=1)
    eye = (lax.broadcasted_iota(jnp.int32, (c2, c2), 0)
           == lax.broadcasted_iota(jnp.int32, (c2, c2), 1)).astype(f32)
    groups = range(HEAD_GROUPS)
    seqs = range(n_seq)
    mm = functools.partial(jnp.dot, preferred_element_type=f32)

    def per_seq_rows(load_row):
        return jnp.concatenate(
            [jnp.broadcast_to(load_row(b), (c_tok, load_row(b).shape[1])) for b in seqs], axis=0)

    def shift_rows(x, fills):
        out = pltpu.roll(x, len(fills), axis=0)
        for i, fill in enumerate(fills):
            out = jnp.where(t_pos == i, fill, out)
        return out

    def last_rows(x, back):
        return [x[b * c_tok + c_tok - back:b * c_tok + c_tok - back + 1, :] for b in seqs]

    def per_token(q, ops):
        u = p_ref[q, :, 2 * CONV_DIM:3 * CONV_DIM] * p_ref[q, :, 0:CONV_DIM]
        um2 = per_seq_rows(lambda b: cu_ref[q, b, 0:1, :])
        um1 = per_seq_rows(lambda b: cu_ref[q, b, 1:2, :])
        conv = (shift_rows(u, [um2, um1]) * convw_ref[0:1, :]
                + shift_rows(u, [um1]) * convw_ref[1:2, :] + u * convw_ref[2:3, :])
        for b, (r2, r1) in enumerate(zip(last_rows(u, 2), last_rows(u, 1))):
            cu_ref[q, b, 0:1, :] = r2
            cu_ref[q, b, 1:2, :] = r1
        yc = p_ref[q, :, CONV_DIM:2 * CONV_DIM] * conv
        mix_ref[q, :, 0:CONV_DIM] = _rms(yc, convn_ref[...])

        pr = p_ref[q, :, 3 * CONV_DIM:IN_PROJ]
        shifted = shift_rows(pr, [per_seq_rows(lambda b: cpr_ref[q, b:b + 1, :])])
        for b, r1 in enumerate(last_rows(pr, 1)):
            cpr_ref[q, b:b + 1, :] = r1
        m = pr + (shifted - pr) * mu_ref[...]
        o1, o2, o3 = RWKV_DIM, 2 * RWKV_DIM, 3 * RWKV_DIM
        r = m[:, :o1]
        k = m[:, o1:o2]
        v = m[:, o2:o3]
        wa_d = m[:, o3:o3 + LORA_DECAY + LORA_A]
        g_d = m[:, o3 + LORA_DECAY + LORA_A:]
        z = w0_ref[...] + mm(jnp.tanh(wa_d).astype(bf16), wdu_ref[...])
        a_lin = a0_ref[...] + mm(wa_d.astype(bf16), aup_ref[...])
        ops["g"] = mm(_sigmoid(g_d).astype(bf16), gup_ref[...])
        yield
        nz = -z
        softplus = jnp.maximum(nz, 0.0) + jnp.log1p(jnp.exp(-jnp.abs(nz)))
        log_d = -jnp.exp(-softplus - 0.5)
        a = _sigmoid(a_lin)
        kk = k * kk_ref[...]
        kk = kk / jnp.maximum(jnp.sqrt(_group_sum_wide(kk * kk, ones_blk, 3)), 1e-12)
        k = k * (1.0 + (a - 1.0) * ka_ref[...])
        yield
        tri = tri_ref[...]
        sums = None
        for part in _split_bf16(log_d, 3):
            term = mm(tri, part)
            sums = term if sums is None else sums + term
        cum, total = sums[:rows], sums[rows:]
        yield
        ops["bonus"] = _group_sum_wide(r * k * rk_ref[...], ones_blk, 3) * v
        p_inv = jnp.exp(-cum)
        p_rest = jnp.exp(total - cum)
        beta = kk * a
        ops.update(at=-kk * jnp.exp(cum - log_d), rt=r * jnp.exp(cum), bt=beta * p_inv,
                   kt=k * p_inv, bc=beta * p_rest, kc=k * p_rest, v=v, p_total=jnp.exp(total))
        yield

    def seq_rows(x2, b):
        return [x2[h * rows + b * c_tok:h * rows + (b + 1) * c_tok] for h in range(HEADS_PER_GROUP)]

    def chains(slabs):
        units = [(q, j) for q in slabs for j in groups]
        idx = range(len(units))

        def stack(name, n):
            q, j = units[n]
            return _stack_heads(ops[q][name][:, j * LANES:(j + 1) * LANES])

        at2 = [stack("at", n) for n in idx]
        rt2 = [stack("rt", n) for n in idx]
        v2f = [stack("v", n) for n in idx]
        v2 = [x.astype(bf16) for x in v2f]
        ar = [jnp.concatenate([at2[n], rt2[n]], axis=0).astype(bf16) for n in idx]
        bk = [jnp.concatenate([stack("bt", n), stack("kt", n)], axis=0).astype(bf16) for n in idx]
        s_pack = [[load_state(q, b, j) for b in seqs] for q, j in units]
        scores = [_nt_dot(ar[n], bk[n]) for n in idx]
        yield
        from_a, from_r = [], []
        for n in idx:
            per_seq = [
                _nt_dot(jnp.concatenate(seq_rows(at2[n], b) + seq_rows(rt2[n], b),
                                        axis=0).astype(bf16),
                        _stack_heads(s_pack[n][b]).astype(bf16)) for b in seqs]
            pieces = lambda part: jnp.concatenate(
                [per_seq[b][(part * HEADS_PER_GROUP + h) * c_tok:
                            (part * HEADS_PER_GROUP + h + 1) * c_tok]
                 for h in range(HEADS_PER_GROUP) for b in seqs], axis=0)
            from_a.append(pieces(0))
            from_r.append(pieces(1))
        yield
        l_ab = [jnp.where(strict, scores[n][:c2, :c2], 0.0) for n in idx]
        l_b = [l_ab[n].astype(bf16) for n in idx]
        power = [mm(l_b[n], l_b[n]) for n in idx]
        yield
        w2 = [from_a[n]
              + mm(jnp.where(strict, scores[n][:c2, c2:], 0.0).astype(bf16), v2[n]) for n in idx]
        yield
        inv = [eye + l_ab[n] for n in idx]
        span = 2
        while span < c_tok:
            if 2 * span < c_tok:
                both = [mm(jnp.concatenate([inv[n], power[n]], axis=0).astype(bf16),
                           power[n].astype(bf16)) for n in idx]
                inv = [inv[n] + both[n][:c2] for n in idx]
                power = [both[n][c2:] for n in idx]
            else:
                inv = [inv[n] + mm(inv[n].astype(bf16), power[n].astype(bf16)) for n in idx]
            span *= 2
            yield
        u2f = [mm(inv[n].astype(bf16), w2[n].astype(bf16)) for n in idx]
        yield
        uv = [jnp.concatenate([u2f[n].astype(bf16), v2[n]], axis=0) for n in idx]
        y2 = [from_r[n]
              + mm(jnp.where(incl_wide, scores[n][c2:], 0.0).astype(bf16), uv[n]) for n in idx]
        for q in slabs:
            ops[q]["y"] = jnp.concatenate(
                [y2[n][:rows] + y2[n][rows:] for n in idx if units[n][0] == q], axis=1)
        yield
        for n, (q, j) in enumerate(units):
            bc2, kc2 = stack("bc", n), stack("kc", n)
            for b in seqs:
                lhs = jnp.concatenate(seq_rows(u2f[n], b) + seq_rows(v2f[n], b), axis=0)
                rhs = jnp.concatenate(seq_rows(bc2, b) + seq_rows(kc2, b), axis=0)
                s_new = _tn_dot(lhs.astype(bf16), rhs.astype(bf16))
                decay = ops[q]["p_total"][b * c_tok:b * c_tok + 1, j * LANES:(j + 1) * LANES]
                store_state(q, b, j,
                            s_pack[n][b] * decay + s_new[:HEAD_DIM] + s_new[HEAD_DIM:])
        yield

    def norm_gate(q, ops):
        y = ops["y"]
        mean = _group_sum_wide(y, ones_blk, 3) * (1.0 / HEAD_DIM)
        yield
        ycen = y - mean
        var = _group_sum_wide(ycen * ycen, ones_blk, 3) * (1.0 / HEAD_DIM)
        yield
        yn = ycen * lax.rsqrt(var + GN_EPS) * lnw_ref[...] + lnb_ref[...]
        mix_ref[q, :, CONV_DIM:CONV_DIM + RWKV_DIM] = (yn + ops["bonus"]) * ops["g"]
        yield

    ops = [dict() for _ in range(n_slab)]
    waves = [range(w, min(w + CHAIN_SLABS, n_slab)) for w in range(0, n_slab, CHAIN_SLABS)]
    for q in waves[0]:
        _drain(per_token(q, ops[q]))
    for w, wave in enumerate(waves):
        sides = [per_token(q, ops[q]) for q in (waves[w + 1] if w + 1 < len(waves) else ())]
        sides += [norm_gate(q, ops[q]) for q in (waves[w - 1] if w > 0 else ())]
        _interleave(chains(wave), sides)
    for q in waves[-1]:
        _drain(norm_gate(q, ops[q]))

    @pl.when(t_idx == n_t - 1)
    def _():
        convout_ref[...] = cu_ref[...]
        if not single_step:
            def unpack(q, b, j):
                store_unpacked(wkvout_ref, q, b, j, s_ref[q, b, j])
            for_each_state(unpack)


def _chunk_mixer(p, p_prev, conv0, wkv0, ones_blk, lp, *, c_tok):
    n_total, t_rows, _ = p.shape
    rows = SLAB_ROWS
    n_seq = rows // c_tok
    n_slab = min(LONG_SLABS if n_seq == 1 else SHORT_SLABS, n_total)
    assert c_tok & (c_tok - 1) == 0 and c_tok >= CONV_WIDTH - 1
    assert t_rows % rows == 0 and n_total % n_slab == 0 and (n_seq == 1 or t_rows == rows)
    r = jnp.arange(rows)
    same = (r[:, None] // c_tok) == (r[None, :] // c_tok)
    tri = jnp.concatenate([same & (r[None, :] <= r[:, None]), same], axis=0).astype(bf16)
    const = lambda shape: pl.BlockSpec(shape, lambda i, t: (0,) * len(shape))
    state = lambda shape, **kw: pl.BlockSpec(
        (n_slab,) + shape, lambda i, t: (i,) + (0,) * len(shape), **kw)
    wkv_spec = state((n_seq, RWKV_HEADS, HEAD_DIM, HEAD_DIM), pipeline_mode=pl.Buffered(1))
    in_specs = [
        pl.BlockSpec((n_slab, rows, IN_PROJ), lambda i, t: (i, t, 0)),
        state((n_seq, RWKV_PROJ)),
        state((n_seq, CONV_WIDTH - 1, CONV_DIM)),
        wkv_spec,
        const((LANES, LANES)),
        const((2 * rows, rows)),
        const((1, RWKV_PROJ)),
        const((CONV_WIDTH, CONV_DIM)),
        const((1, CONV_DIM)),
        const((1, RWKV_DIM)),
        const((LORA_DECAY + LORA_A, RWKV_DIM)),
        const((1, RWKV_DIM)),
        const((LORA_DECAY + LORA_A, RWKV_DIM)),
        const((LORA_GATE, RWKV_DIM)),
        const((1, RWKV_DIM)),
        const((1, RWKV_DIM)),
        const((1, RWKV_DIM)),
        const((1, RWKV_DIM)),
        const((1, RWKV_DIM)),
    ]
    out_specs = [
        pl.BlockSpec((n_slab, rows, D_MODEL), lambda i, t: (i, t, 0)),
        state((n_seq, CONV_WIDTH - 1, CONV_DIM)),
        wkv_spec,
    ]
    out_shape = [
        jax.ShapeDtypeStruct((n_total, t_rows, D_MODEL), f32),
        jax.ShapeDtypeStruct(conv0.shape, f32),
        jax.ShapeDtypeStruct(wkv0.shape, f32),
    ]
    scratch = [
        pltpu.VMEM((n_slab, n_seq, HEAD_GROUPS, HEAD_DIM, LANES), f32),
        pltpu.VMEM((n_slab, n_seq, RWKV_PROJ), f32),
        pltpu.VMEM((n_slab, n_seq, CONV_WIDTH - 1, CONV_DIM), f32),
    ]
    return pl.pallas_call(
        functools.partial(_chunk_mixer_kernel, n_slab=n_slab, c_tok=c_tok,
                          single_step=(t_rows == rows)),
        grid=(n_total // n_slab, t_rows // rows),
        in_specs=in_specs,
        out_specs=out_specs,
        out_shape=out_shape,
        scratch_shapes=scratch,
        compiler_params=pltpu.CompilerParams(
            dimension_semantics=("arbitrary", "arbitrary"),
            vmem_limit_bytes=VMEM_LIMIT_BYTES),
        name="chunk_mixer",
    )(p, p_prev, conv0, wkv0, ones_blk, tri,
      lp["mu"], lp["conv_w"], lp["conv_norm"], lp["w0"], lp["wdu"], lp["a0"], lp["aup"],
      lp["gup"], lp["k_k"], lp["k_a"], lp["r_k"], lp["ln_w"], lp["ln_b"])


def _ffn_kernel(x_ref, mix_ref, wout_ref, g2_ref, wg_ref, wu_ref, wd_ref, gf_ref, o_ref, *,
                final):
    x = x_ref[...] + jnp.dot(mix_ref[...].astype(bf16), wout_ref[...],
                             preferred_element_type=f32)
    hb = _rms(x, g2_ref[...]).astype(bf16)
    gate = jnp.dot(hb, wg_ref[...], preferred_element_type=f32)
    up = jnp.dot(hb, wu_ref[...], preferred_element_type=f32)
    act = gate * _sigmoid(gate) * up
    out = x + jnp.dot(act.astype(bf16), wd_ref[...], preferred_element_type=f32)
    o_ref[...] = _rms(out, gf_ref[...]) if final else out


def _ffn(x, mix, lp, final_norm, *, final):
    n, d = x.shape
    tile = min(ROW_TILE, n)
    layer = lp["layer"]
    resident = lambda rows, cols: pl.BlockSpec((None, rows, cols), lambda i: (layer, 0, 0),
                                               pipeline_mode=pl.Buffered(1))
    return pl.pallas_call(
        functools.partial(_ffn_kernel, final=final),
        grid=(n // tile,),
        in_specs=[
            pl.BlockSpec((tile, d), lambda i: (i, 0)),
            pl.BlockSpec((tile, d), lambda i: (i, 0)),
            resident(d, d),
            pl.BlockSpec((1, d), lambda i: (0, 0)),
            resident(d, D_FF),
            resident(d, D_FF),
            resident(D_FF, d),
            pl.BlockSpec((1, d), lambda i: (0, 0)),
        ],
        out_specs=pl.BlockSpec((tile, d), lambda i: (i, 0)),
        out_shape=jax.ShapeDtypeStruct((n, d), f32),
        compiler_params=pltpu.CompilerParams(
            dimension_semantics=("arbitrary",), vmem_limit_bytes=VMEM_LIMIT_BYTES),
        name="outproj_ffn",
    )(x, mix, lp["w_out"], lp["norm2"], lp["w_gate"], lp["w_up"], lp["w_down"], final_norm)


def _prepare_params(norm1, w_in, mu_shift, conv_w, conv_norm, w0, w_decay_up, a0, a_up, g_up,
                    k_k, k_a, r_k, ln_x_w, ln_x_b, w_out, norm2, w_gate, w_up, w_down,
                    final_norm):
    row = lambda a: a.reshape(1, -1)
    zeros_lora = jnp.zeros((LORA_DECAY, RWKV_DIM), f32)
    w_in_b = w_in.astype(bf16)
    stacked = dict(w_in=w_in_b, w_in_rwkv=w_in_b[:, :, 3 * CONV_DIM:],
                   w_out=w_out.astype(bf16), w_gate=w_gate.astype(bf16),
                   w_up=w_up.astype(bf16), w_down=w_down.astype(bf16))
    layers = []
    for l in range(DEPTH):
        layers.append(dict(
            stacked, layer=l, norm1=row(norm1[l]),
            mu=row(mu_shift[l]), conv_w=conv_w[l], conv_norm=row(conv_norm[l]),
            w0=row(w0[l]),
            wdu=jnp.concatenate([w_decay_up[l], zeros_lora], axis=0).astype(bf16),
            a0=row(a0[l]),
            aup=jnp.concatenate([zeros_lora, a_up[l]], axis=0).astype(bf16),
            gup=g_up[l].astype(bf16),
            k_k=row(k_k[l]), k_a=row(k_a[l]), r_k=row(r_k[l]),
            ln_w=row(ln_x_w[l]), ln_b=row(ln_x_b[l]), norm2=row(norm2[l])))
    lane = jnp.arange(LANES) // HEAD_DIM
    ones_blk = (lane[:, None] == lane[None, :]).astype(bf16)
    return layers, row(final_norm), ones_blk


def _trunk(x, shift0, conv0, wkv0, prepared):
    layers, final_norm, ones_blk = prepared
    bsz, t_len, d = x.shape
    c_tok = min(t_len, SLAB_ROWS)
    n_seq = SLAB_ROWS // c_tok
    slabs = bsz // n_seq
    assert t_len % c_tok == 0 and bsz % n_seq == 0 and (n_seq == 1 or t_len == c_tok)
    xf = x.reshape(bsz * t_len, d)
    shifts, convs, wkvs = [], [], []
    for l, lp in enumerate(layers):
        p = _inproj(xf, lp["norm1"], lp["w_in"], l, normalize=True)
        shifts.append(_rmsnorm(xf.reshape(bsz, t_len, d)[:, -1], lp["norm1"]))
        p_prev = _inproj(shift0[l], lp["norm1"], lp["w_in_rwkv"], l, normalize=False)
        mix, conv_new, wkv_new = _chunk_mixer(
            p.reshape(slabs, n_seq * t_len, IN_PROJ), p_prev.reshape(slabs, n_seq, RWKV_PROJ),
            conv0[l].reshape(slabs, n_seq, CONV_WIDTH - 1, CONV_DIM),
            wkv0[l].reshape(slabs, n_seq, RWKV_HEADS, HEAD_DIM, HEAD_DIM), ones_blk, lp,
            c_tok=c_tok)
        xf = _ffn(xf, mix.reshape(bsz * t_len, d), lp, final_norm, final=(l == DEPTH - 1))
        convs.append(conv_new.reshape(bsz, CONV_WIDTH - 1, CONV_DIM))
        wkvs.append(wkv_new.reshape(bsz, RWKV_HEADS, HEAD_DIM, HEAD_DIM))
    return xf.reshape(bsz, t_len, d), jnp.stack(shifts), jnp.stack(convs), jnp.stack(wkvs)


def kernel(x_prompt, x_sample, state_shift, state_conv, state_wkv, norm1, w_in, mu_shift, conv_w, conv_norm, w0, w_decay_up, a0, a_up, g_up, k_k, k_a, r_k, ln_x_w, ln_x_b, w_out, norm2, w_gate, w_up, w_down, final_norm):
    prepared = _prepare_params(norm1, w_in, mu_shift, conv_w, conv_norm, w0, w_decay_up, a0,
                               a_up, g_up, k_k, k_a, r_k, ln_x_w, ln_x_b, w_out, norm2,
                               w_gate, w_up, w_down, final_norm)
    bp = x_prompt.shape[0]
    dt = x_prompt.dtype
    p_shift0 = jnp.zeros((DEPTH, bp, D_MODEL), dt)
    p_conv0 = jnp.zeros((DEPTH, bp, CONV_WIDTH - 1, CONV_DIM), dt)
    p_wkv0 = jnp.zeros((DEPTH, bp, RWKV_HEADS, HEAD_DIM, HEAD_DIM), state_wkv.dtype)
    y_p, sh_p, cv_p, wk_p = _trunk(x_prompt, p_shift0, p_conv0, p_wkv0, prepared)
    y_s, sh_s, cv_s, wk_s = _trunk(x_sample, state_shift, state_conv, state_wkv, prepared)
    return (y_p, y_s, sh_p, cv_p, wk_p, sh_s, cv_s, wk_s)
```

```python
import functools

import jax
import jax.numpy as jnp
from jax import lax
from jax.experimental import pallas as pl
from jax.experimental.pallas import tpu as pltpu

D_MODEL = 1024
DEPTH = 4
CONV_DIM = 512
CONV_WIDTH = 3
RWKV_DIM = 512
HEAD_DIM = 64
RWKV_HEADS = RWKV_DIM // HEAD_DIM
LORA_DECAY = 64
LORA_A = 64
LORA_GATE = 128
RWKV_PROJ = 3 * RWKV_DIM + LORA_DECAY + LORA_A + LORA_GATE
IN_PROJ = 3 * CONV_DIM + RWKV_PROJ
D_FF = 2816
RMS_EPS = 1e-6
GN_EPS = 64e-5

LANES = 128
SUBLANES = 8
HEADS_PER_GROUP = LANES // HEAD_DIM
HEAD_GROUPS = RWKV_HEADS // HEADS_PER_GROUP
VMEM_LIMIT_BYTES = 56 * 1024 * 1024
ROW_TILE = 512
INPROJ_TILE = 1024
SLAB_ROWS = 64
LONG_SLABS = 8
SHORT_SLABS = 4
CHAIN_SLABS = 2

f32 = jnp.float32
bf16 = jnp.bfloat16


def _split_bf16(x, terms):
    parts = []
    rem = x
    for i in range(terms):
        part = rem.astype(bf16)
        parts.append(part)
        if i + 1 < terms:
            rem = rem - part.astype(f32)
    return parts


def _group_sum_wide(x, ones_blk, terms):
    rows = x.shape[0]
    n_blk = x.shape[1] // LANES
    lhs = jnp.concatenate(
        [part[:, c * LANES:(c + 1) * LANES] for part in _split_bf16(x, terms)
         for c in range(n_blk)], axis=0)
    out = jnp.dot(lhs, ones_blk, preferred_element_type=f32)
    cols = []
    for c in range(n_blk):
        acc = out[c * rows:(c + 1) * rows]
        for t in range(1, terms):
            acc = acc + out[(t * n_blk + c) * rows:(t * n_blk + c + 1) * rows]
        cols.append(acc)
    return jnp.concatenate(cols, axis=1)


def _sigmoid(x):
    return 1.0 / (1.0 + jnp.exp(-x))


def _rms(x, g):
    return x * lax.rsqrt(jnp.mean(x * x, axis=-1, keepdims=True) + RMS_EPS) * g


def _inproj_kernel(x_ref, g_ref, w_ref, o_ref, *, normalize):
    x = x_ref[...]
    if normalize:
        x = _rms(x, g_ref[...])
    o_ref[...] = jnp.dot(x.astype(bf16), w_ref[...], preferred_element_type=f32)


def _inproj(x, g, w, layer, *, normalize):
    n, d = x.shape
    m = w.shape[2]
    tile = min(INPROJ_TILE, n)
    return pl.pallas_call(
        functools.partial(_inproj_kernel, normalize=normalize),
        grid=(n // tile,),
        in_specs=[
            pl.BlockSpec((tile, d), lambda i: (i, 0)),
            pl.BlockSpec((1, d), lambda i: (0, 0)),
            pl.BlockSpec((None, d, m), lambda i: (layer, 0, 0), pipeline_mode=pl.Buffered(1)),
        ],
        out_specs=pl.BlockSpec((tile, m), lambda i: (i, 0)),
        out_shape=jax.ShapeDtypeStruct((n, m), f32),
        compiler_params=pltpu.CompilerParams(
            dimension_semantics=("arbitrary",), vmem_limit_bytes=VMEM_LIMIT_BYTES),
        name="inproj" if normalize else "shift_proj",
    )(x, g, w)


def _rmsnorm_kernel(x_ref, g_ref, o_ref):
    o_ref[...] = _rms(x_ref[...], g_ref[...])


def _rmsnorm(x, g):
    n, d = x.shape
    tile = min(ROW_TILE, n)
    return pl.pallas_call(
        _rmsnorm_kernel,
        grid=(n // tile,),
        in_specs=[pl.BlockSpec((tile, d), lambda i: (i, 0)),
                  pl.BlockSpec((1, d), lambda i: (0, 0))],
        out_specs=pl.BlockSpec((tile, d), lambda i: (i, 0)),
        out_shape=jax.ShapeDtypeStruct((n, d), f32),
        compiler_params=pltpu.CompilerParams(dimension_semantics=("arbitrary",)),
        name="rmsnorm",
    )(x, g)


def _nt_dot(a, b):
    return lax.dot_general(a, b, (((1,), (1,)), ((), ())), preferred_element_type=f32)


def _tn_dot(a, b):
    return lax.dot_general(a, b, (((0,), (0,)), ((), ())), preferred_element_type=f32)


def _stack_heads(x):
    lane = lax.broadcasted_iota(jnp.int32, x.shape, 1)
    return jnp.concatenate(
        [jnp.where(lane < HEAD_DIM, x, 0.0), jnp.where(lane >= HEAD_DIM, x, 0.0)], axis=0)


def _drain(gen):
    for _ in gen:
        pass


def _interleave(main, sides):
    for _ in main:
        for side in sides:
            next(side, None)
    for side in sides:
        _drain(side)


def _chunk_mixer_kernel(
        p_ref, pprev_ref, conv0_ref, wkv0_ref, ones_ref, tri_ref,
        mu_ref, convw_ref, convn_ref, w0_ref, wdu_ref, a0_ref, aup_ref, gup_ref,
        kk_ref, ka_ref, rk_ref, lnw_ref, lnb_ref,
        mix_ref, convout_ref, wkvout_ref,
        s_ref, cpr_ref, cu_ref,
        *, n_slab, c_tok, single_step):
    t_idx = pl.program_id(1)
    n_t = pl.num_programs(1)
    rows = SLAB_ROWS
    n_seq = rows // c_tok
    c2 = 2 * rows

    def load_packed(ref, q, b, j):
        return jnp.concatenate(
            [ref[q, b, HEADS_PER_GROUP * j + h] for h in range(HEADS_PER_GROUP)], axis=1)

    def store_unpacked(ref, q, b, j, s):
        for h in range(HEADS_PER_GROUP):
            ref[q, b, HEADS_PER_GROUP * j + h] = s[:, h * HEAD_DIM:(h + 1) * HEAD_DIM]

    def load_state(q, b, j):
        return load_packed(wkv0_ref, q, b, j) if single_step else s_ref[q, b, j]

    def store_state(q, b, j, s):
        if single_step:
            store_unpacked(wkvout_ref, q, b, j, s)
        else:
            s_ref[q, b, j] = s

    def for_each_state(body):
        def run(i, carry):
            for j in range(HEAD_GROUPS):
                body(i // n_seq, i % n_seq, j)
            return carry
        lax.fori_loop(0, n_slab * n_seq, run, 0)

    @pl.when(t_idx == 0)
    def _():
        if not single_step:
            def pack(q, b, j):
                s_ref[q, b, j] = load_packed(wkv0_ref, q, b, j)
            for_each_state(pack)
        cpr_ref[...] = pprev_ref[...]
        cu_ref[...] = conv0_ref[...]

    ones_blk = ones_ref[...]
    t_pos = lax.broadcasted_iota(jnp.int32, (rows, 1), 0) % c_tok
    ri = lax.broadcasted_iota(jnp.int32, (c2, c2), 0) % rows
    ci = lax.broadcasted_iota(jnp.int32, (c2, c2), 1) % rows
    same_seq = (ri // c_tok) == (ci // c_tok)
    strict = same_seq & (ci % c_tok < ri % c_tok)
    incl = same_seq & (ci % c_tok <= ri % c_tok)
    incl_wide = jnp.concatenate([incl, incl], axis=1)
    eye = (lax.broadcasted_iota(jnp.int32, (c2, c2), 0)
           == lax.broadcasted_iota(jnp.int32, (c2, c2), 1)).astype(f32)
    groups = range(HEAD_GROUPS)
    seqs = range(n_seq)
    mm = functools.partial(jnp.dot, preferred_element_type=f32)

    def per_seq_rows(load_row):
        return jnp.concatenate(
            [jnp.broadcast_to(load_row(b), (c_tok, load_row(b).shape[1])) for b in seqs], axis=0)

    def shift_rows(x, fills):
        out = pltpu.roll(x, len(fills), axis=0)
        for i, fill in enumerate(fills):
            out = jnp.where(t_pos == i, fill, out)
        return out

    def last_rows(x, back):
        return [x[b * c_tok + c_tok - back:b * c_tok + c_tok - back + 1, :] for b in seqs]

    def per_token(q, ops):
        u = p_ref[q, :, 2 * CONV_DIM:3 * CONV_DIM] * p_ref[q, :, 0:CONV_DIM]
        um2 = per_seq_rows(lambda b: cu_ref[q, b, 0:1, :])
        um1 = per_seq_rows(lambda b: cu_ref[q, b, 1:2, :])
        conv = (shift_rows(u, [um2, um1]) * convw_ref[0:1, :]
                + shift_rows(u, [um1]) * convw_ref[1:2, :] + u * convw_ref[2:3, :])
        for b, (r2, r1) in enumerate(zip(last_rows(u, 2), last_rows(u, 1))):
            cu_ref[q, b, 0:1, :] = r2
            cu_ref[q, b, 1:2, :] = r1
        yc = p_ref[q, :, CONV_DIM:2 * CONV_DIM] * conv
        mix_ref[q, :, 0:CONV_DIM] = _rms(yc, convn_ref[...])

        pr = p_ref[q, :, 3 * CONV_DIM:IN_PROJ]
        shifted = shift_rows(pr, [per_seq_rows(lambda b: cpr_ref[q, b:b + 1, :])])
        for b, r1 in enumerate(last_rows(pr, 1)):
            cpr_ref[q, b:b + 1, :] = r1
        m = pr + (shifted - pr) * mu_ref[...]
        o1, o2, o3 = RWKV_DIM, 2 * RWKV_DIM, 3 * RWKV_DIM
        r = m[:, :o1]
        k = m[:, o1:o2]
        v = m[:, o2:o3]
        wa_d = m[:, o3:o3 + LORA_DECAY + LORA_A]
        g_d = m[:, o3 + LORA_DECAY + LORA_A:]
        z = w0_ref[...] + mm(jnp.tanh(wa_d).astype(bf16), wdu_ref[...])
        a_lin = a0_ref[...] + mm(wa_d.astype(bf16), aup_ref[...])
        ops["g"] = mm(_sigmoid(g_d).astype(bf16), gup_ref[...])
        yield
        nz = -z
        softplus = jnp.maximum(nz, 0.0) + jnp.log1p(jnp.exp(-jnp.abs(nz)))
        log_d = -jnp.exp(-softplus - 0.5)
        a = _sigmoid(a_lin)
        kk = k * kk_ref[...]
        kk = kk / jnp.maximum(jnp.sqrt(_group_sum_wide(kk * kk, ones_blk, 3)), 1e-12)
        k = k * (1.0 + (a - 1.0) * ka_ref[...])
        yield
        tri = tri_ref[...]
        sums = None
        for part in _split_bf16(log_d, 3):
            term = mm(tri, part)
            sums = term if sums is None else sums + term
        cum, total = sums[:rows], sums[rows:]
        yield
        ops["bonus"] = _group_sum_wide(r * k * rk_ref[...], ones_blk, 3) * v
        p_inv = jnp.exp(-cum)
        p_rest = jnp.exp(total - cum)
        beta = kk * a
        ops.update(at=-kk * jnp.exp(cum - log_d), rt=r * jnp.exp(cum), bt=beta * p_inv,
                   kt=k * p_inv, bc=beta * p_rest, kc=k * p_rest, v=v, p_total=jnp.exp(total))
        yield

    def seq_rows(x2, b):
        return [x2[h * rows + b * c_tok:h * rows + (b + 1) * c_tok] for h in range(HEADS_PER_GROUP)]

    def chains(slabs):
        units = [(q, j) for q in slabs for j in groups]
        idx = range(len(units))

        def stack(name, n):
            q, j = units[n]
            return _stack_heads(ops[q][name][:, j * LANES:(j + 1) * LANES])

        at2 = [stack("at", n) for n in idx]
        rt2 = [stack("rt", n) for n in idx]
        v2f = [stack("v", n) for n in idx]
        v2 = [x.astype(bf16) for x in v2f]
        ar = [jnp.concatenate([at2[n], rt2[n]], axis=0).astype(bf16) for n in idx]
        bk = [jnp.concatenate([stack("bt", n), stack("kt", n)], axis=0).astype(bf16) for n in idx]
        s_pack = [[load_state(q, b, j) for b in seqs] for q, j in units]
        scores = [_nt_dot(ar[n], bk[n]) for n in idx]
        yield
        from_a, from_r = [], []
        for n in idx:
            per_seq = [
                _nt_dot(jnp.concatenate(seq_rows(at2[n], b) + seq_rows(rt2[n], b),
                                        axis=0).astype(bf16),
                        _stack_heads(s_pack[n][b]).astype(bf16)) for b in seqs]
            pieces = lambda part: jnp.concatenate(
                [per_seq[b][(part * HEADS_PER_GROUP + h) * c_tok:
                            (part * HEADS_PER_GROUP + h + 1) * c_tok]
                 for h in range(HEADS_PER_GROUP) for b in seqs], axis=0)
            from_a.append(pieces(0))
            from_r.append(pieces(1))
        yield
        l_ab = [jnp.where(strict, scores[n][:c2, :c2], 0.0) for n in idx]
        l_b = [l_ab[n].astype(bf16) for n in idx]
        power = [mm(l_b[n], l_b[n]) for n in idx]
        yield
        w2 = [from_a[n]
              + mm(jnp.where(strict, scores[n][:c2, c2:], 0.0).astype(bf16), v2[n]) for n in idx]
        yield
        inv = [eye + l_ab[n] for n in idx]
        span = 2
        while span < c_tok:
            if 2 * span < c_tok:
                both = [mm(jnp.concatenate([inv[n], power[n]], axis=0).astype(bf16),
                           power[n].astype(bf16)) for n in idx]
                inv = [inv[n] + both[n][:c2] for n in idx]
                power = [both[n][c2:] for n in idx]
            else:
                inv = [inv[n] + mm(inv[n].astype(bf16), power[n].astype(bf16)) for n in idx]
            span *= 2
            yield
        u2f = [mm(inv[n].astype(bf16), w2[n].astype(bf16)) for n in idx]
        yield
        uv = [jnp.concatenate([u2f[n].astype(bf16), v2[n]], axis=0) for n in idx]
        y2 = [from_r[n]
              + mm(jnp.where(incl_wide, scores[n][c2:], 0.0).astype(bf16), uv[n]) for n in idx]
        for q in slabs:
            ops[q]["y"] = jnp.concatenate(
                [y2[n][:rows] + y2[n][rows:] for n in idx if units[n][0] == q], axis=1)
        yield
        for n, (q, j) in enumerate(units):
            bc2, kc2 = stack("bc", n), stack("kc", n)
            for b in seqs:
                lhs = jnp.concatenate(seq_rows(u2f[n], b) + seq_rows(v2f[n], b), axis=0)
                rhs = jnp.concatenate(seq_rows(bc2, b) + seq_rows(kc2, b), axis=0)
                s_new = _tn_dot(lhs.astype(bf16), rhs.astype(bf16))
                decay = ops[q]["p_total"][b * c_tok:b * c_tok + 1, j * LANES:(j + 1) * LANES]
                store_state(q, b, j,
                            s_pack[n][b] * decay + s_new[:HEAD_DIM] + s_new[HEAD_DIM:])
        yield

    def norm_gate(q, ops):
        y = ops["y"]
        mean = _group_sum_wide(y, ones_blk, 3) * (1.0 / HEAD_DIM)
        yield
        ycen = y - mean
        var = _group_sum_wide(ycen * ycen, ones_blk, 3) * (1.0 / HEAD_DIM)
        yield
        yn = ycen * lax.rsqrt(var + GN_EPS) * lnw_ref[...] + lnb_ref[...]
        mix_ref[q, :, CONV_DIM:CONV_DIM + RWKV_DIM] = (yn + ops["bonus"]) * ops["g"]
        yield

    ops = [dict() for _ in range(n_slab)]
    waves = [range(w, min(w + CHAIN_SLABS, n_slab)) for w in range(0, n_slab, CHAIN_SLABS)]
    for q in waves[0]:
        _drain(per_token(q, ops[q]))
    for w, wave in enumerate(waves):
        sides = [per_token(q, ops[q]) for q in (waves[w + 1] if w + 1 < len(waves) else ())]
        sides += [norm_gate(q, ops[q]) for q in (waves[w - 1] if w > 0 else ())]
        _interleave(chains(wave), sides)
    for q in waves[-1]:
        _drain(norm_gate(q, ops[q]))

    @pl.when(t_idx == n_t - 1)
    def _():
        convout_ref[...] = cu_ref[...]
        if not single_step:
            def unpack(q, b, j):
                store_unpacked(wkvout_ref, q, b, j, s_ref[q, b, j])
            for_each_state(unpack)


def _chunk_mixer(p, p_prev, conv0, wkv0, ones_blk, lp, *, c_tok):
    n_total, t_rows, _ = p.shape
    rows = SLAB_ROWS
    n_seq = rows // c_tok
    n_slab = min(LONG_SLABS if n_seq == 1 else SHORT_SLABS, n_total)
    assert c_tok & (c_tok - 1) == 0 and c_tok >= CONV_WIDTH - 1
    assert t_rows % rows == 0 and n_total % n_slab == 0 and (n_seq == 1 or t_rows == rows)
    r = jnp.arange(rows)
    same = (r[:, None] // c_tok) == (r[None, :] // c_tok)
    tri = jnp.concatenate([same & (r[None, :] <= r[:, None]), same], axis=0).astype(bf16)
    const = lambda shape: pl.BlockSpec(shape, lambda i, t: (0,) * len(shape))
    state = lambda shape, **kw: pl.BlockSpec(
        (n_slab,) + shape, lambda i, t: (i,) + (0,) * len(shape), **kw)
    single_step = t_rows == rows
    wkv_spec = state((n_seq, RWKV_HEADS, HEAD_DIM, HEAD_DIM))
    in_specs = [
        pl.BlockSpec((n_slab, rows, IN_PROJ), lambda i, t: (i, t, 0)),
        state((n_seq, RWKV_PROJ)),
        state((n_seq, CONV_WIDTH - 1, CONV_DIM)),
        wkv_spec,
        const((LANES, LANES)),
        const((2 * rows, rows)),
        const((1, RWKV_PROJ)),
        const((CONV_WIDTH, CONV_DIM)),
        const((1, CONV_DIM)),
        const((1, RWKV_DIM)),
        const((LORA_DECAY + LORA_A, RWKV_DIM)),
        const((1, RWKV_DIM)),
        const((LORA_DECAY + LORA_A, RWKV_DIM)),
        const((LORA_GATE, RWKV_DIM)),
        const((1, RWKV_DIM)),
        const((1, RWKV_DIM)),
        const((1, RWKV_DIM)),
        const((1, RWKV_DIM)),
        const((1, RWKV_DIM)),
    ]
    out_specs = [
        pl.BlockSpec((n_slab, rows, D_MODEL), lambda i, t: (i, t, 0)),
        state((n_seq, CONV_WIDTH - 1, CONV_DIM)),
        wkv_spec,
    ]
    out_shape = [
        jax.ShapeDtypeStruct((n_total, t_rows, D_MODEL), f32),
        jax.ShapeDtypeStruct(conv0.shape, f32),
        jax.ShapeDtypeStruct(wkv0.shape, f32),
    ]
    scratch = [
        pltpu.VMEM((1, 1, 1, SUBLANES, LANES) if single_step
                   else (n_slab, n_seq, HEAD_GROUPS, HEAD_DIM, LANES), f32),
        pltpu.VMEM((n_slab, n_seq, RWKV_PROJ), f32),
        pltpu.VMEM((n_slab, n_seq, CONV_WIDTH - 1, CONV_DIM), f32),
    ]
    return pl.pallas_call(
        functools.partial(_chunk_mixer_kernel, n_slab=n_slab, c_tok=c_tok,
                          single_step=single_step),
        grid=(n_total // n_slab, t_rows // rows),
        in_specs=in_specs,
        out_specs=out_specs,
        out_shape=out_shape,
        scratch_shapes=scratch,
        compiler_params=pltpu.CompilerParams(
            dimension_semantics=("arbitrary", "arbitrary"),
            vmem_limit_bytes=VMEM_LIMIT_BYTES),
        name="chunk_mixer",
    )(p, p_prev, conv0, wkv0, ones_blk, tri,
      lp["mu"], lp["conv_w"], lp["conv_norm"], lp["w0"], lp["wdu"], lp["a0"], lp["aup"],
      lp["gup"], lp["k_k"], lp["k_a"], lp["r_k"], lp["ln_w"], lp["ln_b"])


def _ffn_kernel(x_ref, mix_ref, wout_ref, g2_ref, wg_ref, wu_ref, wd_ref, gf_ref, o_ref, *,
                final):
    x = x_ref[...] + jnp.dot(mix_ref[...].astype(bf16), wout_ref[...],
                             preferred_element_type=f32)
    hb = _rms(x, g2_ref[...]).astype(bf16)
    gate = jnp.dot(hb, wg_ref[...], preferred_element_type=f32)
    up = jnp.dot(hb, wu_ref[...], preferred_element_type=f32)
    act = gate * _sigmoid(gate) * up
    out = x + jnp.dot(act.astype(bf16), wd_ref[...], preferred_element_type=f32)
    o_ref[...] = _rms(out, gf_ref[...]) if final else out


def _ffn(x, mix, lp, final_norm, *, final):
    n, d = x.shape
    tile = min(ROW_TILE, n)
    layer = lp["layer"]
    resident = lambda rows, cols: pl.BlockSpec((None, rows, cols), lambda i: (layer, 0, 0),
                                               pipeline_mode=pl.Buffered(1))
    return pl.pallas_call(
        functools.partial(_ffn_kernel, final=final),
        grid=(n // tile,),
        in_specs=[
            pl.BlockSpec((tile, d), lambda i: (i, 0)),
            pl.BlockSpec((tile, d), lambda i: (i, 0)),
            resident(d, d),
            pl.BlockSpec((1, d), lambda i: (0, 0)),
            resident(d, D_FF),
            resident(d, D_FF),
            resident(D_FF, d),
            pl.BlockSpec((1, d), lambda i: (0, 0)),
        ],
        out_specs=pl.BlockSpec((tile, d), lambda i: (i, 0)),
        out_shape=jax.ShapeDtypeStruct((n, d), f32),
        compiler_params=pltpu.CompilerParams(
            dimension_semantics=("arbitrary",), vmem_limit_bytes=VMEM_LIMIT_BYTES),
        name="outproj_ffn",
    )(x, mix, lp["w_out"], lp["norm2"], lp["w_gate"], lp["w_up"], lp["w_down"], final_norm)


def _prepare_params(norm1, w_in, mu_shift, conv_w, conv_norm, w0, w_decay_up, a0, a_up, g_up,
                    k_k, k_a, r_k, ln_x_w, ln_x_b, w_out, norm2, w_gate, w_up, w_down,
                    final_norm):
    row = lambda a: a.reshape(1, -1)
    zeros_lora = jnp.zeros((LORA_DECAY, RWKV_DIM), f32)
    w_in_b = w_in.astype(bf16)
    stacked = dict(w_in=w_in_b, w_in_rwkv=w_in_b[:, :, 3 * CONV_DIM:],
                   w_out=w_out.astype(bf16), w_gate=w_gate.astype(bf16),
                   w_up=w_up.astype(bf16), w_down=w_down.astype(bf16))
    layers = []
    for l in range(DEPTH):
        layers.append(dict(
            stacked, layer=l, norm1=row(norm1[l]),
            mu=row(mu_shift[l]), conv_w=conv_w[l], conv_norm=row(conv_norm[l]),
            w0=row(w0[l]),
            wdu=jnp.concatenate([w_decay_up[l], zeros_lora], axis=0).astype(bf16),
            a0=row(a0[l]),
            aup=jnp.concatenate([zeros_lora, a_up[l]], axis=0).astype(bf16),
            gup=g_up[l].astype(bf16),
            k_k=row(k_k[l]), k_a=row(k_a[l]), r_k=row(r_k[l]),
            ln_w=row(ln_x_w[l]), ln_b=row(ln_x_b[l]), norm2=row(norm2[l])))
    lane = jnp.arange(LANES) // HEAD_DIM
    ones_blk = (lane[:, None] == lane[None, :]).astype(bf16)
    return layers, row(final_norm), ones_blk


def _trunk(x, shift0, conv0, wkv0, prepared):
    layers, final_norm, ones_blk = prepared
    bsz, t_len, d = x.shape
    c_tok = min(t_len, SLAB_ROWS)
    n_seq = SLAB_ROWS // c_tok
    slabs = bsz // n_seq
    assert t_len % c_tok == 0 and bsz % n_seq == 0 and (n_seq == 1 or t_len == c_tok)
    xf = x.reshape(bsz * t_len, d)
    shifts, convs, wkvs = [], [], []
    for l, lp in enumerate(layers):
        p = _inproj(xf, lp["norm1"], lp["w_in"], l, normalize=True)
        shifts.append(_rmsnorm(xf.reshape(bsz, t_len, d)[:, -1], lp["norm1"]))
        p_prev = _inproj(shift0[l], lp["norm1"], lp["w_in_rwkv"], l, normalize=False)
        mix, conv_new, wkv_new = _chunk_mixer(
            p.reshape(slabs, n_seq * t_len, IN_PROJ), p_prev.reshape(slabs, n_seq, RWKV_PROJ),
            conv0[l].reshape(slabs, n_seq, CONV_WIDTH - 1, CONV_DIM),
            wkv0[l].reshape(slabs, n_seq, RWKV_HEADS, HEAD_DIM, HEAD_DIM), ones_blk, lp,
            c_tok=c_tok)
        xf = _ffn(xf, mix.reshape(bsz * t_len, d), lp, final_norm, final=(l == DEPTH - 1))
        convs.append(conv_new.reshape(bsz, CONV_WIDTH - 1, CONV_DIM))
        wkvs.append(wkv_new.reshape(bsz, RWKV_HEADS, HEAD_DIM, HEAD_DIM))
    return xf.reshape(bsz, t_len, d), jnp.stack(shifts), jnp.stack(convs), jnp.stack(wkvs)


def kernel(x_prompt, x_sample, state_shift, state_conv, state_wkv, norm1, w_in, mu_shift, conv_w, conv_norm, w0, w_decay_up, a0, a_up, g_up, k_k, k_a, r_k, ln_x_w, ln_x_b, w_out, norm2, w_gate, w_up, w_down, final_norm):
    prepared = _prepare_params(norm1, w_in, mu_shift, conv_w, conv_norm, w0, w_decay_up, a0,
                               a_up, g_up, k_k, k_a, r_k, ln_x_w, ln_x_b, w_out, norm2,
                               w_gate, w_up, w_down, final_norm)
    bp = x_prompt.shape[0]
    dt = x_prompt.dtype
    p_shift0 = jnp.zeros((DEPTH, bp, D_MODEL), dt)
    p_conv0 = jnp.zeros((DEPTH, bp, CONV_WIDTH - 1, CONV_DIM), dt)
    p_wkv0 = jnp.zeros((DEPTH, bp, RWKV_HEADS, HEAD_DIM, HEAD_DIM), state_wkv.dtype)
    y_p, sh_p, cv_p, wk_p = _trunk(x_prompt, p_shift0, p_conv0, p_wkv0, prepared)
    y_s, sh_s, cv_s, wk_s = _trunk(x_sample, state_shift, state_conv, state_wkv, prepared)
    return (y_p, y_s, sh_p, cv_p, wk_p, sh_s, cv_s, wk_s)
```

```python
import functools

import jax
import jax.numpy as jnp
from jax import lax
from jax.experimental import pallas as pl
from jax.experimental.pallas import tpu as pltpu

D_MODEL = 1024
DEPTH = 4
CONV_DIM = 512
CONV_WIDTH = 3
RWKV_DIM = 512
HEAD_DIM = 64
RWKV_HEADS = RWKV_DIM // HEAD_DIM
LORA_DECAY = 64
LORA_A = 64
LORA_GATE = 128
RWKV_PROJ = 3 * RWKV_DIM + LORA_DECAY + LORA_A + LORA_GATE
IN_PROJ = 3 * CONV_DIM + RWKV_PROJ
D_FF = 2816
RMS_EPS = 1e-6
GN_EPS = 64e-5

LANES = 128
SUBLANES = 8
HEADS_PER_GROUP = LANES // HEAD_DIM
HEAD_GROUPS = RWKV_HEADS // HEADS_PER_GROUP
VMEM_LIMIT_BYTES = 56 * 1024 * 1024
ROW_TILE = 512
INPROJ_TILE = 1024
SLAB_ROWS = 64
LONG_SLABS = 8
SHORT_SLABS = 4
CHAIN_SLABS = 2

f32 = jnp.float32
bf16 = jnp.bfloat16


def _split_bf16(x, terms):
    parts = []
    rem = x
    for i in range(terms):
        part = rem.astype(bf16)
        parts.append(part)
        if i + 1 < terms:
            rem = rem - part.astype(f32)
    return parts


def _group_sum_wide(x):
    lane = lax.broadcasted_iota(jnp.int32, (x.shape[0], LANES), 1)
    first = lane < HEAD_DIM
    cols = []
    for c in range(x.shape[1] // LANES):
        blk = x[:, c * LANES:(c + 1) * LANES]
        lo = jnp.sum(jnp.where(first, blk, 0.0), axis=1, keepdims=True)
        hi = jnp.sum(jnp.where(first, 0.0, blk), axis=1, keepdims=True)
        cols.append(jnp.where(first, lo, hi))
    return jnp.concatenate(cols, axis=1)


def _sigmoid(x):
    return 1.0 / (1.0 + jnp.exp(-x))


def _rms(x, g):
    return x * lax.rsqrt(jnp.mean(x * x, axis=-1, keepdims=True) + RMS_EPS) * g


def _inproj_kernel(x_ref, g_ref, w_ref, o_ref, *, normalize):
    x = x_ref[...]
    if normalize:
        x = _rms(x, g_ref[...])
    o_ref[...] = jnp.dot(x.astype(bf16), w_ref[...], preferred_element_type=f32)


def _inproj(x, g, w, layer, *, normalize):
    n, d = x.shape
    m = w.shape[2]
    tile = min(INPROJ_TILE, n)
    return pl.pallas_call(
        functools.partial(_inproj_kernel, normalize=normalize),
        grid=(n // tile,),
        in_specs=[
            pl.BlockSpec((tile, d), lambda i: (i, 0)),
            pl.BlockSpec((1, d), lambda i: (0, 0)),
            pl.BlockSpec((None, d, m), lambda i: (layer, 0, 0), pipeline_mode=pl.Buffered(1)),
        ],
        out_specs=pl.BlockSpec((tile, m), lambda i: (i, 0)),
        out_shape=jax.ShapeDtypeStruct((n, m), f32),
        compiler_params=pltpu.CompilerParams(
            dimension_semantics=("arbitrary",), vmem_limit_bytes=VMEM_LIMIT_BYTES),
        name="inproj" if normalize else "shift_proj",
    )(x, g, w)


def _rmsnorm_kernel(x_ref, g_ref, o_ref):
    o_ref[...] = _rms(x_ref[...], g_ref[...])


def _rmsnorm(x, g):
    n, d = x.shape
    tile = min(ROW_TILE, n)
    return pl.pallas_call(
        _rmsnorm_kernel,
        grid=(n // tile,),
        in_specs=[pl.BlockSpec((tile, d), lambda i: (i, 0)),
                  pl.BlockSpec((1, d), lambda i: (0, 0))],
        out_specs=pl.BlockSpec((tile, d), lambda i: (i, 0)),
        out_shape=jax.ShapeDtypeStruct((n, d), f32),
        compiler_params=pltpu.CompilerParams(dimension_semantics=("arbitrary",)),
        name="rmsnorm",
    )(x, g)


def _nt_dot(a, b):
    return lax.dot_general(a, b, (((1,), (1,)), ((), ())), preferred_element_type=f32)


def _tn_dot(a, b):
    return lax.dot_general(a, b, (((0,), (0,)), ((), ())), preferred_element_type=f32)


def _stack_heads(x):
    lane = lax.broadcasted_iota(jnp.int32, x.shape, 1)
    return jnp.concatenate(
        [jnp.where(lane < HEAD_DIM, x, 0.0), jnp.where(lane >= HEAD_DIM, x, 0.0)], axis=0)


def _drain(gen):
    for _ in gen:
        pass


def _interleave(main, sides):
    for _ in main:
        for side in sides:
            next(side, None)
    for side in sides:
        _drain(side)


def _chunk_mixer_kernel(
        p_ref, pprev_ref, conv0_ref, wkv0_ref, tri_ref,
        mu_ref, convw_ref, convn_ref, w0_ref, wdu_ref, a0_ref, aup_ref, gup_ref,
        kk_ref, ka_ref, rk_ref, lnw_ref, lnb_ref,
        mix_ref, convout_ref, wkvout_ref,
        s_ref, cpr_ref, cu_ref,
        *, n_slab, c_tok, single_step):
    t_idx = pl.program_id(1)
    n_t = pl.num_programs(1)
    rows = SLAB_ROWS
    n_seq = rows // c_tok
    c2 = 2 * rows

    def load_packed(ref, q, b, j):
        return jnp.concatenate(
            [ref[q, b, HEADS_PER_GROUP * j + h] for h in range(HEADS_PER_GROUP)], axis=1)

    def store_unpacked(ref, q, b, j, s):
        for h in range(HEADS_PER_GROUP):
            ref[q, b, HEADS_PER_GROUP * j + h] = s[:, h * HEAD_DIM:(h + 1) * HEAD_DIM]

    def load_state(q, b, j):
        return load_packed(wkv0_ref, q, b, j) if single_step else s_ref[q, b, j]

    def store_state(q, b, j, s):
        if single_step:
            store_unpacked(wkvout_ref, q, b, j, s)
        else:
            s_ref[q, b, j] = s

    def for_each_state(body):
        def run(i, carry):
            for j in range(HEAD_GROUPS):
                body(i // n_seq, i % n_seq, j)
            return carry
        lax.fori_loop(0, n_slab * n_seq, run, 0)

    @pl.when(t_idx == 0)
    def _():
        if not single_step:
            def pack(q, b, j):
                s_ref[q, b, j] = load_packed(wkv0_ref, q, b, j)
            for_each_state(pack)
        cpr_ref[...] = pprev_ref[...]
        cu_ref[...] = conv0_ref[...]

    t_pos = lax.broadcasted_iota(jnp.int32, (rows, 1), 0) % c_tok
    ri = lax.broadcasted_iota(jnp.int32, (c2, c2), 0) % rows
    ci = lax.broadcasted_iota(jnp.int32, (c2, c2), 1) % rows
    same_seq = (ri // c_tok) == (ci // c_tok)
    strict = same_seq & (ci % c_tok < ri % c_tok)
    incl = same_seq & (ci % c_tok <= ri % c_tok)
    incl_wide = jnp.concatenate([incl, incl], axis=1)
    eye = (lax.broadcasted_iota(jnp.int32, (c2, c2), 0)
           == lax.broadcasted_iota(jnp.int32, (c2, c2), 1)).astype(f32)
    groups = range(HEAD_GROUPS)
    seqs = range(n_seq)
    mm = functools.partial(jnp.dot, preferred_element_type=f32)

    def per_seq_rows(load_row):
        return jnp.concatenate(
            [jnp.broadcast_to(load_row(b), (c_tok, load_row(b).shape[1])) for b in seqs], axis=0)

    def shift_rows(x, fills):
        out = pltpu.roll(x, len(fills), axis=0)
        for i, fill in enumerate(fills):
            out = jnp.where(t_pos == i, fill, out)
        return out

    def last_rows(x, back):
        return [x[b * c_tok + c_tok - back:b * c_tok + c_tok - back + 1, :] for b in seqs]

    def per_token(q, ops):
        u = p_ref[q, :, 2 * CONV_DIM:3 * CONV_DIM] * p_ref[q, :, 0:CONV_DIM]
        um2 = per_seq_rows(lambda b: cu_ref[q, b, 0:1, :])
        um1 = per_seq_rows(lambda b: cu_ref[q, b, 1:2, :])
        conv = (shift_rows(u, [um2, um1]) * convw_ref[0:1, :]
                + shift_rows(u, [um1]) * convw_ref[1:2, :] + u * convw_ref[2:3, :])
        for b, (r2, r1) in enumerate(zip(last_rows(u, 2), last_rows(u, 1))):
            cu_ref[q, b, 0:1, :] = r2
            cu_ref[q, b, 1:2, :] = r1
        yc = p_ref[q, :, CONV_DIM:2 * CONV_DIM] * conv
        mix_ref[q, :, 0:CONV_DIM] = _rms(yc, convn_ref[...])

        pr = p_ref[q, :, 3 * CONV_DIM:IN_PROJ]
        shifted = shift_rows(pr, [per_seq_rows(lambda b: cpr_ref[q, b:b + 1, :])])
        for b, r1 in enumerate(last_rows(pr, 1)):
            cpr_ref[q, b:b + 1, :] = r1
        m = pr + (shifted - pr) * mu_ref[...]
        o1, o2, o3 = RWKV_DIM, 2 * RWKV_DIM, 3 * RWKV_DIM
        r = m[:, :o1]
        k = m[:, o1:o2]
        v = m[:, o2:o3]
        wa_d = m[:, o3:o3 + LORA_DECAY + LORA_A]
        g_d = m[:, o3 + LORA_DECAY + LORA_A:]
        z = w0_ref[...] + mm(jnp.tanh(wa_d).astype(bf16), wdu_ref[...])
        a_lin = a0_ref[...] + mm(wa_d.astype(bf16), aup_ref[...])
        ops["g"] = mm(_sigmoid(g_d).astype(bf16), gup_ref[...])
        yield
        nz = -z
        softplus = jnp.maximum(nz, 0.0) + jnp.log1p(jnp.exp(-jnp.abs(nz)))
        log_d = -jnp.exp(-softplus - 0.5)
        a = _sigmoid(a_lin)
        kk = k * kk_ref[...]
        kk = kk / jnp.maximum(jnp.sqrt(_group_sum_wide(kk * kk)), 1e-12)
        k = k * (1.0 + (a - 1.0) * ka_ref[...])
        yield
        tri = tri_ref[...]
        sums = None
        for part in _split_bf16(log_d, 3):
            term = mm(tri, part)
            sums = term if sums is None else sums + term
        cum, total = sums[:rows], sums[rows:]
        yield
        ops["bonus"] = _group_sum_wide(r * k * rk_ref[...]) * v
        p_inv = jnp.exp(-cum)
        p_rest = jnp.exp(total - cum)
        beta = kk * a
        ops.update(at=-kk * jnp.exp(cum - log_d), rt=r * jnp.exp(cum), bt=beta * p_inv,
                   kt=k * p_inv, bc=beta * p_rest, kc=k * p_rest, v=v, p_total=jnp.exp(total))
        yield

    def seq_rows(x2, b):
        return [x2[h * rows + b * c_tok:h * rows + (b + 1) * c_tok] for h in range(HEADS_PER_GROUP)]

    def chains(slabs):
        units = [(q, j) for q in slabs for j in groups]
        idx = range(len(units))

        def stack(name, n):
            q, j = units[n]
            return _stack_heads(ops[q][name][:, j * LANES:(j + 1) * LANES])

        at2 = [stack("at", n) for n in idx]
        rt2 = [stack("rt", n) for n in idx]
        v2f = [stack("v", n) for n in idx]
        v2 = [x.astype(bf16) for x in v2f]
        ar = [jnp.concatenate([at2[n], rt2[n]], axis=0).astype(bf16) for n in idx]
        bk = [jnp.concatenate([stack("bt", n), stack("kt", n)], axis=0).astype(bf16) for n in idx]
        s_pack = [[load_state(q, b, j) for b in seqs] for q, j in units]
        scores = [_nt_dot(ar[n], bk[n]) for n in idx]
        yield
        from_a, from_r = [], []
        for n in idx:
            per_seq = [
                _nt_dot(jnp.concatenate(seq_rows(at2[n], b) + seq_rows(rt2[n], b),
                                        axis=0).astype(bf16),
                        _stack_heads(s_pack[n][b]).astype(bf16)) for b in seqs]
            pieces = lambda part: jnp.concatenate(
                [per_seq[b][(part * HEADS_PER_GROUP + h) * c_tok:
                            (part * HEADS_PER_GROUP + h + 1) * c_tok]
                 for h in range(HEADS_PER_GROUP) for b in seqs], axis=0)
            from_a.append(pieces(0))
            from_r.append(pieces(1))
        yield
        l_ab = [jnp.where(strict, scores[n][:c2, :c2], 0.0) for n in idx]
        l_b = [l_ab[n].astype(bf16) for n in idx]
        power = [mm(l_b[n], l_b[n]) for n in idx]
        yield
        w2 = [from_a[n]
              + mm(jnp.where(strict, scores[n][:c2, c2:], 0.0).astype(bf16), v2[n]) for n in idx]
        yield
        inv = [eye + l_ab[n] for n in idx]
        span = 2
        while span < c_tok:
            if 2 * span < c_tok:
                both = [mm(jnp.concatenate([inv[n], power[n]], axis=0).astype(bf16),
                           power[n].astype(bf16)) for n in idx]
                inv = [inv[n] + both[n][:c2] for n in idx]
                power = [both[n][c2:] for n in idx]
            else:
                inv = [inv[n] + mm(inv[n].astype(bf16), power[n].astype(bf16)) for n in idx]
            span *= 2
            yield
        u2f = [mm(inv[n].astype(bf16), w2[n].astype(bf16)) for n in idx]
        yield
        uv = [jnp.concatenate([u2f[n].astype(bf16), v2[n]], axis=0) for n in idx]
        y2 = [from_r[n]
              + mm(jnp.where(incl_wide, scores[n][c2:], 0.0).astype(bf16), uv[n]) for n in idx]
        for q in slabs:
            ops[q]["y"] = jnp.concatenate(
                [y2[n][:rows] + y2[n][rows:] for n in idx if units[n][0] == q], axis=1)
        yield
        for n, (q, j) in enumerate(units):
            bc2, kc2 = stack("bc", n), stack("kc", n)
            for b in seqs:
                lhs = jnp.concatenate(seq_rows(u2f[n], b) + seq_rows(v2f[n], b), axis=0)
                rhs = jnp.concatenate(seq_rows(bc2, b) + seq_rows(kc2, b), axis=0)
                s_new = _tn_dot(lhs.astype(bf16), rhs.astype(bf16))
                decay = ops[q]["p_total"][b * c_tok:b * c_tok + 1, j * LANES:(j + 1) * LANES]
                store_state(q, b, j,
                            s_pack[n][b] * decay + s_new[:HEAD_DIM] + s_new[HEAD_DIM:])
        yield

    def norm_gate(q, ops):
        y = ops["y"]
        mean = _group_sum_wide(y) * (1.0 / HEAD_DIM)
        yield
        ycen = y - mean
        var = _group_sum_wide(ycen * ycen) * (1.0 / HEAD_DIM)
        yield
        yn = ycen * lax.rsqrt(var + GN_EPS) * lnw_ref[...] + lnb_ref[...]
        mix_ref[q, :, CONV_DIM:CONV_DIM + RWKV_DIM] = (yn + ops["bonus"]) * ops["g"]
        yield

    ops = [dict() for _ in range(n_slab)]
    waves = [range(w, min(w + CHAIN_SLABS, n_slab)) for w in range(0, n_slab, CHAIN_SLABS)]
    for q in waves[0]:
        _drain(per_token(q, ops[q]))
    for w, wave in enumerate(waves):
        sides = [per_token(q, ops[q]) for q in (waves[w + 1] if w + 1 < len(waves) else ())]
        sides += [norm_gate(q, ops[q]) for q in (waves[w - 1] if w > 0 else ())]
        _interleave(chains(wave), sides)
    for q in waves[-1]:
        _drain(norm_gate(q, ops[q]))

    @pl.when(t_idx == n_t - 1)
    def _():
        convout_ref[...] = cu_ref[...]
        if not single_step:
            def unpack(q, b, j):
                store_unpacked(wkvout_ref, q, b, j, s_ref[q, b, j])
            for_each_state(unpack)


def _chunk_mixer(p, p_prev, conv0, wkv0, lp, *, c_tok):
    n_total, t_rows, _ = p.shape
    rows = SLAB_ROWS
    n_seq = rows // c_tok
    n_slab = min(LONG_SLABS if n_seq == 1 else SHORT_SLABS, n_total)
    assert c_tok & (c_tok - 1) == 0 and c_tok >= CONV_WIDTH - 1
    assert t_rows % rows == 0 and n_total % n_slab == 0 and (n_seq == 1 or t_rows == rows)
    r = jnp.arange(rows)
    same = (r[:, None] // c_tok) == (r[None, :] // c_tok)
    tri = jnp.concatenate([same & (r[None, :] <= r[:, None]), same], axis=0).astype(bf16)
    const = lambda shape: pl.BlockSpec(shape, lambda i, t: (0,) * len(shape))
    state = lambda shape, **kw: pl.BlockSpec(
        (n_slab,) + shape, lambda i, t: (i,) + (0,) * len(shape), **kw)
    single_step = t_rows == rows
    wkv_spec = state((n_seq, RWKV_HEADS, HEAD_DIM, HEAD_DIM))
    in_specs = [
        pl.BlockSpec((n_slab, rows, IN_PROJ), lambda i, t: (i, t, 0)),
        state((n_seq, RWKV_PROJ)),
        state((n_seq, CONV_WIDTH - 1, CONV_DIM)),
        wkv_spec,
        const((2 * rows, rows)),
        const((1, RWKV_PROJ)),
        const((CONV_WIDTH, CONV_DIM)),
        const((1, CONV_DIM)),
        const((1, RWKV_DIM)),
        const((LORA_DECAY + LORA_A, RWKV_DIM)),
        const((1, RWKV_DIM)),
        const((LORA_DECAY + LORA_A, RWKV_DIM)),
        const((LORA_GATE, RWKV_DIM)),
        const((1, RWKV_DIM)),
        const((1, RWKV_DIM)),
        const((1, RWKV_DIM)),
        const((1, RWKV_DIM)),
        const((1, RWKV_DIM)),
    ]
    out_specs = [
        pl.BlockSpec((n_slab, rows, D_MODEL), lambda i, t: (i, t, 0)),
        state((n_seq, CONV_WIDTH - 1, CONV_DIM)),
        wkv_spec,
    ]
    out_shape = [
        jax.ShapeDtypeStruct((n_total, t_rows, D_MODEL), f32),
        jax.ShapeDtypeStruct(conv0.shape, f32),
        jax.ShapeDtypeStruct(wkv0.shape, f32),
    ]
    scratch = [
        pltpu.VMEM((1, 1, 1, SUBLANES, LANES) if single_step
                   else (n_slab, n_seq, HEAD_GROUPS, HEAD_DIM, LANES), f32),
        pltpu.VMEM((n_slab, n_seq, RWKV_PROJ), f32),
        pltpu.VMEM((n_slab, n_seq, CONV_WIDTH - 1, CONV_DIM), f32),
    ]
    return pl.pallas_call(
        functools.partial(_chunk_mixer_kernel, n_slab=n_slab, c_tok=c_tok,
                          single_step=single_step),
        grid=(n_total // n_slab, t_rows // rows),
        in_specs=in_specs,
        out_specs=out_specs,
        out_shape=out_shape,
        scratch_shapes=scratch,
        compiler_params=pltpu.CompilerParams(
            dimension_semantics=("arbitrary", "arbitrary"),
            vmem_limit_bytes=VMEM_LIMIT_BYTES),
        name="chunk_mixer",
    )(p, p_prev, conv0, wkv0, tri,
      lp["mu"], lp["conv_w"], lp["conv_norm"], lp["w0"], lp["wdu"], lp["a0"], lp["aup"],
      lp["gup"], lp["k_k"], lp["k_a"], lp["r_k"], lp["ln_w"], lp["ln_b"])


def _ffn_kernel(x_ref, mix_ref, wout_ref, g2_ref, wg_ref, wu_ref, wd_ref, gf_ref, o_ref, *,
                final):
    x = x_ref[...] + jnp.dot(mix_ref[...].astype(bf16), wout_ref[...],
                             preferred_element_type=f32)
    hb = _rms(x, g2_ref[...]).astype(bf16)
    gate = jnp.dot(hb, wg_ref[...], preferred_element_type=f32)
    up = jnp.dot(hb, wu_ref[...], preferred_element_type=f32)
    act = gate * _sigmoid(gate) * up
    out = x + jnp.dot(act.astype(bf16), wd_ref[...], preferred_element_type=f32)
    o_ref[...] = _rms(out, gf_ref[...]) if final else out


def _ffn(x, mix, lp, final_norm, *, final):
    n, d = x.shape
    tile = min(ROW_TILE, n)
    layer = lp["layer"]
    resident = lambda rows, cols: pl.BlockSpec((None, rows, cols), lambda i: (layer, 0, 0),
                                               pipeline_mode=pl.Buffered(1))
    return pl.pallas_call(
        functools.partial(_ffn_kernel, final=final),
        grid=(n // tile,),
        in_specs=[
            pl.BlockSpec((tile, d), lambda i: (i, 0)),
            pl.BlockSpec((tile, d), lambda i: (i, 0)),
            resident(d, d),
            pl.BlockSpec((1, d), lambda i: (0, 0)),
            resident(d, D_FF),
            resident(d, D_FF),
            resident(D_FF, d),
            pl.BlockSpec((1, d), lambda i: (0, 0)),
        ],
        out_specs=pl.BlockSpec((tile, d), lambda i: (i, 0)),
        out_shape=jax.ShapeDtypeStruct((n, d), f32),
        compiler_params=pltpu.CompilerParams(
            dimension_semantics=("arbitrary",), vmem_limit_bytes=VMEM_LIMIT_BYTES),
        name="outproj_ffn",
    )(x, mix, lp["w_out"], lp["norm2"], lp["w_gate"], lp["w_up"], lp["w_down"], final_norm)


def _prepare_params(norm1, w_in, mu_shift, conv_w, conv_norm, w0, w_decay_up, a0, a_up, g_up,
                    k_k, k_a, r_k, ln_x_w, ln_x_b, w_out, norm2, w_gate, w_up, w_down,
                    final_norm):
    row = lambda a: a.reshape(1, -1)
    zeros_lora = jnp.zeros((LORA_DECAY, RWKV_DIM), f32)
    w_in_b = w_in.astype(bf16)
    stacked = dict(w_in=w_in_b, w_in_rwkv=w_in_b[:, :, 3 * CONV_DIM:],
                   w_out=w_out.astype(bf16), w_gate=w_gate.astype(bf16),
                   w_up=w_up.astype(bf16), w_down=w_down.astype(bf16))
    layers = []
    for l in range(DEPTH):
        layers.append(dict(
            stacked, layer=l, norm1=row(norm1[l]),
            mu=row(mu_shift[l]), conv_w=conv_w[l], conv_norm=row(conv_norm[l]),
            w0=row(w0[l]),
            wdu=jnp.concatenate([w_decay_up[l], zeros_lora], axis=0).astype(bf16),
            a0=row(a0[l]),
            aup=jnp.concatenate([zeros_lora, a_up[l]], axis=0).astype(bf16),
            gup=g_up[l].astype(bf16),
            k_k=row(k_k[l]), k_a=row(k_a[l]), r_k=row(r_k[l]),
            ln_w=row(ln_x_w[l]), ln_b=row(ln_x_b[l]), norm2=row(norm2[l])))
    return layers, row(final_norm)


def _trunk(x, shift0, conv0, wkv0, prepared):
    layers, final_norm = prepared
    bsz, t_len, d = x.shape
    c_tok = min(t_len, SLAB_ROWS)
    n_seq = SLAB_ROWS // c_tok
    slabs = bsz // n_seq
    assert t_len % c_tok == 0 and bsz % n_seq == 0 and (n_seq == 1 or t_len == c_tok)
    xf = x.reshape(bsz * t_len, d)
    shifts, convs, wkvs = [], [], []
    for l, lp in enumerate(layers):
        p = _inproj(xf, lp["norm1"], lp["w_in"], l, normalize=True)
        shifts.append(_rmsnorm(xf.reshape(bsz, t_len, d)[:, -1], lp["norm1"]))
        p_prev = _inproj(shift0[l], lp["norm1"], lp["w_in_rwkv"], l, normalize=False)
        mix, conv_new, wkv_new = _chunk_mixer(
            p.reshape(slabs, n_seq * t_len, IN_PROJ), p_prev.reshape(slabs, n_seq, RWKV_PROJ),
            conv0[l].reshape(slabs, n_seq, CONV_WIDTH - 1, CONV_DIM),
            wkv0[l].reshape(slabs, n_seq, RWKV_HEADS, HEAD_DIM, HEAD_DIM), lp, c_tok=c_tok)
        xf = _ffn(xf, mix.reshape(bsz * t_len, d), lp, final_norm, final=(l == DEPTH - 1))
        convs.append(conv_new.reshape(bsz, CONV_WIDTH - 1, CONV_DIM))
        wkvs.append(wkv_new.reshape(bsz, RWKV_HEADS, HEAD_DIM, HEAD_DIM))
    return xf.reshape(bsz, t_len, d), jnp.stack(shifts), jnp.stack(convs), jnp.stack(wkvs)


def kernel(x_prompt, x_sample, state_shift, state_conv, state_wkv, norm1, w_in, mu_shift, conv_w, conv_norm, w0, w_decay_up, a0, a_up, g_up, k_k, k_a, r_k, ln_x_w, ln_x_b, w_out, norm2, w_gate, w_up, w_down, final_norm):
    prepared = _prepare_params(norm1, w_in, mu_shift, conv_w, conv_norm, w0, w_decay_up, a0,
                               a_up, g_up, k_k, k_a, r_k, ln_x_w, ln_x_b, w_out, norm2,
                               w_gate, w_up, w_down, final_norm)
    bp = x_prompt.shape[0]
    dt = x_prompt.dtype
    p_shift0 = jnp.zeros((DEPTH, bp, D_MODEL), dt)
    p_conv0 = jnp.zeros((DEPTH, bp, CONV_WIDTH - 1, CONV_DIM), dt)
    p_wkv0 = jnp.zeros((DEPTH, bp, RWKV_HEADS, HEAD_DIM, HEAD_DIM), state_wkv.dtype)
    y_p, sh_p, cv_p, wk_p = _trunk(x_prompt, p_shift0, p_conv0, p_wkv0, prepared)
    y_s, sh_s, cv_s, wk_s = _trunk(x_sample, state_shift, state_conv, state_wkv, prepared)
    return (y_p, y_s, sh_p, cv_p, wk_p, sh_s, cv_s, wk_s)
```

```python
import functools

import jax
import jax.numpy as jnp
from jax import lax
from jax.experimental import pallas as pl
from jax.experimental.pallas import tpu as pltpu

D_MODEL = 1024
DEPTH = 4
CONV_DIM = 512
CONV_WIDTH = 3
RWKV_DIM = 512
HEAD_DIM = 64
RWKV_HEADS = RWKV_DIM // HEAD_DIM
LORA_DECAY = 64
LORA_A = 64
LORA_GATE = 128
RWKV_PROJ = 3 * RWKV_DIM + LORA_DECAY + LORA_A + LORA_GATE
IN_PROJ = 3 * CONV_DIM + RWKV_PROJ
D_FF = 2816
RMS_EPS = 1e-6
GN_EPS = 64e-5

LANES = 128
SUBLANES = 8
HEADS_PER_GROUP = LANES // HEAD_DIM
HEAD_GROUPS = RWKV_HEADS // HEADS_PER_GROUP
VMEM_LIMIT_BYTES = 56 * 1024 * 1024
ROW_TILE = 512
INPROJ_TILE = 1024
SLAB_ROWS = 64
LONG_SLABS = 8
SHORT_SLABS = 4
CHAIN_SLABS = 2

f32 = jnp.float32
bf16 = jnp.bfloat16


def _split_bf16(x, terms):
    parts = []
    rem = x
    for i in range(terms):
        part = rem.astype(bf16)
        parts.append(part)
        if i + 1 < terms:
            rem = rem - part.astype(f32)
    return parts


def _group_sum_wide(x):
    lane = lax.broadcasted_iota(jnp.int32, (x.shape[0], LANES), 1)
    first = lane < HEAD_DIM
    cols = []
    for c in range(x.shape[1] // LANES):
        blk = x[:, c * LANES:(c + 1) * LANES]
        lo = jnp.sum(jnp.where(first, blk, 0.0), axis=1, keepdims=True)
        hi = jnp.sum(jnp.where(first, 0.0, blk), axis=1, keepdims=True)
        cols.append(jnp.where(first, lo, hi))
    return jnp.concatenate(cols, axis=1)


def _sigmoid(x):
    return 1.0 / (1.0 + jnp.exp(-x))


def _rms(x, g):
    return x * lax.rsqrt(jnp.mean(x * x, axis=-1, keepdims=True) + RMS_EPS) * g


def _inproj_kernel(x_ref, g_ref, w_ref, o_ref, *, normalize):
    x = x_ref[...]
    if normalize:
        x = _rms(x, g_ref[...])
    o_ref[...] = jnp.dot(x.astype(bf16), w_ref[...], preferred_element_type=f32)


def _inproj(x, g, w, layer, *, normalize):
    n, d = x.shape
    m = w.shape[2]
    tile = min(INPROJ_TILE, n)
    return pl.pallas_call(
        functools.partial(_inproj_kernel, normalize=normalize),
        grid=(n // tile,),
        in_specs=[
            pl.BlockSpec((tile, d), lambda i: (i, 0)),
            pl.BlockSpec((1, d), lambda i: (0, 0)),
            pl.BlockSpec((None, d, m), lambda i: (layer, 0, 0), pipeline_mode=pl.Buffered(1)),
        ],
        out_specs=pl.BlockSpec((tile, m), lambda i: (i, 0)),
        out_shape=jax.ShapeDtypeStruct((n, m), f32),
        compiler_params=pltpu.CompilerParams(
            dimension_semantics=("arbitrary",), vmem_limit_bytes=VMEM_LIMIT_BYTES),
        name="inproj" if normalize else "shift_proj",
    )(x, g, w)


def _rmsnorm_kernel(x_ref, g_ref, o_ref):
    o_ref[...] = _rms(x_ref[...], g_ref[...])


def _rmsnorm(x, g):
    n, d = x.shape
    tile = min(ROW_TILE, n)
    return pl.pallas_call(
        _rmsnorm_kernel,
        grid=(n // tile,),
        in_specs=[pl.BlockSpec((tile, d), lambda i: (i, 0)),
                  pl.BlockSpec((1, d), lambda i: (0, 0))],
        out_specs=pl.BlockSpec((tile, d), lambda i: (i, 0)),
        out_shape=jax.ShapeDtypeStruct((n, d), f32),
        compiler_params=pltpu.CompilerParams(dimension_semantics=("arbitrary",)),
        name="rmsnorm",
    )(x, g)


def _nt_dot(a, b):
    return lax.dot_general(a, b, (((1,), (1,)), ((), ())), preferred_element_type=f32)


def _tn_dot(a, b):
    return lax.dot_general(a, b, (((0,), (0,)), ((), ())), preferred_element_type=f32)


def _stack_heads(x):
    lane = lax.broadcasted_iota(jnp.int32, x.shape, 1)
    return jnp.concatenate(
        [jnp.where(lane < HEAD_DIM, x, 0.0), jnp.where(lane >= HEAD_DIM, x, 0.0)], axis=0)


def _drain(gen):
    for _ in gen:
        pass


def _interleave(main, sides):
    for _ in main:
        for side in sides:
            next(side, None)
    for side in sides:
        _drain(side)


def _chunk_mixer_kernel(
        p_ref, pprev_ref, conv0_ref, wkv0_ref, tri_ref,
        mu_ref, convw_ref, convn_ref, w0_ref, wdu_ref, a0_ref, aup_ref, gup_ref,
        kk_ref, ka_ref, rk_ref, lnw_ref, lnb_ref,
        mix_ref, convout_ref, wkvout_ref,
        s_ref, cpr_ref, cu_ref,
        *, n_slab, c_tok, single_step):
    t_idx = pl.program_id(1)
    n_t = pl.num_programs(1)
    rows = SLAB_ROWS
    n_seq = rows // c_tok
    c2 = 2 * rows

    def load_packed(ref, q, b, j):
        return jnp.concatenate(
            [ref[q, b, HEADS_PER_GROUP * j + h] for h in range(HEADS_PER_GROUP)], axis=1)

    def store_unpacked(ref, q, b, j, s):
        for h in range(HEADS_PER_GROUP):
            ref[q, b, HEADS_PER_GROUP * j + h] = s[:, h * HEAD_DIM:(h + 1) * HEAD_DIM]

    def load_state(q, b, j):
        return load_packed(wkv0_ref, q, b, j) if single_step else s_ref[q, b, j]

    def store_state(q, b, j, s):
        if single_step:
            store_unpacked(wkvout_ref, q, b, j, s)
        else:
            s_ref[q, b, j] = s

    def for_each_state(body):
        def run(i, carry):
            for j in range(HEAD_GROUPS):
                body(i // n_seq, i % n_seq, j)
            return carry
        lax.fori_loop(0, n_slab * n_seq, run, 0)

    @pl.when(t_idx == 0)
    def _():
        if not single_step:
            def pack(q, b, j):
                s_ref[q, b, j] = load_packed(wkv0_ref, q, b, j)
            for_each_state(pack)
        cpr_ref[...] = pprev_ref[...]
        cu_ref[...] = conv0_ref[...]

    t_pos = lax.broadcasted_iota(jnp.int32, (rows, 1), 0) % c_tok
    ri = lax.broadcasted_iota(jnp.int32, (c2, c2), 0) % rows
    ci = lax.broadcasted_iota(jnp.int32, (c2, c2), 1) % rows
    same_seq = (ri // c_tok) == (ci // c_tok)
    strict = same_seq & (ci % c_tok < ri % c_tok)
    incl = same_seq & (ci % c_tok <= ri % c_tok)
    incl_wide = jnp.concatenate([incl, incl], axis=1)
    eye = (lax.broadcasted_iota(jnp.int32, (c2, c2), 0)
           == lax.broadcasted_iota(jnp.int32, (c2, c2), 1)).astype(f32)
    groups = range(HEAD_GROUPS)
    seqs = range(n_seq)
    mm = functools.partial(jnp.dot, preferred_element_type=f32)

    def per_seq_rows(load_row):
        return jnp.concatenate(
            [jnp.broadcast_to(load_row(b), (c_tok, load_row(b).shape[1])) for b in seqs], axis=0)

    def shift_rows(x, fills):
        out = pltpu.roll(x, len(fills), axis=0)
        for i, fill in enumerate(fills):
            out = jnp.where(t_pos == i, fill, out)
        return out

    def last_rows(x, back):
        return [x[b * c_tok + c_tok - back:b * c_tok + c_tok - back + 1, :] for b in seqs]

    def per_token(q, ops):
        u = p_ref[q, :, 2 * CONV_DIM:3 * CONV_DIM] * p_ref[q, :, 0:CONV_DIM]
        um2 = per_seq_rows(lambda b: cu_ref[q, b, 0:1, :])
        um1 = per_seq_rows(lambda b: cu_ref[q, b, 1:2, :])
        conv = (shift_rows(u, [um2, um1]) * convw_ref[0:1, :]
                + shift_rows(u, [um1]) * convw_ref[1:2, :] + u * convw_ref[2:3, :])
        for b, (r2, r1) in enumerate(zip(last_rows(u, 2), last_rows(u, 1))):
            cu_ref[q, b, 0:1, :] = r2
            cu_ref[q, b, 1:2, :] = r1
        yc = p_ref[q, :, CONV_DIM:2 * CONV_DIM] * conv
        mix_ref[q, :, 0:CONV_DIM] = _rms(yc, convn_ref[...])

        pr = p_ref[q, :, 3 * CONV_DIM:IN_PROJ]
        shifted = shift_rows(pr, [per_seq_rows(lambda b: cpr_ref[q, b:b + 1, :])])
        for b, r1 in enumerate(last_rows(pr, 1)):
            cpr_ref[q, b:b + 1, :] = r1
        m = pr + (shifted - pr) * mu_ref[...]
        o1, o2, o3 = RWKV_DIM, 2 * RWKV_DIM, 3 * RWKV_DIM
        r = m[:, :o1]
        k = m[:, o1:o2]
        v = m[:, o2:o3]
        wa_d = m[:, o3:o3 + LORA_DECAY + LORA_A]
        g_d = m[:, o3 + LORA_DECAY + LORA_A:]
        z = w0_ref[...] + mm(jnp.tanh(wa_d).astype(bf16), wdu_ref[...])
        a_lin = a0_ref[...] + mm(wa_d.astype(bf16), aup_ref[...])
        ops["g"] = mm(_sigmoid(g_d).astype(bf16), gup_ref[...])
        yield
        nz = -z
        softplus = jnp.maximum(nz, 0.0) + jnp.log1p(jnp.exp(-jnp.abs(nz)))
        log_d = -jnp.exp(-softplus - 0.5)
        a = _sigmoid(a_lin)
        kk = k * kk_ref[...]
        kk = kk / jnp.maximum(jnp.sqrt(_group_sum_wide(kk * kk)), 1e-12)
        k = k * (1.0 + (a - 1.0) * ka_ref[...])
        yield
        tri = tri_ref[...]
        cum = None
        for part in _split_bf16(log_d, 3):
            term = mm(tri, part)
            cum = term if cum is None else cum + term
        ends = last_rows(cum, 1)
        total = per_seq_rows(lambda b: ends[b])
        yield
        ops["bonus"] = _group_sum_wide(r * k * rk_ref[...]) * v
        p_inv = jnp.exp(-cum)
        p_rest = jnp.exp(total - cum)
        beta = kk * a
        ops.update(at=-kk * jnp.exp(cum - log_d), rt=r * jnp.exp(cum), bt=beta * p_inv,
                   kt=k * p_inv, bc=beta * p_rest, kc=k * p_rest, v=v, p_total=jnp.exp(total))
        yield

    def seq_rows(x2, b):
        return [x2[h * rows + b * c_tok:h * rows + (b + 1) * c_tok] for h in range(HEADS_PER_GROUP)]

    def chains(slabs):
        units = [(q, j) for q in slabs for j in groups]
        idx = range(len(units))

        def stack(name, n):
            q, j = units[n]
            return _stack_heads(ops[q][name][:, j * LANES:(j + 1) * LANES])

        at2 = [stack("at", n) for n in idx]
        rt2 = [stack("rt", n) for n in idx]
        v2f = [stack("v", n) for n in idx]
        v2 = [x.astype(bf16) for x in v2f]
        ar = [jnp.concatenate([at2[n], rt2[n]], axis=0).astype(bf16) for n in idx]
        bk = [jnp.concatenate([stack("bt", n), stack("kt", n)], axis=0).astype(bf16) for n in idx]
        s_pack = [[load_state(q, b, j) for b in seqs] for q, j in units]
        scores = [_nt_dot(ar[n], bk[n]) for n in idx]
        yield
        from_a, from_r = [], []
        for n in idx:
            per_seq = [
                _nt_dot(jnp.concatenate(seq_rows(at2[n], b) + seq_rows(rt2[n], b),
                                        axis=0).astype(bf16),
                        _stack_heads(s_pack[n][b]).astype(bf16)) for b in seqs]
            pieces = lambda part: jnp.concatenate(
                [per_seq[b][(part * HEADS_PER_GROUP + h) * c_tok:
                            (part * HEADS_PER_GROUP + h + 1) * c_tok]
                 for h in range(HEADS_PER_GROUP) for b in seqs], axis=0)
            from_a.append(pieces(0))
            from_r.append(pieces(1))
        yield
        l_ab = [jnp.where(strict, scores[n][:c2, :c2], 0.0) for n in idx]
        l_b = [l_ab[n].astype(bf16) for n in idx]
        power = [mm(l_b[n], l_b[n]) for n in idx]
        yield
        w2 = [from_a[n]
              + mm(jnp.where(strict, scores[n][:c2, c2:], 0.0).astype(bf16), v2[n]) for n in idx]
        yield
        inv = [eye + l_ab[n] for n in idx]
        span = 2
        while span < c_tok:
            if 2 * span < c_tok:
                both = [mm(jnp.concatenate([inv[n], power[n]], axis=0).astype(bf16),
                           power[n].astype(bf16)) for n in idx]
                inv = [inv[n] + both[n][:c2] for n in idx]
                power = [both[n][c2:] for n in idx]
            else:
                inv = [inv[n] + mm(inv[n].astype(bf16), power[n].astype(bf16)) for n in idx]
            span *= 2
            yield
        u2f = [mm(inv[n].astype(bf16), w2[n].astype(bf16)) for n in idx]
        yield
        uv = [jnp.concatenate([u2f[n].astype(bf16), v2[n]], axis=0) for n in idx]
        y2 = [from_r[n]
              + mm(jnp.where(incl_wide, scores[n][c2:], 0.0).astype(bf16), uv[n]) for n in idx]
        for q in slabs:
            ops[q]["y"] = jnp.concatenate(
                [y2[n][:rows] + y2[n][rows:] for n in idx if units[n][0] == q], axis=1)
        yield
        for n, (q, j) in enumerate(units):
            bc2, kc2 = stack("bc", n), stack("kc", n)
            for b in seqs:
                lhs = jnp.concatenate(seq_rows(u2f[n], b) + seq_rows(v2f[n], b), axis=0)
                rhs = jnp.concatenate(seq_rows(bc2, b) + seq_rows(kc2, b), axis=0)
                s_new = _tn_dot(lhs.astype(bf16), rhs.astype(bf16))
                decay = ops[q]["p_total"][b * c_tok:b * c_tok + 1, j * LANES:(j + 1) * LANES]
                store_state(q, b, j,
                            s_pack[n][b] * decay + s_new[:HEAD_DIM] + s_new[HEAD_DIM:])
        yield

    def norm_gate(q, ops):
        y = ops["y"]
        mean = _group_sum_wide(y) * (1.0 / HEAD_DIM)
        yield
        ycen = y - mean
        var = _group_sum_wide(ycen * ycen) * (1.0 / HEAD_DIM)
        yield
        yn = ycen * lax.rsqrt(var + GN_EPS) * lnw_ref[...] + lnb_ref[...]
        mix_ref[q, :, CONV_DIM:CONV_DIM + RWKV_DIM] = (yn + ops["bonus"]) * ops["g"]
        yield

    ops = [dict() for _ in range(n_slab)]
    waves = [range(w, min(w + CHAIN_SLABS, n_slab)) for w in range(0, n_slab, CHAIN_SLABS)]
    for q in waves[0]:
        _drain(per_token(q, ops[q]))
    for w, wave in enumerate(waves):
        sides = [per_token(q, ops[q]) for q in (waves[w + 1] if w + 1 < len(waves) else ())]
        sides += [norm_gate(q, ops[q]) for q in (waves[w - 1] if w > 0 else ())]
        _interleave(chains(wave), sides)
    for q in waves[-1]:
        _drain(norm_gate(q, ops[q]))

    @pl.when(t_idx == n_t - 1)
    def _():
        convout_ref[...] = cu_ref[...]
        if not single_step:
            def unpack(q, b, j):
                store_unpacked(wkvout_ref, q, b, j, s_ref[q, b, j])
            for_each_state(unpack)


def _chunk_mixer(p, p_prev, conv0, wkv0, lp, *, c_tok):
    n_total, t_rows, _ = p.shape
    rows = SLAB_ROWS
    n_seq = rows // c_tok
    n_slab = min(LONG_SLABS if n_seq == 1 else SHORT_SLABS, n_total)
    assert c_tok & (c_tok - 1) == 0 and c_tok >= CONV_WIDTH - 1
    assert t_rows % rows == 0 and n_total % n_slab == 0 and (n_seq == 1 or t_rows == rows)
    r = jnp.arange(rows)
    same = (r[:, None] // c_tok) == (r[None, :] // c_tok)
    tri = (same & (r[None, :] <= r[:, None])).astype(bf16)
    const = lambda shape: pl.BlockSpec(shape, lambda i, t: (0,) * len(shape))
    state = lambda shape, **kw: pl.BlockSpec(
        (n_slab,) + shape, lambda i, t: (i,) + (0,) * len(shape), **kw)
    single_step = t_rows == rows
    wkv_spec = state((n_seq, RWKV_HEADS, HEAD_DIM, HEAD_DIM))
    in_specs = [
        pl.BlockSpec((n_slab, rows, IN_PROJ), lambda i, t: (i, t, 0)),
        state((n_seq, RWKV_PROJ)),
        state((n_seq, CONV_WIDTH - 1, CONV_DIM)),
        wkv_spec,
        const((rows, rows)),
        const((1, RWKV_PROJ)),
        const((CONV_WIDTH, CONV_DIM)),
        const((1, CONV_DIM)),
        const((1, RWKV_DIM)),
        const((LORA_DECAY + LORA_A, RWKV_DIM)),
        const((1, RWKV_DIM)),
        const((LORA_DECAY + LORA_A, RWKV_DIM)),
        const((LORA_GATE, RWKV_DIM)),
        const((1, RWKV_DIM)),
        const((1, RWKV_DIM)),
        const((1, RWKV_DIM)),
        const((1, RWKV_DIM)),
        const((1, RWKV_DIM)),
    ]
    out_specs = [
        pl.BlockSpec((n_slab, rows, D_MODEL), lambda i, t: (i, t, 0)),
        state((n_seq, CONV_WIDTH - 1, CONV_DIM)),
        wkv_spec,
    ]
    out_shape = [
        jax.ShapeDtypeStruct((n_total, t_rows, D_MODEL), f32),
        jax.ShapeDtypeStruct(conv0.shape, f32),
        jax.ShapeDtypeStruct(wkv0.shape, f32),
    ]
    scratch = [
        pltpu.VMEM((1, 1, 1, SUBLANES, LANES) if single_step
                   else (n_slab, n_seq, HEAD_GROUPS, HEAD_DIM, LANES), f32),
        pltpu.VMEM((n_slab, n_seq, RWKV_PROJ), f32),
        pltpu.VMEM((n_slab, n_seq, CONV_WIDTH - 1, CONV_DIM), f32),
    ]
    return pl.pallas_call(
        functools.partial(_chunk_mixer_kernel, n_slab=n_slab, c_tok=c_tok,
                          single_step=single_step),
        grid=(n_total // n_slab, t_rows // rows),
        in_specs=in_specs,
        out_specs=out_specs,
        out_shape=out_shape,
        scratch_shapes=scratch,
        compiler_params=pltpu.CompilerParams(
            dimension_semantics=("arbitrary", "arbitrary"),
            vmem_limit_bytes=VMEM_LIMIT_BYTES),
        name="chunk_mixer",
    )(p, p_prev, conv0, wkv0, tri,
      lp["mu"], lp["conv_w"], lp["conv_norm"], lp["w0"], lp["wdu"], lp["a0"], lp["aup"],
      lp["gup"], lp["k_k"], lp["k_a"], lp["r_k"], lp["ln_w"], lp["ln_b"])


def _ffn_kernel(x_ref, mix_ref, wout_ref, g2_ref, wg_ref, wu_ref, wd_ref, gf_ref, o_ref, *,
                final):
    x = x_ref[...] + jnp.dot(mix_ref[...].astype(bf16), wout_ref[...],
                             preferred_element_type=f32)
    hb = _rms(x, g2_ref[...]).astype(bf16)
    gate = jnp.dot(hb, wg_ref[...], preferred_element_type=f32)
    up = jnp.dot(hb, wu_ref[...], preferred_element_type=f32)
    act = gate * _sigmoid(gate) * up
    out = x + jnp.dot(act.astype(bf16), wd_ref[...], preferred_element_type=f32)
    o_ref[...] = _rms(out, gf_ref[...]) if final else out


def _ffn(x, mix, lp, final_norm, *, final):
    n, d = x.shape
    tile = min(ROW_TILE, n)
    layer = lp["layer"]
    resident = lambda rows, cols: pl.BlockSpec((None, rows, cols), lambda i: (layer, 0, 0),
                                               pipeline_mode=pl.Buffered(1))
    return pl.pallas_call(
        functools.partial(_ffn_kernel, final=final),
        grid=(n // tile,),
        in_specs=[
            pl.BlockSpec((tile, d), lambda i: (i, 0)),
            pl.BlockSpec((tile, d), lambda i: (i, 0)),
            resident(d, d),
            pl.BlockSpec((1, d), lambda i: (0, 0)),
            resident(d, D_FF),
            resident(d, D_FF),
            resident(D_FF, d),
            pl.BlockSpec((1, d), lambda i: (0, 0)),
        ],
        out_specs=pl.BlockSpec((tile, d), lambda i: (i, 0)),
        out_shape=jax.ShapeDtypeStruct((n, d), f32),
        compiler_params=pltpu.CompilerParams(
            dimension_semantics=("arbitrary",), vmem_limit_bytes=VMEM_LIMIT_BYTES),
        name="outproj_ffn",
    )(x, mix, lp["w_out"], lp["norm2"], lp["w_gate"], lp["w_up"], lp["w_down"], final_norm)


def _prepare_params(norm1, w_in, mu_shift, conv_w, conv_norm, w0, w_decay_up, a0, a_up, g_up,
                    k_k, k_a, r_k, ln_x_w, ln_x_b, w_out, norm2, w_gate, w_up, w_down,
                    final_norm):
    row = lambda a: a.reshape(1, -1)
    zeros_lora = jnp.zeros((LORA_DECAY, RWKV_DIM), f32)
    w_in_b = w_in.astype(bf16)
    stacked = dict(w_in=w_in_b, w_in_rwkv=w_in_b[:, :, 3 * CONV_DIM:],
                   w_out=w_out.astype(bf16), w_gate=w_gate.astype(bf16),
                   w_up=w_up.astype(bf16), w_down=w_down.astype(bf16))
    layers = []
    for l in range(DEPTH):
        layers.append(dict(
            stacked, layer=l, norm1=row(norm1[l]),
            mu=row(mu_shift[l]), conv_w=conv_w[l], conv_norm=row(conv_norm[l]),
            w0=row(w0[l]),
            wdu=jnp.concatenate([w_decay_up[l], zeros_lora], axis=0).astype(bf16),
            a0=row(a0[l]),
            aup=jnp.concatenate([zeros_lora, a_up[l]], axis=0).astype(bf16),
            gup=g_up[l].astype(bf16),
            k_k=row(k_k[l]), k_a=row(k_a[l]), r_k=row(r_k[l]),
            ln_w=row(ln_x_w[l]), ln_b=row(ln_x_b[l]), norm2=row(norm2[l])))
    return layers, row(final_norm)


def _trunk(x, shift0, conv0, wkv0, prepared):
    layers, final_norm = prepared
    bsz, t_len, d = x.shape
    c_tok = min(t_len, SLAB_ROWS)
    n_seq = SLAB_ROWS // c_tok
    slabs = bsz // n_seq
    assert t_len % c_tok == 0 and bsz % n_seq == 0 and (n_seq == 1 or t_len == c_tok)
    xf = x.reshape(bsz * t_len, d)
    shifts, convs, wkvs = [], [], []
    for l, lp in enumerate(layers):
        p = _inproj(xf, lp["norm1"], lp["w_in"], l, normalize=True)
        shifts.append(_rmsnorm(xf.reshape(bsz, t_len, d)[:, -1], lp["norm1"]))
        p_prev = _inproj(shift0[l], lp["norm1"], lp["w_in_rwkv"], l, normalize=False)
        mix, conv_new, wkv_new = _chunk_mixer(
            p.reshape(slabs, n_seq * t_len, IN_PROJ), p_prev.reshape(slabs, n_seq, RWKV_PROJ),
            conv0[l].reshape(slabs, n_seq, CONV_WIDTH - 1, CONV_DIM),
            wkv0[l].reshape(slabs, n_seq, RWKV_HEADS, HEAD_DIM, HEAD_DIM), lp, c_tok=c_tok)
        xf = _ffn(xf, mix.reshape(bsz * t_len, d), lp, final_norm, final=(l == DEPTH - 1))
        convs.append(conv_new.reshape(bsz, CONV_WIDTH - 1, CONV_DIM))
        wkvs.append(wkv_new.reshape(bsz, RWKV_HEADS, HEAD_DIM, HEAD_DIM))
    return xf.reshape(bsz, t_len, d), jnp.stack(shifts), jnp.stack(convs), jnp.stack(wkvs)


def kernel(x_prompt, x_sample, state_shift, state_conv, state_wkv, norm1, w_in, mu_shift, conv_w, conv_norm, w0, w_decay_up, a0, a_up, g_up, k_k, k_a, r_k, ln_x_w, ln_x_b, w_out, norm2, w_gate, w_up, w_down, final_norm):
    prepared = _prepare_params(norm1, w_in, mu_shift, conv_w, conv_norm, w0, w_decay_up, a0,
                               a_up, g_up, k_k, k_a, r_k, ln_x_w, ln_x_b, w_out, norm2,
                               w_gate, w_up, w_down, final_norm)
    bp = x_prompt.shape[0]
    dt = x_prompt.dtype
    p_shift0 = jnp.zeros((DEPTH, bp, D_MODEL), dt)
    p_conv0 = jnp.zeros((DEPTH, bp, CONV_WIDTH - 1, CONV_DIM), dt)
    p_wkv0 = jnp.zeros((DEPTH, bp, RWKV_HEADS, HEAD_DIM, HEAD_DIM), state_wkv.dtype)
    y_p, sh_p, cv_p, wk_p = _trunk(x_prompt, p_shift0, p_conv0, p_wkv0, prepared)
    y_s, sh_s, cv_s, wk_s = _trunk(x_sample, state_shift, state_conv, state_wkv, prepared)
    return (y_p, y_s, sh_p, cv_p, wk_p, sh_s, cv_s, wk_s)
```

```python
import functools

import jax
import jax.numpy as jnp
from jax import lax
from jax.experimental import pallas as pl
from jax.experimental.pallas import tpu as pltpu

D_MODEL = 1024
DEPTH = 4
CONV_DIM = 512
CONV_WIDTH = 3
RWKV_DIM = 512
HEAD_DIM = 64
RWKV_HEADS = RWKV_DIM // HEAD_DIM
LORA_DECAY = 64
LORA_A = 64
LORA_GATE = 128
RWKV_PROJ = 3 * RWKV_DIM + LORA_DECAY + LORA_A + LORA_GATE
IN_PROJ = 3 * CONV_DIM + RWKV_PROJ
D_FF = 2816
RMS_EPS = 1e-6
GN_EPS = 64e-5

LANES = 128
SUBLANES = 8
HEADS_PER_GROUP = LANES // HEAD_DIM
HEAD_GROUPS = RWKV_HEADS // HEADS_PER_GROUP
VMEM_LIMIT_BYTES = 56 * 1024 * 1024
ROW_TILE = 512
INPROJ_TILE = 1024
INPROJ_SUB_ROWS = 256
SLAB_ROWS = 64
LONG_SLABS = 8
SHORT_SLABS = 4
CHAIN_SLABS = 2

f32 = jnp.float32
bf16 = jnp.bfloat16


def _group_sum_wide(x):
    lane = lax.broadcasted_iota(jnp.int32, (x.shape[0], LANES), 1)
    first = lane < HEAD_DIM
    cols = []
    for c in range(x.shape[1] // LANES):
        blk = x[:, c * LANES:(c + 1) * LANES]
        lo = jnp.sum(jnp.where(first, blk, 0.0), axis=1, keepdims=True)
        hi = jnp.sum(jnp.where(first, 0.0, blk), axis=1, keepdims=True)
        cols.append(jnp.where(first, lo, hi))
    return jnp.concatenate(cols, axis=1)


def _sigmoid(x):
    return 1.0 / (1.0 + jnp.exp(-x))


def _rms(x, g):
    return x * lax.rsqrt(jnp.mean(x * x, axis=-1, keepdims=True) + RMS_EPS) * g


def _inproj_kernel(x_ref, g_ref, w_ref, o_ref, *, normalize):
    n_sub = max(1, x_ref.shape[0] // INPROJ_SUB_ROWS)
    sub = x_ref.shape[0] // n_sub
    for i in range(n_sub):
        x = x_ref[i * sub:(i + 1) * sub, :]
        if normalize:
            x = _rms(x, g_ref[...])
        o_ref[i * sub:(i + 1) * sub, :] = jnp.dot(x.astype(bf16), w_ref[...],
                                                  preferred_element_type=f32)


def _inproj(x, g, w, layer, *, normalize):
    n, d = x.shape
    m = w.shape[2]
    tile = min(INPROJ_TILE, n)
    return pl.pallas_call(
        functools.partial(_inproj_kernel, normalize=normalize),
        grid=(n // tile,),
        in_specs=[
            pl.BlockSpec((tile, d), lambda i: (i, 0)),
            pl.BlockSpec((1, d), lambda i: (0, 0)),
            pl.BlockSpec((None, d, m), lambda i: (layer, 0, 0), pipeline_mode=pl.Buffered(1)),
        ],
        out_specs=pl.BlockSpec((tile, m), lambda i: (i, 0)),
        out_shape=jax.ShapeDtypeStruct((n, m), f32),
        compiler_params=pltpu.CompilerParams(
            dimension_semantics=("arbitrary",), vmem_limit_bytes=VMEM_LIMIT_BYTES),
        name="inproj" if normalize else "shift_proj",
    )(x, g, w)


def _rmsnorm_kernel(x_ref, g_ref, o_ref):
    o_ref[...] = _rms(x_ref[...], g_ref[...])


def _rmsnorm(x, g):
    n, d = x.shape
    tile = min(ROW_TILE, n)
    return pl.pallas_call(
        _rmsnorm_kernel,
        grid=(n // tile,),
        in_specs=[pl.BlockSpec((tile, d), lambda i: (i, 0)),
                  pl.BlockSpec((1, d), lambda i: (0, 0))],
        out_specs=pl.BlockSpec((tile, d), lambda i: (i, 0)),
        out_shape=jax.ShapeDtypeStruct((n, d), f32),
        compiler_params=pltpu.CompilerParams(dimension_semantics=("arbitrary",)),
        name="rmsnorm",
    )(x, g)


def _nt_dot(a, b):
    return lax.dot_general(a, b, (((1,), (1,)), ((), ())), preferred_element_type=f32)


def _tn_dot(a, b):
    return lax.dot_general(a, b, (((0,), (0,)), ((), ())), preferred_element_type=f32)


def _stack_heads(x):
    lane = lax.broadcasted_iota(jnp.int32, x.shape, 1)
    return jnp.concatenate(
        [jnp.where(lane < HEAD_DIM, x, 0.0), jnp.where(lane >= HEAD_DIM, x, 0.0)], axis=0)


def _drain(gen):
    for _ in gen:
        pass


def _interleave(main, sides):
    for _ in main:
        for side in sides:
            next(side, None)
    for side in sides:
        _drain(side)


def _chunk_mixer_kernel(
        p_ref, pprev_ref, conv0_ref, wkv0_ref,
        mu_ref, convw_ref, convn_ref, w0_ref, wdu_ref, a0_ref, aup_ref, gup_ref,
        kk_ref, ka_ref, rk_ref, lnw_ref, lnb_ref,
        mix_ref, convout_ref, wkvout_ref,
        s_ref, cpr_ref, cu_ref,
        *, n_slab, c_tok, single_step):
    t_idx = pl.program_id(1)
    n_t = pl.num_programs(1)
    rows = SLAB_ROWS
    n_seq = rows // c_tok
    c2 = 2 * rows

    def load_packed(ref, q, b, j):
        return jnp.concatenate(
            [ref[q, b, HEADS_PER_GROUP * j + h] for h in range(HEADS_PER_GROUP)], axis=1)

    def store_unpacked(ref, q, b, j, s):
        for h in range(HEADS_PER_GROUP):
            ref[q, b, HEADS_PER_GROUP * j + h] = s[:, h * HEAD_DIM:(h + 1) * HEAD_DIM]

    def load_state(q, b, j):
        return load_packed(wkv0_ref, q, b, j) if single_step else s_ref[q, b, j]

    def store_state(q, b, j, s):
        if single_step:
            store_unpacked(wkvout_ref, q, b, j, s)
        else:
            s_ref[q, b, j] = s

    def for_each_state(body):
        def run(i, carry):
            for j in range(HEAD_GROUPS):
                body(i // n_seq, i % n_seq, j)
            return carry
        lax.fori_loop(0, n_slab * n_seq, run, 0)

    @pl.when(t_idx == 0)
    def _():
        if not single_step:
            def pack(q, b, j):
                s_ref[q, b, j] = load_packed(wkv0_ref, q, b, j)
            for_each_state(pack)
        cpr_ref[...] = pprev_ref[...]
        cu_ref[...] = conv0_ref[...]

    t_pos = lax.broadcasted_iota(jnp.int32, (rows, 1), 0) % c_tok
    ri = lax.broadcasted_iota(jnp.int32, (c2, c2), 0) % rows
    ci = lax.broadcasted_iota(jnp.int32, (c2, c2), 1) % rows
    same_seq = (ri // c_tok) == (ci // c_tok)
    strict = same_seq & (ci % c_tok < ri % c_tok)
    incl = same_seq & (ci % c_tok <= ri % c_tok)
    incl_wide = jnp.concatenate([incl, incl], axis=1)
    eye = (lax.broadcasted_iota(jnp.int32, (c2, c2), 0)
           == lax.broadcasted_iota(jnp.int32, (c2, c2), 1)).astype(f32)
    groups = range(HEAD_GROUPS)
    seqs = range(n_seq)
    mm = functools.partial(jnp.dot, preferred_element_type=f32)

    def per_seq_rows(load_row):
        return jnp.concatenate(
            [jnp.broadcast_to(load_row(b), (c_tok, load_row(b).shape[1])) for b in seqs], axis=0)

    def shift_rows(x, fills):
        out = pltpu.roll(x, len(fills), axis=0)
        for i, fill in enumerate(fills):
            out = jnp.where(t_pos == i, fill, out)
        return out

    def last_rows(x, back):
        return [x[b * c_tok + c_tok - back:b * c_tok + c_tok - back + 1, :] for b in seqs]

    def per_token(q, ops):
        u = p_ref[q, :, 2 * CONV_DIM:3 * CONV_DIM] * p_ref[q, :, 0:CONV_DIM]
        um2 = per_seq_rows(lambda b: cu_ref[q, b, 0:1, :])
        um1 = per_seq_rows(lambda b: cu_ref[q, b, 1:2, :])
        conv = (shift_rows(u, [um2, um1]) * convw_ref[0:1, :]
                + shift_rows(u, [um1]) * convw_ref[1:2, :] + u * convw_ref[2:3, :])
        for b, (r2, r1) in enumerate(zip(last_rows(u, 2), last_rows(u, 1))):
            cu_ref[q, b, 0:1, :] = r2
            cu_ref[q, b, 1:2, :] = r1
        yc = p_ref[q, :, CONV_DIM:2 * CONV_DIM] * conv
        mix_ref[q, :, 0:CONV_DIM] = _rms(yc, convn_ref[...])

        pr = p_ref[q, :, 3 * CONV_DIM:IN_PROJ]
        shifted = shift_rows(pr, [per_seq_rows(lambda b: cpr_ref[q, b:b + 1, :])])
        for b, r1 in enumerate(last_rows(pr, 1)):
            cpr_ref[q, b:b + 1, :] = r1
        m = pr + (shifted - pr) * mu_ref[...]
        o1, o2, o3 = RWKV_DIM, 2 * RWKV_DIM, 3 * RWKV_DIM
        r = m[:, :o1]
        k = m[:, o1:o2]
        v = m[:, o2:o3]
        wa_d = m[:, o3:o3 + LORA_DECAY + LORA_A]
        g_d = m[:, o3 + LORA_DECAY + LORA_A:]
        z = w0_ref[...] + mm(jnp.tanh(wa_d).astype(bf16), wdu_ref[...])
        a_lin = a0_ref[...] + mm(wa_d.astype(bf16), aup_ref[...])
        ops["g"] = mm(_sigmoid(g_d).astype(bf16), gup_ref[...])
        yield
        nz = -z
        softplus = jnp.maximum(nz, 0.0) + jnp.log1p(jnp.exp(-jnp.abs(nz)))
        log_d = -jnp.exp(-softplus - 0.5)
        a = _sigmoid(a_lin)
        kk = k * kk_ref[...]
        kk = kk / jnp.maximum(jnp.sqrt(_group_sum_wide(kk * kk)), 1e-12)
        k = k * (1.0 + (a - 1.0) * ka_ref[...])
        yield
        cum = log_d
        step = 1
        while step < c_tok:
            cum = cum + jnp.where(t_pos >= step, pltpu.roll(cum, step, axis=0), 0.0)
            step *= 2
        ends = last_rows(cum, 1)
        total = per_seq_rows(lambda b: ends[b])
        yield
        ops["bonus"] = _group_sum_wide(r * k * rk_ref[...]) * v
        p_inv = jnp.exp(-cum)
        p_rest = jnp.exp(total - cum)
        beta = kk * a
        ops.update(at=-kk * jnp.exp(cum - log_d), rt=r * jnp.exp(cum), bt=beta * p_inv,
                   kt=k * p_inv, bc=beta * p_rest, kc=k * p_rest, v=v, p_total=jnp.exp(total))
        yield

    def seq_rows(x2, b):
        return [x2[h * rows + b * c_tok:h * rows + (b + 1) * c_tok] for h in range(HEADS_PER_GROUP)]

    def chains(slabs):
        units = [(q, j) for q in slabs for j in groups]
        idx = range(len(units))

        def stack(name, n):
            q, j = units[n]
            return _stack_heads(ops[q][name][:, j * LANES:(j + 1) * LANES])

        at2 = [stack("at", n) for n in idx]
        rt2 = [stack("rt", n) for n in idx]
        v2f = [stack("v", n) for n in idx]
        v2 = [x.astype(bf16) for x in v2f]
        ar = [jnp.concatenate([at2[n], rt2[n]], axis=0).astype(bf16) for n in idx]
        bk = [jnp.concatenate([stack("bt", n), stack("kt", n)], axis=0).astype(bf16) for n in idx]
        s_pack = [[load_state(q, b, j) for b in seqs] for q, j in units]
        scores = [_nt_dot(ar[n], bk[n]) for n in idx]
        yield
        from_a, from_r = [], []
        for n in idx:
            per_seq = [
                _nt_dot(jnp.concatenate(seq_rows(at2[n], b) + seq_rows(rt2[n], b),
                                        axis=0).astype(bf16),
                        _stack_heads(s_pack[n][b]).astype(bf16)) for b in seqs]
            pieces = lambda part: jnp.concatenate(
                [per_seq[b][(part * HEADS_PER_GROUP + h) * c_tok:
                            (part * HEADS_PER_GROUP + h + 1) * c_tok]
                 for h in range(HEADS_PER_GROUP) for b in seqs], axis=0)
            from_a.append(pieces(0))
            from_r.append(pieces(1))
        yield
        l_ab = [jnp.where(strict, scores[n][:c2, :c2], 0.0) for n in idx]
        l_b = [l_ab[n].astype(bf16) for n in idx]
        power = [mm(l_b[n], l_b[n]) for n in idx]
        yield
        w2 = [from_a[n]
              + mm(jnp.where(strict, scores[n][:c2, c2:], 0.0).astype(bf16), v2[n]) for n in idx]
        yield
        inv = [eye + l_ab[n] for n in idx]
        span = 2
        while span < c_tok:
            if 2 * span < c_tok:
                both = [mm(jnp.concatenate([inv[n], power[n]], axis=0).astype(bf16),
                           power[n].astype(bf16)) for n in idx]
                inv = [inv[n] + both[n][:c2] for n in idx]
                power = [both[n][c2:] for n in idx]
            else:
                inv = [inv[n] + mm(inv[n].astype(bf16), power[n].astype(bf16)) for n in idx]
            span *= 2
            yield
        u2f = [mm(inv[n].astype(bf16), w2[n].astype(bf16)) for n in idx]
        yield
        uv = [jnp.concatenate([u2f[n].astype(bf16), v2[n]], axis=0) for n in idx]
        y2 = [from_r[n]
              + mm(jnp.where(incl_wide, scores[n][c2:], 0.0).astype(bf16), uv[n]) for n in idx]
        for q in slabs:
            ops[q]["y"] = jnp.concatenate(
                [y2[n][:rows] + y2[n][rows:] for n in idx if units[n][0] == q], axis=1)
        yield
        for n, (q, j) in enumerate(units):
            bc2, kc2 = stack("bc", n), stack("kc", n)
            for b in seqs:
                lhs = jnp.concatenate(seq_rows(u2f[n], b) + seq_rows(v2f[n], b), axis=0)
                rhs = jnp.concatenate(seq_rows(bc2, b) + seq_rows(kc2, b), axis=0)
                s_new = _tn_dot(lhs.astype(bf16), rhs.astype(bf16))
                decay = ops[q]["p_total"][b * c_tok:b * c_tok + 1, j * LANES:(j + 1) * LANES]
                store_state(q, b, j,
                            s_pack[n][b] * decay + s_new[:HEAD_DIM] + s_new[HEAD_DIM:])
        yield

    def norm_gate(q, ops):
        y = ops["y"]
        mean = _group_sum_wide(y) * (1.0 / HEAD_DIM)
        yield
        ycen = y - mean
        var = _group_sum_wide(ycen * ycen) * (1.0 / HEAD_DIM)
        yield
        yn = ycen * lax.rsqrt(var + GN_EPS) * lnw_ref[...] + lnb_ref[...]
        mix_ref[q, :, CONV_DIM:CONV_DIM + RWKV_DIM] = (yn + ops["bonus"]) * ops["g"]
        yield

    ops = [dict() for _ in range(n_slab)]
    waves = [range(w, min(w + CHAIN_SLABS, n_slab)) for w in range(0, n_slab, CHAIN_SLABS)]
    for q in waves[0]:
        _drain(per_token(q, ops[q]))
    for w, wave in enumerate(waves):
        sides = [per_token(q, ops[q]) for q in (waves[w + 1] if w + 1 < len(waves) else ())]
        sides += [norm_gate(q, ops[q]) for q in (waves[w - 1] if w > 0 else ())]
        _interleave(chains(wave), sides)
    for q in waves[-1]:
        _drain(norm_gate(q, ops[q]))

    @pl.when(t_idx == n_t - 1)
    def _():
        convout_ref[...] = cu_ref[...]
        if not single_step:
            def unpack(q, b, j):
                store_unpacked(wkvout_ref, q, b, j, s_ref[q, b, j])
            for_each_state(unpack)


def _chunk_mixer(p, p_prev, conv0, wkv0, lp, *, c_tok):
    n_total, t_rows, _ = p.shape
    rows = SLAB_ROWS
    n_seq = rows // c_tok
    n_slab = min(LONG_SLABS if n_seq == 1 else SHORT_SLABS, n_total)
    assert c_tok & (c_tok - 1) == 0 and c_tok >= CONV_WIDTH - 1
    assert t_rows % rows == 0 and n_total % n_slab == 0 and (n_seq == 1 or t_rows == rows)
    const = lambda shape: pl.BlockSpec(shape, lambda i, t: (0,) * len(shape))
    state = lambda shape, **kw: pl.BlockSpec(
        (n_slab,) + shape, lambda i, t: (i,) + (0,) * len(shape), **kw)
    single_step = t_rows == rows
    wkv_spec = state((n_seq, RWKV_HEADS, HEAD_DIM, HEAD_DIM))
    in_specs = [
        pl.BlockSpec((n_slab, rows, IN_PROJ), lambda i, t: (i, t, 0)),
        state((n_seq, RWKV_PROJ)),
        state((n_seq, CONV_WIDTH - 1, CONV_DIM)),
        wkv_spec,
        const((1, RWKV_PROJ)),
        const((CONV_WIDTH, CONV_DIM)),
        const((1, CONV_DIM)),
        const((1, RWKV_DIM)),
        const((LORA_DECAY + LORA_A, RWKV_DIM)),
        const((1, RWKV_DIM)),
        const((LORA_DECAY + LORA_A, RWKV_DIM)),
        const((LORA_GATE, RWKV_DIM)),
        const((1, RWKV_DIM)),
        const((1, RWKV_DIM)),
        const((1, RWKV_DIM)),
        const((1, RWKV_DIM)),
        const((1, RWKV_DIM)),
    ]
    out_specs = [
        pl.BlockSpec((n_slab, rows, D_MODEL), lambda i, t: (i, t, 0)),
        state((n_seq, CONV_WIDTH - 1, CONV_DIM)),
        wkv_spec,
    ]
    out_shape = [
        jax.ShapeDtypeStruct((n_total, t_rows, D_MODEL), f32),
        jax.ShapeDtypeStruct(conv0.shape, f32),
        jax.ShapeDtypeStruct(wkv0.shape, f32),
    ]
    scratch = [
        pltpu.VMEM((1, 1, 1, SUBLANES, LANES) if single_step
                   else (n_slab, n_seq, HEAD_GROUPS, HEAD_DIM, LANES), f32),
        pltpu.VMEM((n_slab, n_seq, RWKV_PROJ), f32),
        pltpu.VMEM((n_slab, n_seq, CONV_WIDTH - 1, CONV_DIM), f32),
    ]
    return pl.pallas_call(
        functools.partial(_chunk_mixer_kernel, n_slab=n_slab, c_tok=c_tok,
                          single_step=single_step),
        grid=(n_total // n_slab, t_rows // rows),
        in_specs=in_specs,
        out_specs=out_specs,
        out_shape=out_shape,
        scratch_shapes=scratch,
        compiler_params=pltpu.CompilerParams(
            dimension_semantics=("arbitrary", "arbitrary"),
            vmem_limit_bytes=VMEM_LIMIT_BYTES),
        name="chunk_mixer",
    )(p, p_prev, conv0, wkv0,
      lp["mu"], lp["conv_w"], lp["conv_norm"], lp["w0"], lp["wdu"], lp["a0"], lp["aup"],
      lp["gup"], lp["k_k"], lp["k_a"], lp["r_k"], lp["ln_w"], lp["ln_b"])


def _ffn_kernel(x_ref, mix_ref, wout_ref, g2_ref, wg_ref, wu_ref, wd_ref, gf_ref, o_ref, *,
                final):
    x = x_ref[...] + jnp.dot(mix_ref[...].astype(bf16), wout_ref[...],
                             preferred_element_type=f32)
    hb = _rms(x, g2_ref[...]).astype(bf16)
    gate = jnp.dot(hb, wg_ref[...], preferred_element_type=f32)
    up = jnp.dot(hb, wu_ref[...], preferred_element_type=f32)
    act = gate * _sigmoid(gate) * up
    out = x + jnp.dot(act.astype(bf16), wd_ref[...], preferred_element_type=f32)
    o_ref[...] = _rms(out, gf_ref[...]) if final else out


def _ffn(x, mix, lp, final_norm, *, final):
    n, d = x.shape
    tile = min(ROW_TILE, n)
    layer = lp["layer"]
    resident = lambda rows, cols: pl.BlockSpec((None, rows, cols), lambda i: (layer, 0, 0),
                                               pipeline_mode=pl.Buffered(1))
    return pl.pallas_call(
        functools.partial(_ffn_kernel, final=final),
        grid=(n // tile,),
        in_specs=[
            pl.BlockSpec((tile, d), lambda i: (i, 0)),
            pl.BlockSpec((tile, d), lambda i: (i, 0)),
            resident(d, d),
            pl.BlockSpec((1, d), lambda i: (0, 0)),
            resident(d, D_FF),
            resident(d, D_FF),
            resident(D_FF, d),
            pl.BlockSpec((1, d), lambda i: (0, 0)),
        ],
        out_specs=pl.BlockSpec((tile, d), lambda i: (i, 0)),
        out_shape=jax.ShapeDtypeStruct((n, d), f32),
        compiler_params=pltpu.CompilerParams(
            dimension_semantics=("arbitrary",), vmem_limit_bytes=VMEM_LIMIT_BYTES),
        name="outproj_ffn",
    )(x, mix, lp["w_out"], lp["norm2"], lp["w_gate"], lp["w_up"], lp["w_down"], final_norm)


def _prepare_params(norm1, w_in, mu_shift, conv_w, conv_norm, w0, w_decay_up, a0, a_up, g_up,
                    k_k, k_a, r_k, ln_x_w, ln_x_b, w_out, norm2, w_gate, w_up, w_down,
                    final_norm):
    row = lambda a: a.reshape(1, -1)
    zeros_lora = jnp.zeros((LORA_DECAY, RWKV_DIM), f32)
    w_in_b = w_in.astype(bf16)
    stacked = dict(w_in=w_in_b, w_in_rwkv=w_in_b[:, :, 3 * CONV_DIM:],
                   w_out=w_out.astype(bf16), w_gate=w_gate.astype(bf16),
                   w_up=w_up.astype(bf16), w_down=w_down.astype(bf16))
    layers = []
    for l in range(DEPTH):
        layers.append(dict(
            stacked, layer=l, norm1=row(norm1[l]),
            mu=row(mu_shift[l]), conv_w=conv_w[l], conv_norm=row(conv_norm[l]),
            w0=row(w0[l]),
            wdu=jnp.concatenate([w_decay_up[l], zeros_lora], axis=0).astype(bf16),
            a0=row(a0[l]),
            aup=jnp.concatenate([zeros_lora, a_up[l]], axis=0).astype(bf16),
            gup=g_up[l].astype(bf16),
            k_k=row(k_k[l]), k_a=row(k_a[l]), r_k=row(r_k[l]),
            ln_w=row(ln_x_w[l]), ln_b=row(ln_x_b[l]), norm2=row(norm2[l])))
    return layers, row(final_norm)


def _trunk(x, shift0, conv0, wkv0, prepared):
    layers, final_norm = prepared
    bsz, t_len, d = x.shape
    c_tok = min(t_len, SLAB_ROWS)
    n_seq = SLAB_ROWS // c_tok
    slabs = bsz // n_seq
    assert t_len % c_tok == 0 and bsz % n_seq == 0 and (n_seq == 1 or t_len == c_tok)
    xf = x.reshape(bsz * t_len, d)
    shifts, convs, wkvs = [], [], []
    for l, lp in enumerate(layers):
        p = _inproj(xf, lp["norm1"], lp["w_in"], l, normalize=True)
        shifts.append(_rmsnorm(xf.reshape(bsz, t_len, d)[:, -1], lp["norm1"]))
        p_prev = _inproj(shift0[l], lp["norm1"], lp["w_in_rwkv"], l, normalize=False)
        mix, conv_new, wkv_new = _chunk_mixer(
            p.reshape(slabs, n_seq * t_len, IN_PROJ), p_prev.reshape(slabs, n_seq, RWKV_PROJ),
            conv0[l].reshape(slabs, n_seq, CONV_WIDTH - 1, CONV_DIM),
            wkv0[l].reshape(slabs, n_seq, RWKV_HEADS, HEAD_DIM, HEAD_DIM), lp, c_tok=c_tok)
        xf = _ffn(xf, mix.reshape(bsz * t_len, d), lp, final_norm, final=(l == DEPTH - 1))
        convs.append(conv_new.reshape(bsz, CONV_WIDTH - 1, CONV_DIM))
        wkvs.append(wkv_new.reshape(bsz, RWKV_HEADS, HEAD_DIM, HEAD_DIM))
    return xf.reshape(bsz, t_len, d), jnp.stack(shifts), jnp.stack(convs), jnp.stack(wkvs)


def kernel(x_prompt, x_sample, state_shift, state_conv, state_wkv, norm1, w_in, mu_shift, conv_w, conv_norm, w0, w_decay_up, a0, a_up, g_up, k_k, k_a, r_k, ln_x_w, ln_x_b, w_out, norm2, w_gate, w_up, w_down, final_norm):
    prepared = _prepare_params(norm1, w_in, mu_shift, conv_w, conv_norm, w0, w_decay_up, a0,
                               a_up, g_up, k_k, k_a, r_k, ln_x_w, ln_x_b, w_out, norm2,
                               w_gate, w_up, w_down, final_norm)
    bp = x_prompt.shape[0]
    dt = x_prompt.dtype
    p_shift0 = jnp.zeros((DEPTH, bp, D_MODEL), dt)
    p_conv0 = jnp.zeros((DEPTH, bp, CONV_WIDTH - 1, CONV_DIM), dt)
    p_wkv0 = jnp.zeros((DEPTH, bp, RWKV_HEADS, HEAD_DIM, HEAD_DIM), state_wkv.dtype)
    y_p, sh_p, cv_p, wk_p = _trunk(x_prompt, p_shift0, p_conv0, p_wkv0, prepared)
    y_s, sh_s, cv_s, wk_s = _trunk(x_sample, state_shift, state_conv, state_wkv, prepared)
    return (y_p, y_s, sh_p, cv_p, wk_p, sh_s, cv_s, wk_s)
```

```python
import functools

import jax
import jax.numpy as jnp
from jax import lax
from jax.experimental import pallas as pl
from jax.experimental.pallas import tpu as pltpu

D_MODEL = 1024
DEPTH = 4
CONV_DIM = 512
CONV_WIDTH = 3
RWKV_DIM = 512
HEAD_DIM = 64
RWKV_HEADS = RWKV_DIM // HEAD_DIM
LORA_DECAY = 64
LORA_A = 64
LORA_GATE = 128
RWKV_PROJ = 3 * RWKV_DIM + LORA_DECAY + LORA_A + LORA_GATE
IN_PROJ = 3 * CONV_DIM + RWKV_PROJ
D_FF = 2816
RMS_EPS = 1e-6
GN_EPS = 64e-5

LANES = 128
SUBLANES = 8
HEADS_PER_GROUP = LANES // HEAD_DIM
HEAD_GROUPS = RWKV_HEADS // HEADS_PER_GROUP
VMEM_LIMIT_BYTES = 56 * 1024 * 1024
ROW_TILE = 512
INPROJ_TILE = 1024
INPROJ_SUB_ROWS = 256
SLAB_ROWS = 64
LONG_SLABS = 8
SHORT_SLABS = 4
CHAIN_SLABS = 2

f32 = jnp.float32
bf16 = jnp.bfloat16


def _group_sum_wide(x):
    lane = lax.broadcasted_iota(jnp.int32, (x.shape[0], LANES), 1)
    first = lane < HEAD_DIM
    cols = []
    for c in range(x.shape[1] // LANES):
        blk = x[:, c * LANES:(c + 1) * LANES]
        lo = jnp.sum(jnp.where(first, blk, 0.0), axis=1, keepdims=True)
        hi = jnp.sum(jnp.where(first, 0.0, blk), axis=1, keepdims=True)
        cols.append(jnp.where(first, lo, hi))
    return jnp.concatenate(cols, axis=1)


def _sigmoid(x):
    return 1.0 / (1.0 + jnp.exp(-x))


def _rms(x, g):
    return x * lax.rsqrt(jnp.mean(x * x, axis=-1, keepdims=True) + RMS_EPS) * g


def _inproj_kernel(x_ref, g_ref, w_ref, o_ref, *, normalize):
    n_sub = max(1, x_ref.shape[0] // INPROJ_SUB_ROWS)
    sub = x_ref.shape[0] // n_sub
    for i in range(n_sub):
        x = x_ref[i * sub:(i + 1) * sub, :]
        if normalize:
            x = _rms(x, g_ref[...])
        o_ref[i * sub:(i + 1) * sub, :] = jnp.dot(x.astype(bf16), w_ref[...],
                                                  preferred_element_type=f32)


def _inproj(x, g, w, layer, *, normalize):
    n, d = x.shape
    m = w.shape[2]
    tile = INPROJ_TILE if n >= 2 * INPROJ_TILE else min(ROW_TILE, n)
    return pl.pallas_call(
        functools.partial(_inproj_kernel, normalize=normalize),
        grid=(n // tile,),
        in_specs=[
            pl.BlockSpec((tile, d), lambda i: (i, 0)),
            pl.BlockSpec((1, d), lambda i: (0, 0)),
            pl.BlockSpec((None, d, m), lambda i: (layer, 0, 0), pipeline_mode=pl.Buffered(1)),
        ],
        out_specs=pl.BlockSpec((tile, m), lambda i: (i, 0)),
        out_shape=jax.ShapeDtypeStruct((n, m), f32),
        compiler_params=pltpu.CompilerParams(
            dimension_semantics=("arbitrary",), vmem_limit_bytes=VMEM_LIMIT_BYTES),
        name="inproj" if normalize else "shift_proj",
    )(x, g, w)


def _rmsnorm_kernel(x_ref, g_ref, o_ref):
    o_ref[...] = _rms(x_ref[...], g_ref[...])


def _rmsnorm(x, g):
    n, d = x.shape
    tile = min(ROW_TILE, n)
    return pl.pallas_call(
        _rmsnorm_kernel,
        grid=(n // tile,),
        in_specs=[pl.BlockSpec((tile, d), lambda i: (i, 0)),
                  pl.BlockSpec((1, d), lambda i: (0, 0))],
        out_specs=pl.BlockSpec((tile, d), lambda i: (i, 0)),
        out_shape=jax.ShapeDtypeStruct((n, d), f32),
        compiler_params=pltpu.CompilerParams(dimension_semantics=("arbitrary",)),
        name="rmsnorm",
    )(x, g)


def _nt_dot(a, b):
    return lax.dot_general(a, b, (((1,), (1,)), ((), ())), preferred_element_type=f32)


def _tn_dot(a, b):
    return lax.dot_general(a, b, (((0,), (0,)), ((), ())), preferred_element_type=f32)


def _stack_heads(x):
    lane = lax.broadcasted_iota(jnp.int32, x.shape, 1)
    return jnp.concatenate(
        [jnp.where(lane < HEAD_DIM, x, 0.0), jnp.where(lane >= HEAD_DIM, x, 0.0)], axis=0)


def _drain(gen):
    for _ in gen:
        pass


def _interleave(main, sides):
    for _ in main:
        for side in sides:
            next(side, None)
    for side in sides:
        _drain(side)


def _chunk_mixer_kernel(
        p_ref, pprev_ref, conv0_ref, wkv0_ref,
        mu_ref, convw_ref, convn_ref, w0_ref, wdu_ref, a0_ref, aup_ref, gup_ref,
        kk_ref, ka_ref, rk_ref, lnw_ref, lnb_ref,
        mix_ref, convout_ref, wkvout_ref,
        s_ref, cpr_ref, cu_ref,
        *, n_slab, c_tok, single_step):
    t_idx = pl.program_id(1)
    n_t = pl.num_programs(1)
    rows = SLAB_ROWS
    n_seq = rows // c_tok
    c2 = 2 * rows

    def load_packed(ref, q, b, j):
        return jnp.concatenate(
            [ref[q, b, HEADS_PER_GROUP * j + h] for h in range(HEADS_PER_GROUP)], axis=1)

    def store_unpacked(ref, q, b, j, s):
        for h in range(HEADS_PER_GROUP):
            ref[q, b, HEADS_PER_GROUP * j + h] = s[:, h * HEAD_DIM:(h + 1) * HEAD_DIM]

    def load_state(q, b, j):
        return load_packed(wkv0_ref, q, b, j) if single_step else s_ref[q, b, j]

    def store_state(q, b, j, s):
        if single_step:
            store_unpacked(wkvout_ref, q, b, j, s)
        else:
            s_ref[q, b, j] = s

    def for_each_state(body):
        def run(i, carry):
            for j in range(HEAD_GROUPS):
                body(i // n_seq, i % n_seq, j)
            return carry
        lax.fori_loop(0, n_slab * n_seq, run, 0)

    @pl.when(t_idx == 0)
    def _():
        if not single_step:
            def pack(q, b, j):
                s_ref[q, b, j] = load_packed(wkv0_ref, q, b, j)
            for_each_state(pack)
        cpr_ref[...] = pprev_ref[...]
        cu_ref[...] = conv0_ref[...]

    t_pos = lax.broadcasted_iota(jnp.int32, (rows, 1), 0) % c_tok
    ri = lax.broadcasted_iota(jnp.int32, (c2, c2), 0) % rows
    ci = lax.broadcasted_iota(jnp.int32, (c2, c2), 1) % rows
    same_seq = (ri // c_tok) == (ci // c_tok)
    strict = same_seq & (ci % c_tok < ri % c_tok)
    incl = same_seq & (ci % c_tok <= ri % c_tok)
    incl_wide = jnp.concatenate([incl, incl], axis=1)
    eye = (lax.broadcasted_iota(jnp.int32, (c2, c2), 0)
           == lax.broadcasted_iota(jnp.int32, (c2, c2), 1)).astype(f32)
    groups = range(HEAD_GROUPS)
    seqs = range(n_seq)
    mm = functools.partial(jnp.dot, preferred_element_type=f32)

    def per_seq_rows(load_row):
        return jnp.concatenate(
            [jnp.broadcast_to(load_row(b), (c_tok, load_row(b).shape[1])) for b in seqs], axis=0)

    def shift_rows(x, fills):
        out = pltpu.roll(x, len(fills), axis=0)
        for i, fill in enumerate(fills):
            out = jnp.where(t_pos == i, fill, out)
        return out

    def last_rows(x, back):
        return [x[b * c_tok + c_tok - back:b * c_tok + c_tok - back + 1, :] for b in seqs]

    def per_token(q, ops):
        u = p_ref[q, :, 2 * CONV_DIM:3 * CONV_DIM] * p_ref[q, :, 0:CONV_DIM]
        um2 = per_seq_rows(lambda b: cu_ref[q, b, 0:1, :])
        um1 = per_seq_rows(lambda b: cu_ref[q, b, 1:2, :])
        conv = (shift_rows(u, [um2, um1]) * convw_ref[0:1, :]
                + shift_rows(u, [um1]) * convw_ref[1:2, :] + u * convw_ref[2:3, :])
        for b, (r2, r1) in enumerate(zip(last_rows(u, 2), last_rows(u, 1))):
            cu_ref[q, b, 0:1, :] = r2
            cu_ref[q, b, 1:2, :] = r1
        yc = p_ref[q, :, CONV_DIM:2 * CONV_DIM] * conv
        mix_ref[q, :, 0:CONV_DIM] = _rms(yc, convn_ref[...])

        pr = p_ref[q, :, 3 * CONV_DIM:IN_PROJ]
        shifted = shift_rows(pr, [per_seq_rows(lambda b: cpr_ref[q, b:b + 1, :])])
        for b, r1 in enumerate(last_rows(pr, 1)):
            cpr_ref[q, b:b + 1, :] = r1
        m = pr + (shifted - pr) * mu_ref[...]
        o1, o2, o3 = RWKV_DIM, 2 * RWKV_DIM, 3 * RWKV_DIM
        r = m[:, :o1]
        k = m[:, o1:o2]
        v = m[:, o2:o3]
        wa_d = m[:, o3:o3 + LORA_DECAY + LORA_A]
        g_d = m[:, o3 + LORA_DECAY + LORA_A:]
        z = w0_ref[...] + mm(jnp.tanh(wa_d).astype(bf16), wdu_ref[...])
        a_lin = a0_ref[...] + mm(wa_d.astype(bf16), aup_ref[...])
        ops["g"] = mm(_sigmoid(g_d).astype(bf16), gup_ref[...])
        yield
        nz = -z
        softplus = jnp.maximum(nz, 0.0) + jnp.log(1.0 + jnp.exp(-jnp.abs(nz)))
        log_d = -jnp.exp(-softplus - 0.5)
        a = _sigmoid(a_lin)
        kk = k * kk_ref[...]
        kk = kk / jnp.maximum(jnp.sqrt(_group_sum_wide(kk * kk)), 1e-12)
        k = k * (1.0 + (a - 1.0) * ka_ref[...])
        yield
        cum = log_d
        step = 1
        while step < c_tok:
            cum = cum + jnp.where(t_pos >= step, pltpu.roll(cum, step, axis=0), 0.0)
            step *= 2
        ends = last_rows(cum, 1)
        total = per_seq_rows(lambda b: ends[b])
        yield
        ops["bonus"] = _group_sum_wide(r * k * rk_ref[...]) * v
        p_inv = jnp.exp(-cum)
        p_rest = jnp.exp(total - cum)
        beta = kk * a
        ops.update(at=-kk * jnp.exp(cum - log_d), rt=r * jnp.exp(cum), bt=beta * p_inv,
                   kt=k * p_inv, bc=beta * p_rest, kc=k * p_rest, v=v, p_total=jnp.exp(total))
        yield

    def seq_rows(x2, b):
        return [x2[h * rows + b * c_tok:h * rows + (b + 1) * c_tok] for h in range(HEADS_PER_GROUP)]

    def chains(slabs):
        units = [(q, j) for q in slabs for j in groups]
        idx = range(len(units))

        def stack(name, n):
            q, j = units[n]
            return _stack_heads(ops[q][name][:, j * LANES:(j + 1) * LANES])

        at2 = [stack("at", n) for n in idx]
        rt2 = [stack("rt", n) for n in idx]
        v2f = [stack("v", n) for n in idx]
        v2 = [x.astype(bf16) for x in v2f]
        ar = [jnp.concatenate([at2[n], rt2[n]], axis=0).astype(bf16) for n in idx]
        bk = [jnp.concatenate([stack("bt", n), stack("kt", n)], axis=0).astype(bf16) for n in idx]
        s_pack = [[load_state(q, b, j) for b in seqs] for q, j in units]
        scores = [_nt_dot(ar[n], bk[n]) for n in idx]
        yield
        from_a, from_r = [], []
        for n in idx:
            per_seq = [
                _nt_dot(jnp.concatenate(seq_rows(at2[n], b) + seq_rows(rt2[n], b),
                                        axis=0).astype(bf16),
                        _stack_heads(s_pack[n][b]).astype(bf16)) for b in seqs]
            pieces = lambda part: jnp.concatenate(
                [per_seq[b][(part * HEADS_PER_GROUP + h) * c_tok:
                            (part * HEADS_PER_GROUP + h + 1) * c_tok]
                 for h in range(HEADS_PER_GROUP) for b in seqs], axis=0)
            from_a.append(pieces(0))
            from_r.append(pieces(1))
        yield
        l_ab = [jnp.where(strict, scores[n][:c2, :c2], 0.0) for n in idx]
        l_b = [l_ab[n].astype(bf16) for n in idx]
        power = [mm(l_b[n], l_b[n]) for n in idx]
        yield
        w2 = [from_a[n]
              + mm(jnp.where(strict, scores[n][:c2, c2:], 0.0).astype(bf16), v2[n]) for n in idx]
        yield
        inv = [eye + l_ab[n] for n in idx]
        span = 2
        while span < c_tok:
            if 2 * span < c_tok:
                both = [mm(jnp.concatenate([inv[n], power[n]], axis=0).astype(bf16),
                           power[n].astype(bf16)) for n in idx]
                inv = [inv[n] + both[n][:c2] for n in idx]
                power = [both[n][c2:] for n in idx]
            else:
                inv = [inv[n] + mm(inv[n].astype(bf16), power[n].astype(bf16)) for n in idx]
            span *= 2
            yield
        u2f = [mm(inv[n].astype(bf16), w2[n].astype(bf16)) for n in idx]
        yield
        uv = [jnp.concatenate([u2f[n].astype(bf16), v2[n]], axis=0) for n in idx]
        y2 = [from_r[n]
              + mm(jnp.where(incl_wide, scores[n][c2:], 0.0).astype(bf16), uv[n]) for n in idx]
        for q in slabs:
            ops[q]["y"] = jnp.concatenate(
                [y2[n][:rows] + y2[n][rows:] for n in idx if units[n][0] == q], axis=1)
        yield
        for n, (q, j) in enumerate(units):
            bc2, kc2 = stack("bc", n), stack("kc", n)
            for b in seqs:
                lhs = jnp.concatenate(seq_rows(u2f[n], b) + seq_rows(v2f[n], b), axis=0)
                rhs = jnp.concatenate(seq_rows(bc2, b) + seq_rows(kc2, b), axis=0)
                s_new = _tn_dot(lhs.astype(bf16), rhs.astype(bf16))
                decay = ops[q]["p_total"][b * c_tok:b * c_tok + 1, j * LANES:(j + 1) * LANES]
                store_state(q, b, j,
                            s_pack[n][b] * decay + s_new[:HEAD_DIM] + s_new[HEAD_DIM:])
        yield

    def norm_gate(q, ops):
        y = ops["y"]
        mean = _group_sum_wide(y) * (1.0 / HEAD_DIM)
        yield
        ycen = y - mean
        var = _group_sum_wide(ycen * ycen) * (1.0 / HEAD_DIM)
        yield
        yn = ycen * lax.rsqrt(var + GN_EPS) * lnw_ref[...] + lnb_ref[...]
        mix_ref[q, :, CONV_DIM:CONV_DIM + RWKV_DIM] = (yn + ops["bonus"]) * ops["g"]
        yield

    ops = [dict() for _ in range(n_slab)]
    waves = [range(w, min(w + CHAIN_SLABS, n_slab)) for w in range(0, n_slab, CHAIN_SLABS)]
    for q in waves[0]:
        _drain(per_token(q, ops[q]))
    for w, wave in enumerate(waves):
        sides = [per_token(q, ops[q]) for q in (waves[w + 1] if w + 1 < len(waves) else ())]
        sides += [norm_gate(q, ops[q]) for q in (waves[w - 1] if w > 0 else ())]
        _interleave(chains(wave), sides)
    for q in waves[-1]:
        _drain(norm_gate(q, ops[q]))

    @pl.when(t_idx == n_t - 1)
    def _():
        convout_ref[...] = cu_ref[...]
        if not single_step:
            def unpack(q, b, j):
                store_unpacked(wkvout_ref, q, b, j, s_ref[q, b, j])
            for_each_state(unpack)


def _chunk_mixer(p, p_prev, conv0, wkv0, lp, *, c_tok):
    n_total, t_rows, _ = p.shape
    rows = SLAB_ROWS
    n_seq = rows // c_tok
    n_slab = min(LONG_SLABS if n_seq == 1 else SHORT_SLABS, n_total)
    assert c_tok & (c_tok - 1) == 0 and c_tok >= CONV_WIDTH - 1
    assert t_rows % rows == 0 and n_total % n_slab == 0 and (n_seq == 1 or t_rows == rows)
    const = lambda shape: pl.BlockSpec(shape, lambda i, t: (0,) * len(shape))
    state = lambda shape, **kw: pl.BlockSpec(
        (n_slab,) + shape, lambda i, t: (i,) + (0,) * len(shape), **kw)
    single_step = t_rows == rows
    wkv_spec = state((n_seq, RWKV_HEADS, HEAD_DIM, HEAD_DIM))
    in_specs = [
        pl.BlockSpec((n_slab, rows, IN_PROJ), lambda i, t: (i, t, 0)),
        state((n_seq, RWKV_PROJ)),
        state((n_seq, CONV_WIDTH - 1, CONV_DIM)),
        wkv_spec,
        const((1, RWKV_PROJ)),
        const((CONV_WIDTH, CONV_DIM)),
        const((1, CONV_DIM)),
        const((1, RWKV_DIM)),
        const((LORA_DECAY + LORA_A, RWKV_DIM)),
        const((1, RWKV_DIM)),
        const((LORA_DECAY + LORA_A, RWKV_DIM)),
        const((LORA_GATE, RWKV_DIM)),
        const((1, RWKV_DIM)),
        const((1, RWKV_DIM)),
        const((1, RWKV_DIM)),
        const((1, RWKV_DIM)),
        const((1, RWKV_DIM)),
    ]
    out_specs = [
        pl.BlockSpec((n_slab, rows, D_MODEL), lambda i, t: (i, t, 0)),
        state((n_seq, CONV_WIDTH - 1, CONV_DIM)),
        wkv_spec,
    ]
    out_shape = [
        jax.ShapeDtypeStruct((n_total, t_rows, D_MODEL), f32),
        jax.ShapeDtypeStruct(conv0.shape, f32),
        jax.ShapeDtypeStruct(wkv0.shape, f32),
    ]
    scratch = [
        pltpu.VMEM((1, 1, 1, SUBLANES, LANES) if single_step
                   else (n_slab, n_seq, HEAD_GROUPS, HEAD_DIM, LANES), f32),
        pltpu.VMEM((n_slab, n_seq, RWKV_PROJ), f32),
        pltpu.VMEM((n_slab, n_seq, CONV_WIDTH - 1, CONV_DIM), f32),
    ]
    return pl.pallas_call(
        functools.partial(_chunk_mixer_kernel, n_slab=n_slab, c_tok=c_tok,
                          single_step=single_step),
        grid=(n_total // n_slab, t_rows // rows),
        in_specs=in_specs,
        out_specs=out_specs,
        out_shape=out_shape,
        scratch_shapes=scratch,
        compiler_params=pltpu.CompilerParams(
            dimension_semantics=("arbitrary", "arbitrary"),
            vmem_limit_bytes=VMEM_LIMIT_BYTES),
        name="chunk_mixer",
    )(p, p_prev, conv0, wkv0,
      lp["mu"], lp["conv_w"], lp["conv_norm"], lp["w0"], lp["wdu"], lp["a0"], lp["aup"],
      lp["gup"], lp["k_k"], lp["k_a"], lp["r_k"], lp["ln_w"], lp["ln_b"])


def _ffn_kernel(x_ref, mix_ref, wout_ref, g2_ref, wg_ref, wu_ref, wd_ref, gf_ref, o_ref, *,
                final):
    x = x_ref[...] + jnp.dot(mix_ref[...].astype(bf16), wout_ref[...],
                             preferred_element_type=f32)
    hb = _rms(x, g2_ref[...]).astype(bf16)
    gate = jnp.dot(hb, wg_ref[...], preferred_element_type=f32)
    up = jnp.dot(hb, wu_ref[...], preferred_element_type=f32)
    act = gate * _sigmoid(gate) * up
    out = x + jnp.dot(act.astype(bf16), wd_ref[...], preferred_element_type=f32)
    o_ref[...] = _rms(out, gf_ref[...]) if final else out


def _ffn(x, mix, lp, final_norm, *, final):
    n, d = x.shape
    tile = min(ROW_TILE, n)
    layer = lp["layer"]
    resident = lambda rows, cols: pl.BlockSpec((None, rows, cols), lambda i: (layer, 0, 0),
                                               pipeline_mode=pl.Buffered(1))
    return pl.pallas_call(
        functools.partial(_ffn_kernel, final=final),
        grid=(n // tile,),
        in_specs=[
            pl.BlockSpec((tile, d), lambda i: (i, 0)),
            pl.BlockSpec((tile, d), lambda i: (i, 0)),
            resident(d, d),
            pl.BlockSpec((1, d), lambda i: (0, 0)),
            resident(d, D_FF),
            resident(d, D_FF),
            resident(D_FF, d),
            pl.BlockSpec((1, d), lambda i: (0, 0)),
        ],
        out_specs=pl.BlockSpec((tile, d), lambda i: (i, 0)),
        out_shape=jax.ShapeDtypeStruct((n, d), f32),
        compiler_params=pltpu.CompilerParams(
            dimension_semantics=("arbitrary",), vmem_limit_bytes=VMEM_LIMIT_BYTES),
        name="outproj_ffn",
    )(x, mix, lp["w_out"], lp["norm2"], lp["w_gate"], lp["w_up"], lp["w_down"], final_norm)


def _prepare_params(norm1, w_in, mu_shift, conv_w, conv_norm, w0, w_decay_up, a0, a_up, g_up,
                    k_k, k_a, r_k, ln_x_w, ln_x_b, w_out, norm2, w_gate, w_up, w_down,
                    final_norm):
    row = lambda a: a.reshape(1, -1)
    zeros_lora = jnp.zeros((LORA_DECAY, RWKV_DIM), f32)
    w_in_b = w_in.astype(bf16)
    stacked = dict(w_in=w_in_b, w_in_rwkv=w_in_b[:, :, 3 * CONV_DIM:],
                   w_out=w_out.astype(bf16), w_gate=w_gate.astype(bf16),
                   w_up=w_up.astype(bf16), w_down=w_down.astype(bf16))
    layers = []
    for l in range(DEPTH):
        layers.append(dict(
            stacked, layer=l, norm1=row(norm1[l]),
            mu=row(mu_shift[l]), conv_w=conv_w[l], conv_norm=row(conv_norm[l]),
            w0=row(w0[l]),
            wdu=jnp.concatenate([w_decay_up[l], zeros_lora], axis=0).astype(bf16),
            a0=row(a0[l]),
            aup=jnp.concatenate([zeros_lora, a_up[l]], axis=0).astype(bf16),
            gup=g_up[l].astype(bf16),
            k_k=row(k_k[l]), k_a=row(k_a[l]), r_k=row(r_k[l]),
            ln_w=row(ln_x_w[l]), ln_b=row(ln_x_b[l]), norm2=row(norm2[l])))
    return layers, row(final_norm)


def _trunk(x, shift0, conv0, wkv0, prepared):
    layers, final_norm = prepared
    bsz, t_len, d = x.shape
    c_tok = min(t_len, SLAB_ROWS)
    n_seq = SLAB_ROWS // c_tok
    slabs = bsz // n_seq
    assert t_len % c_tok == 0 and bsz % n_seq == 0 and (n_seq == 1 or t_len == c_tok)
    xf = x.reshape(bsz * t_len, d)
    shifts, convs, wkvs = [], [], []
    for l, lp in enumerate(layers):
        p = _inproj(xf, lp["norm1"], lp["w_in"], l, normalize=True)
        shifts.append(_rmsnorm(xf.reshape(bsz, t_len, d)[:, -1], lp["norm1"]))
        p_prev = _inproj(shift0[l], lp["norm1"], lp["w_in_rwkv"], l, normalize=False)
        mix, conv_new, wkv_new = _chunk_mixer(
            p.reshape(slabs, n_seq * t_len, IN_PROJ), p_prev.reshape(slabs, n_seq, RWKV_PROJ),
            conv0[l].reshape(slabs, n_seq, CONV_WIDTH - 1, CONV_DIM),
            wkv0[l].reshape(slabs, n_seq, RWKV_HEADS, HEAD_DIM, HEAD_DIM), lp, c_tok=c_tok)
        xf = _ffn(xf, mix.reshape(bsz * t_len, d), lp, final_norm, final=(l == DEPTH - 1))
        convs.append(conv_new.reshape(bsz, CONV_WIDTH - 1, CONV_DIM))
        wkvs.append(wkv_new.reshape(bsz, RWKV_HEADS, HEAD_DIM, HEAD_DIM))
    return xf.reshape(bsz, t_len, d), jnp.stack(shifts), jnp.stack(convs), jnp.stack(wkvs)


def kernel(x_prompt, x_sample, state_shift, state_conv, state_wkv, norm1, w_in, mu_shift, conv_w, conv_norm, w0, w_decay_up, a0, a_up, g_up, k_k, k_a, r_k, ln_x_w, ln_x_b, w_out, norm2, w_gate, w_up, w_down, final_norm):
    prepared = _prepare_params(norm1, w_in, mu_shift, conv_w, conv_norm, w0, w_decay_up, a0,
                               a_up, g_up, k_k, k_a, r_k, ln_x_w, ln_x_b, w_out, norm2,
                               w_gate, w_up, w_down, final_norm)
    bp = x_prompt.shape[0]
    dt = x_prompt.dtype
    p_shift0 = jnp.zeros((DEPTH, bp, D_MODEL), dt)
    p_conv0 = jnp.zeros((DEPTH, bp, CONV_WIDTH - 1, CONV_DIM), dt)
    p_wkv0 = jnp.zeros((DEPTH, bp, RWKV_HEADS, HEAD_DIM, HEAD_DIM), state_wkv.dtype)
    y_p, sh_p, cv_p, wk_p = _trunk(x_prompt, p_shift0, p_conv0, p_wkv0, prepared)
    y_s, sh_s, cv_s, wk_s = _trunk(x_sample, state_shift, state_conv, state_wkv, prepared)
    return (y_p, y_s, sh_p, cv_p, wk_p, sh_s, cv_s, wk_s)
```
